```python
import math
import jax, jax.numpy as jnp
from jax import lax
import numpy as np

D_MODEL = 2048
BATCH = 4
SEQ = 2048
DEPTH = 2
DEC_BATCH = 8
DEC_SEQ = 32
PAST_LEN = 4096

CHUNK = 64
EPS = 1e-6
N_HEADS = 8
QK_DIM = 64
V_DIM = 2 * QK_DIM
Q_WIDTH = N_HEADS * 2 * QK_DIM
ATT_WIDTH = N_HEADS * V_DIM
N_BUCKETS = 32
MAX_DISTANCE = 128
Q_BLOCK = 128
RG_WIDTH = 1024
RG_BLOCKS = 8
RG_BLOCK = RG_WIDTH // RG_BLOCKS
CONV_W = 4
RG_C = 8.0
S5_WIDTH = 1024
S5_GROUP = 16
S5_GROUPS = S5_WIDTH // S5_GROUP
S5_STATE = 64
PEER_HEADS = 8
N_KEYS = 128
N_EXPERTS = N_KEYS * N_KEYS
PEER_QDIM = 256
PEER_TOPK = 16
PEER_BLOCK = 128
N_BRANCH = 3
IN_COLS = 2 * Q_WIDTH + ATT_WIDTH + 2 * RG_WIDTH + S5_WIDTH + N_BRANCH * D_MODEL

kernel_name = "hybrid_streaming_encoder_step"


def rmsnorm(x, g):
    xf = x.astype(jnp.float32)
    y = xf * lax.rsqrt(jnp.mean(xf * xf, axis=-1, keepdims=True) + EPS)
    return (y * g.astype(jnp.float32)).astype(x.dtype)


def split_cols(proj):
    sizes = (Q_WIDTH, Q_WIDTH, ATT_WIDTH, RG_WIDTH, RG_WIDTH, S5_WIDTH, N_BRANCH * D_MODEL)
    out = []
    o = 0
    for s in sizes:
        out.append(proj[..., o:o + s])
        o += s
    return out


def rel_bucket(rel):
    half = N_BUCKETS // 2
    max_exact = half // 2
    ret = jnp.where(rel > 0, half, 0)
    n = jnp.abs(rel)
    nf = jnp.maximum(n, 1).astype(jnp.float32)
    large = max_exact + (jnp.log(nf / max_exact) / math.log(MAX_DISTANCE / max_exact)
                         * (half - max_exact)).astype(jnp.int32)
    large = jnp.minimum(large, half - 1)
    return ret + jnp.where(n < max_exact, n, large)


def diff_attn_core(q, k, v, qpos, kpos, rel_bias, lam):
    mask = (kpos[None, :] // CHUNK) <= (qpos[:, None] // CHUNK)
    bias = rel_bias.astype(jnp.float32)[rel_bucket(kpos[None, :] - qpos[:, None])]
    bias = jnp.transpose(bias, (2, 0, 1))
    logits = jnp.einsum('bqhmd,bkhmd->bmhqk', q, k, preferred_element_type=jnp.float32) * (QK_DIM ** -0.5)
    logits = jnp.where(mask, logits + bias[None, None], -jnp.inf)
    p = jax.nn.softmax(logits, axis=-1)
    w = p[:, 0] - lam * p[:, 1]
    return jnp.einsum('bhqk,bkhd->bqhd', w.astype(v.dtype), v)


def diff_attention_blocks(q, k, v, rel_bias, lam):
    B, T = q.shape[0], q.shape[1]
    pos = jnp.arange(T, dtype=jnp.int32)

    def block(i):
        start = i * Q_BLOCK
        qb = lax.dynamic_slice_in_dim(q, start, Q_BLOCK, axis=1)
        qpos = lax.dynamic_slice_in_dim(pos, start, Q_BLOCK)
        return diff_attn_core(qb, k, v, qpos, pos, rel_bias, lam)

    o = lax.map(block, jnp.arange(T // Q_BLOCK))
    return jnp.moveaxis(o, 0, 1).reshape(B, T, N_HEADS, V_DIM)


def causal_dwconv(x, buf, w, b):
    xp = jnp.concatenate([buf.astype(x.dtype), x], axis=1)
    y = lax.conv_general_dilated(xp, w[:, None, :].astype(x.dtype), window_strides=(1,), padding='VALID',
                                 dimension_numbers=('NWC', 'WIO', 'NWC'), feature_group_count=x.shape[-1])
    return y + b, xp[:, -(CONV_W - 1):]


def linear_scan(a, b, h0):
    b = b.at[:, 0].add(a[:, 0] * h0)

    def comb(e1, e2):
        a1, b1 = e1
        a2, b2 = e2
        return a1 * a2, a2 * b1 + b2

    _, h = lax.associative_scan(comb, (a, b), axis=1)
    return h


def rglru(x, h0, wa, ba, wx, bx, lam):
    B, T, _ = x.shape
    xb = x.reshape(B, T, RG_BLOCKS, RG_BLOCK)
    r = jax.nn.sigmoid(jnp.einsum('btgi,gij->btgj', xb, wa).reshape(B, T, RG_WIDTH) + ba)
    i = jax.nn.sigmoid(jnp.einsum('btgi,gij->btgj', xb, wx).reshape(B, T, RG_WIDTH) + bx)
    log_a = -RG_C * r.astype(jnp.float32) * jax.nn.softplus(-lam.astype(jnp.float32))
    a = jnp.exp(log_a)
    b = jnp.sqrt(-jnp.expm1(2.0 * log_a)) * (i.astype(jnp.float32) * x.astype(jnp.float32))
    return linear_scan(a, b, h0.astype(jnp.float32))


def s5_scan(u, s0_re, s0_im, lam_re, lam_im, log_step, b_re, b_im, c_re, c_im, d):
    B, T, _ = u.shape
    f32 = jnp.float32
    uf = u.astype(f32)
    ug = uf.reshape(B, T, S5_GROUPS, S5_GROUP)
    step = jnp.exp(log_step.astype(f32))[:, None]
    lr, li = lam_re.astype(f32), lam_im.astype(f32)
    mag = jnp.exp(lr * step)
    ab_re, ab_im = mag * jnp.cos(li * step), mag * jnp.sin(li * step)
    den = lr * lr + li * li
    nr, ni = ab_re - 1.0, ab_im
    co_re = (nr * lr + ni * li) / den
    co_im = (ni * lr - nr * li) / den
    bre, bim = b_re.astype(f32), b_im.astype(f32)
    bb_re = co_re[..., None] * bre - co_im[..., None] * bim
    bb_im = co_re[..., None] * bim + co_im[..., None] * bre
    bu_re = jnp.einsum('gph,btgh->btgp', bb_re, ug)
    bu_im = jnp.einsum('gph,btgh->btgp', bb_im, ug)
    s0r, s0i = s0_re.astype(f32), s0_im.astype(f32)
    bu_re = bu_re.at[:, 0].add(ab_re * s0r - ab_im * s0i)
    bu_im = bu_im.at[:, 0].add(ab_re * s0i + ab_im * s0r)
    ar = jnp.broadcast_to(ab_re, bu_re.shape)
    ai = jnp.broadcast_to(ab_im, bu_im.shape)

    def comb(e1, e2):
        a1r, a1i, b1r, b1i = e1
        a2r, a2i, b2r, b2i = e2
        return (a2r * a1r - a2i * a1i, a2r * a1i + a2i * a1r,
                a2r * b1r - a2i * b1i + b2r, a2r * b1i + a2i * b1r + b2i)

    _, _, xr, xi = lax.associative_scan(comb, (ar, ai, bu_re, bu_im), axis=1)
    y = (jnp.einsum('ghp,btgp->btgh', c_re.astype(f32), xr)
         - jnp.einsum('ghp,btgp->btgh', c_im.astype(f32), xi)).reshape(B, T, S5_WIDTH)
    y = y + d.astype(f32) * uf
    return y, xr[:, -1], xi[:, -1]


def peer(xn, wq, subkeys, u_tab, v_tab):
    B, T, D = xn.shape
    xt = xn.reshape(B * T, D)
    n = xt.shape[0]
    pad = (-n) % PEER_BLOCK
    xt = jnp.pad(xt, ((0, pad), (0, 0)))

    def block(xb):
        q = (xb @ wq).reshape(PEER_BLOCK, PEER_HEADS, 2, PEER_QDIM // 2)
        s = jnp.einsum('thpd,pnd->thpn', q, subkeys, preferred_element_type=jnp.float32)
        s1, i1 = lax.top_k(s[:, :, 0], PEER_TOPK)
        s2, i2 = lax.top_k(s[:, :, 1], PEER_TOPK)
        cand = (s1[..., :, None] + s2[..., None, :]).reshape(PEER_BLOCK, PEER_HEADS, PEER_TOPK * PEER_TOPK)
        cidx = (i1[..., :, None] * N_KEYS + i2[..., None, :]).reshape(PEER_BLOCK, PEER_HEADS, PEER_TOPK * PEER_TOPK)
        top, sel = lax.top_k(cand, PEER_TOPK)
        eidx = jnp.take_along_axis(cidx, sel, axis=-1)
        g = jax.nn.softmax(top, axis=-1)
        act = jax.nn.gelu(jnp.einsum('td,thkd->thk', xb, u_tab[eidx]).astype(jnp.float32), approximate=False) * g
        return jnp.einsum('thk,thkd->td', act.astype(xb.dtype), v_tab[eidx])

    y = lax.map(block, xt.reshape(-1, PEER_BLOCK, D))
    return y.reshape(-1, D)[:n].reshape(B, T, D)


def layer(x, l, lp, rel_bias, kv_cache, rg_h0, rg_buf0, s5_re0, s5_im0):
    B, T, _ = x.shape
    xn = rmsnorm(x, lp['norm1_g'])
    q, k, v, x_rg, x_gate, u_s5, gate_logits = split_cols(xn @ lp['w_in'])
    q = rmsnorm(q.reshape(B, T, N_HEADS, 2, QK_DIM), lp['qn_g'])
    k = rmsnorm(k.reshape(B, T, N_HEADS, 2, QK_DIM), lp['kn_g'])
    v = v.reshape(B, T, N_HEADS, V_DIM)
    lam_init = 0.8 - 0.6 * math.exp(-0.3 * l)
    f32 = jnp.float32
    lam = (jnp.exp(jnp.sum(lp['lam_q1'].astype(f32) * lp['lam_k1'].astype(f32)))
           - jnp.exp(jnp.sum(lp['lam_q2'].astype(f32) * lp['lam_k2'].astype(f32))) + lam_init)
    if kv_cache is None:
        o = diff_attention_blocks(q, k, v, rel_bias, lam)
    else:
        ck, cv = kv_cache
        P = ck.shape[1]
        k_all = jnp.concatenate([ck.reshape(B, P, N_HEADS, 2, QK_DIM).astype(k.dtype), k], axis=1)
        v_all = jnp.concatenate([cv.astype(v.dtype), v], axis=1)
        qpos = P + jnp.arange(T, dtype=jnp.int32)
        kpos = jnp.arange(P + T, dtype=jnp.int32)
        o = diff_attn_core(q, k_all, v_all, qpos, kpos, rel_bias, lam)
    o = rmsnorm(o, lp['subln_g']) * (1.0 - lam_init)
    br_att = o.reshape(B, T, ATT_WIDTH) @ lp['w_att_o']
    xc, rg_buf = causal_dwconv(x_rg, rg_buf0, lp['rg_conv_w'], lp['rg_conv_b'])
    h = rglru(xc, rg_h0, lp['rg_wa'], lp['rg_ba'], lp['rg_wx'], lp['rg_bx'], lp['rg_lambda'])
    br_rg = (h.astype(x.dtype) * jax.nn.gelu(x_gate, approximate=False)) @ lp['w_rg_o']
    ys, s5_re, s5_im = s5_scan(u_s5, s5_re0, s5_im0, lp['s5_lam_re'], lp['s5_lam_im'], lp['s5_log_step'],
                               lp['s5_b_re'], lp['s5_b_im'], lp['s5_c_re'], lp['s5_c_im'], lp['s5_d'])
    g = jax.nn.gelu(ys.astype(x.dtype), approximate=False)
    br_s5 = (g * jax.nn.sigmoid(g @ lp['s5_glu_w'] + lp['s5_glu_b'])) @ lp['w_s5_o']
    gates = jax.nn.sigmoid(gate_logits + lp['b_gate']).reshape(B, T, N_BRANCH, D_MODEL)
    merged = gates[:, :, 0] * br_att + gates[:, :, 1] * br_rg + gates[:, :, 2] * br_s5
    x = x + merged @ lp['w_out']
    x = x + peer(rmsnorm(x, lp['norm2_g']), lp['peer_wq'], lp['peer_subkeys'], lp['peer_u'], lp['peer_v'])
    k_rows = k.reshape(B, T, N_HEADS, 2 * QK_DIM)
    return x, k_rows, v, h[:, -1], rg_buf, s5_re, s5_im


def setup_inputs(seed: int = 0) -> dict:
    key = jax.random.key(seed)
    keys = jax.random.split(key, 64)
    cnt = [0]
    f32 = jnp.float32

    def nk():
        k = keys[cnt[0]]
        cnt[0] += 1
        return k

    def nrm(shape, scale):
        return jax.random.normal(nk(), shape, f32) * scale

    def gain(shape):
        return 1.0 + nrm(shape, 0.02)

    L, D = DEPTH, D_MODEL
    inp = {}
    inp['x_prompt'] = nrm((BATCH, SEQ, D), 1.0)
    inp['x_sample'] = nrm((DEC_BATCH, DEC_SEQ, D), 1.0)
    inp['cache_k'] = nrm((L, DEC_BATCH, PAST_LEN, N_HEADS, 2 * QK_DIM), 1.0)
    inp['cache_v'] = nrm((L, DEC_BATCH, PAST_LEN, N_HEADS, V_DIM), 1.0)
    inp['state_rg_h'] = nrm((L, DEC_BATCH, RG_WIDTH), 0.5)
    inp['state_rg_conv'] = nrm((L, DEC_BATCH, CONV_W - 1, RG_WIDTH), 1.0)
    inp['state_s5_re'] = nrm((L, DEC_BATCH, S5_GROUPS, S5_STATE), 0.1)
    inp['state_s5_im'] = nrm((L, DEC_BATCH, S5_GROUPS, S5_STATE), 0.1)
    inp['rel_bias'] = nrm((N_BUCKETS, N_HEADS), 0.2)
    inp['norm1_g'] = gain((L, D))
    inp['w_in'] = nrm((L, D, IN_COLS), D ** -0.5)
    inp['b_gate'] = nrm((L, N_BRANCH * D), 0.01)
    inp['qn_g'] = gain((L, QK_DIM))
    inp['kn_g'] = gain((L, QK_DIM))
    inp['lam_q1'] = nrm((L, QK_DIM), 0.1)
    inp['lam_k1'] = nrm((L, QK_DIM), 0.1)
    inp['lam_q2'] = nrm((L, QK_DIM), 0.1)
    inp['lam_k2'] = nrm((L, QK_DIM), 0.1)
    inp['subln_g'] = gain((L, V_DIM))
    inp['w_att_o'] = nrm((L, ATT_WIDTH, D), ATT_WIDTH ** -0.5)
    inp['rg_conv_w'] = nrm((L, CONV_W, RG_WIDTH), CONV_W ** -0.5)
    inp['rg_conv_b'] = nrm((L, RG_WIDTH), 0.01)
    inp['rg_wa'] = nrm((L, RG_BLOCKS, RG_BLOCK, RG_BLOCK), RG_BLOCK ** -0.5)
    inp['rg_ba'] = nrm((L, RG_WIDTH), 0.01)
    inp['rg_wx'] = nrm((L, RG_BLOCKS, RG_BLOCK, RG_BLOCK), RG_BLOCK ** -0.5)
    inp['rg_bx'] = nrm((L, RG_WIDTH), 0.01)
    ua = jax.random.uniform(nk(), (L, RG_WIDTH), f32, 0.9, 0.999)
    a = ua ** (1.0 / RG_C)
    inp['rg_lambda'] = jnp.log(a) - jnp.log1p(-a)
    inp['w_rg_o'] = nrm((L, RG_WIDTH, D), RG_WIDTH ** -0.5)
    inp['s5_lam_re'] = -0.5 + nrm((L, S5_GROUPS, S5_STATE), 0.01)
    inp['s5_lam_im'] = math.pi * jnp.arange(S5_STATE, dtype=f32) + nrm((L, S5_GROUPS, S5_STATE), 0.01)
    inp['s5_log_step'] = jax.random.uniform(nk(), (L, S5_GROUPS), f32, math.log(1e-3), math.log(1e-1))
    inp['s5_b_re'] = nrm((L, S5_GROUPS, S5_STATE, S5_GROUP), (2 * S5_GROUP) ** -0.5)
    inp['s5_b_im'] = nrm((L, S5_GROUPS, S5_STATE, S5_GROUP), (2 * S5_GROUP) ** -0.5)
    inp['s5_c_re'] = nrm((L, S5_GROUPS, S5_GROUP, S5_STATE), S5_STATE ** -0.5)
    inp['s5_c_im'] = nrm((L, S5_GROUPS, S5_GROUP, S5_STATE), S5_STATE ** -0.5)
    inp['s5_d'] = nrm((L, S5_WIDTH), 1.0)
    inp['s5_glu_w'] = nrm((L, S5_WIDTH, S5_WIDTH), S5_WIDTH ** -0.5)
    inp['s5_glu_b'] = nrm((L, S5_WIDTH), 0.01)
    inp['w_s5_o'] = nrm((L, S5_WIDTH, D), S5_WIDTH ** -0.5)
    inp['w_out'] = nrm((L, D, D), D ** -0.5)
    inp['norm2_g'] = gain((L, D))
    inp['peer_wq'] = nrm((L, D, PEER_HEADS * PEER_QDIM), D ** -0.5)
    inp['peer_subkeys'] = nrm((L, 2, N_KEYS, PEER_QDIM // 2), (PEER_QDIM // 2) ** -0.5)
    inp['peer_u'] = nrm((L, N_EXPERTS, D), D ** -0.5)
    inp['peer_v'] = nrm((L, N_EXPERTS, D), PEER_HEADS ** -0.5)
    return inp


def reference(x_prompt, x_sample, cache_k, cache_v, state_rg_h, state_rg_conv, state_s5_re, state_s5_im,
              rel_bias, norm1_g, w_in, b_gate, qn_g, kn_g, lam_q1, lam_k1, lam_q2, lam_k2, subln_g, w_att_o,
              rg_conv_w, rg_conv_b, rg_wa, rg_ba, rg_wx, rg_bx, rg_lambda, w_rg_o,
              s5_lam_re, s5_lam_im, s5_log_step, s5_b_re, s5_b_im, s5_c_re, s5_c_im, s5_d, s5_glu_w, s5_glu_b,
              w_s5_o, w_out, norm2_g, peer_wq, peer_subkeys, peer_u, peer_v):
    def layer_params(l):
        return dict(norm1_g=norm1_g[l], w_in=w_in[l], b_gate=b_gate[l], qn_g=qn_g[l], kn_g=kn_g[l],
                    lam_q1=lam_q1[l], lam_k1=lam_k1[l], lam_q2=lam_q2[l], lam_k2=lam_k2[l],
                    subln_g=subln_g[l], w_att_o=w_att_o[l], rg_conv_w=rg_conv_w[l], rg_conv_b=rg_conv_b[l],
                    rg_wa=rg_wa[l], rg_ba=rg_ba[l], rg_wx=rg_wx[l], rg_bx=rg_bx[l], rg_lambda=rg_lambda[l],
                    w_rg_o=w_rg_o[l], s5_lam_re=s5_lam_re[l], s5_lam_im=s5_lam_im[l],
                    s5_log_step=s5_log_step[l], s5_b_re=s5_b_re[l], s5_b_im=s5_b_im[l],
                    s5_c_re=s5_c_re[l], s5_c_im=s5_c_im[l], s5_d=s5_d[l], s5_glu_w=s5_glu_w[l],
                    s5_glu_b=s5_glu_b[l], w_s5_o=w_s5_o[l], w_out=w_out[l], norm2_g=norm2_g[l],
                    peer_wq=peer_wq[l], peer_subkeys=peer_subkeys[l], peer_u=peer_u[l], peer_v=peer_v[l])

    Bp = x_prompt.shape[0]
    yp = x_prompt
    st_p = []
    for l in range(DEPTH):
        yp, *st = layer(yp, l, layer_params(l), rel_bias, None,
                        jnp.zeros((Bp, RG_WIDTH), jnp.float32),
                        jnp.zeros((Bp, CONV_W - 1, RG_WIDTH), x_prompt.dtype),
                        jnp.zeros((Bp, S5_GROUPS, S5_STATE), jnp.float32),
                        jnp.zeros((Bp, S5_GROUPS, S5_STATE), jnp.float32))
        st_p.append(st)

    ys = x_sample
    st_s = []
    for l in range(DEPTH):
        ys, *st = layer(ys, l, layer_params(l), rel_bias, (cache_k[l], cache_v[l]),
                        state_rg_h[l], state_rg_conv[l], state_s5_re[l], state_s5_im[l])
        st_s.append(st)

    k_p = jnp.stack([s[0] for s in st_p])
    v_p = jnp.stack([s[1] for s in st_p])
    rgh_p = jnp.stack([s[2] for s in st_p])
    rgc_p = jnp.stack([s[3] for s in st_p])
    s5r_p = jnp.stack([s[4] for s in st_p])
    s5i_p = jnp.stack([s[5] for s in st_p])
    k_s = jnp.stack([s[0] for s in st_s])
    v_s = jnp.stack([s[1] for s in st_s])
    rgh_s = jnp.stack([s[2] for s in st_s])
    rgc_s = jnp.stack([s[3] for s in st_s])
    s5r_s = jnp.stack([s[4] for s in st_s])
    s5i_s = jnp.stack([s[5] for s in st_s])
    return (yp, ys, k_p, v_p, rgh_p, rgc_p, s5r_p, s5i_p, k_s, v_s, rgh_s, rgc_s, s5r_s, s5i_s)
```

```python
import functools
import math

import jax
import jax.numpy as jnp
from jax import lax
from jax.experimental import pallas as pl
from jax.experimental.pallas import tpu as pltpu

f32 = jnp.float32
bf16 = jnp.bfloat16

D_MODEL = 2048
CHUNK = 64
EPS = 1e-6
N_HEADS = 8
QK_DIM = 64
V_DIM = 128
Q_WIDTH = 1024
ATT_WIDTH = 1024
N_BUCKETS = 32
MAX_DISTANCE = 128
RG_WIDTH = 1024
RG_BLOCKS = 8
RG_BLOCK = 128
CONV_W = 4
RG_C = 8.0
S5_WIDTH = 1024
S5_GROUP = 16
S5_GROUPS = 64
S5_STATE = 64
S5_LANES = S5_GROUPS * S5_STATE
PEER_HEADS = 8
N_KEYS = 128
N_EXPERTS = N_KEYS * N_KEYS
PEER_TOPK = 16
N_BRANCH = 3
IN_COLS = 2 * Q_WIDTH + ATT_WIDTH + 2 * RG_WIDTH + S5_WIDTH + N_BRANCH * D_MODEL

SUBLANES = 8
LANES = 128
VMEM_LIMIT = 56 * 1024 * 1024

NSUB = SUBLANES
SUBLEN = 32
TB = NSUB * SUBLEN
MASK_NEG = -1e30

COL_Q, COL_K, COL_V, COL_RG, COL_GATE, COL_S5 = 0, 1, 2, 3, 4, 5
COL_GATES = 6 * 1024


def _cparams(sem):
    return pltpu.CompilerParams(dimension_semantics=sem, vmem_limit_bytes=VMEM_LIMIT)


def _gelu(x):
    return 0.5 * x * (1.0 + lax.erf(x * (2.0 ** -0.5)))


def _dot_nt(a, b):
    return lax.dot_general(a, b, (((1,), (1,)), ((), ())), preferred_element_type=f32)


def _dot_tn(a, b):
    return lax.dot_general(a, b, (((0,), (0,)), ((), ())), preferred_element_type=f32)


def _norm_matmul_body(x_ref, g_ref, w_ref, o_ref, *rest, emit_xn):
    xn_ref = rest[-1]

    @pl.when(pl.program_id(1) == 0)
    def _():
        x = x_ref[...]
        ms = jnp.mean(x * x, axis=-1, keepdims=True)
        xn = (x * lax.rsqrt(ms + EPS) * g_ref[...]).astype(bf16)
        xn_ref[...] = xn
        if emit_xn:
            rest[0][...] = xn

    o_ref[...] = jnp.dot(xn_ref[...], w_ref[...], preferred_element_type=f32).astype(o_ref.dtype)


def norm_matmul(x, g, w, *, tm, tn, out_dtype, emit_xn=False):
    m, d = x.shape
    n = w.shape[1]
    out_shape = [jax.ShapeDtypeStruct((m, n), out_dtype)]
    out_specs = [pl.BlockSpec((tm, tn), lambda i, j: (i, j))]
    if emit_xn:
        out_shape.append(jax.ShapeDtypeStruct((m, d), bf16))
        out_specs.append(pl.BlockSpec((tm, d), lambda i, j: (i, 0)))
    res = pl.pallas_call(
        functools.partial(_norm_matmul_body, emit_xn=emit_xn),
        grid=(m // tm, n // tn),
        in_specs=[
            pl.BlockSpec((tm, d), lambda i, j: (i, 0)),
            pl.BlockSpec((1, d), lambda i, j: (0, 0)),
            pl.BlockSpec((d, tn), lambda i, j: (0, j)),
        ],
        out_specs=out_specs,
        out_shape=out_shape,
        scratch_shapes=[pltpu.VMEM((tm, d), bf16)],
        compiler_params=_cparams(("parallel", "arbitrary")),
        name="norm_matmul",
    )(x, g.reshape(1, d), w)
    return res if emit_xn else res[0]


def _group_mean_sq(x, ones_bd):
    sq = x * x
    hi = sq.astype(bf16)
    lo = (sq - hi.astype(f32)).astype(bf16)
    s = jnp.dot(hi, ones_bd, preferred_element_type=f32) + jnp.dot(lo, ones_bd, preferred_element_type=f32)
    return s * (1.0 / QK_DIM)


def _qknorm_body(q_ref, k_ref, qg_ref, kg_ref, qo_ref, ko_ref):
    r = lax.broadcasted_iota(jnp.int32, (LANES, LANES), 0) // QK_DIM
    c = lax.broadcasted_iota(jnp.int32, (LANES, LANES), 1) // QK_DIM
    ones_bd = (r == c).astype(bf16)
    for t in range(Q_WIDTH // LANES):
        sl = slice(t * LANES, (t + 1) * LANES)
        q = q_ref[:, sl]
        k = k_ref[:, sl]
        qn = q * lax.rsqrt(_group_mean_sq(q, ones_bd) + EPS) * qg_ref[:, sl]
        kn = k * lax.rsqrt(_group_mean_sq(k, ones_bd) + EPS) * kg_ref[:, sl]
        qo_ref[:, sl] = (qn * (QK_DIM ** -0.5)).astype(bf16)
        ko_ref[:, sl] = kn


def qk_norm(proj, qn_g, kn_g, *, tm):
    m = proj.shape[0]
    qg = jnp.tile(qn_g, Q_WIDTH // QK_DIM).reshape(1, Q_WIDTH)
    kg = jnp.tile(kn_g, Q_WIDTH // QK_DIM).reshape(1, Q_WIDTH)
    return pl.pallas_call(
        _qknorm_body,
        grid=(m // tm,),
        in_specs=[
            pl.BlockSpec((tm, Q_WIDTH), lambda i: (i, COL_Q)),
            pl.BlockSpec((tm, Q_WIDTH), lambda i: (i, COL_K)),
            pl.BlockSpec((1, Q_WIDTH), lambda i: (0, 0)),
            pl.BlockSpec((1, Q_WIDTH), lambda i: (0, 0)),
        ],
        out_specs=[
            pl.BlockSpec((tm, Q_WIDTH), lambda i: (i, 0)),
            pl.BlockSpec((tm, Q_WIDTH), lambda i: (i, 0)),
        ],
        out_shape=[jax.ShapeDtypeStruct((m, Q_WIDTH), bf16), jax.ShapeDtypeStruct((m, Q_WIDTH), f32)],
        compiler_params=_cparams(("parallel",)),
        name="qk_norm",
    )(proj, proj, qg, kg)


def _bucket(rel):
    half = N_BUCKETS // 2
    max_exact = half // 2
    ret = jnp.where(rel > 0, half, 0)
    n = jnp.abs(rel)
    nf = jnp.maximum(n, 1).astype(f32)
    large = max_exact + (jnp.log(nf / max_exact) / math.log(MAX_DISTANCE / max_exact)
                         * (half - max_exact)).astype(jnp.int32)
    large = jnp.minimum(large, half - 1)
    return ret + jnp.where(n < max_exact, n, large)


def _bias_from_bucket(bucket, rb_ref, h):
    out = jnp.zeros(bucket.shape, f32)
    for j in range(N_BUCKETS):
        out = jnp.where(bucket == j, rb_ref[j, h], out)
    return out


def _block_time(r):
    return (r % NSUB) * SUBLEN + r // NSUB


def _bias_body(rb_ref, pb_ref, sbc_ref, sbn_ref, *, past):
    h = pl.program_id(0)
    tq = _block_time(lax.broadcasted_iota(jnp.int32, (TB, TB), 0))
    tk = _block_time(lax.broadcasted_iota(jnp.int32, (TB, TB), 1))
    b0 = _bias_from_bucket(_bucket(tk - tq), rb_ref, h)
    pb_ref[0] = jnp.where(tk // CHUNK <= tq // CHUNK, b0, MASK_NEG)
    pb_ref[1] = _bias_from_bucket(_bucket(tk - tq - TB), rb_ref, h)
    nq = sbc_ref.shape[0]
    qpos = past + lax.broadcasted_iota(jnp.int32, (nq, past), 0)
    kpos = lax.broadcasted_iota(jnp.int32, (nq, past), 1)
    sbc_ref[...] = _bias_from_bucket(_bucket(kpos - qpos), rb_ref, h)
    qn = lax.broadcasted_iota(jnp.int32, (nq, nq), 0)
    kn = lax.broadcasted_iota(jnp.int32, (nq, nq), 1)
    sbn_ref[...] = _bias_from_bucket(_bucket(kn - qn), rb_ref, h)


def bias_tiles(rel_bias, past):
    return pl.pallas_call(
        functools.partial(_bias_body, past=past),
        grid=(N_HEADS,),
        in_specs=[pl.BlockSpec(memory_space=pltpu.SMEM)],
        out_specs=[
            pl.BlockSpec((None, 2, TB, TB), lambda h: (h, 0, 0, 0)),
            pl.BlockSpec((None, SUBLEN, past), lambda h: (h, 0, 0)),
            pl.BlockSpec((None, SUBLEN, SUBLEN), lambda h: (h, 0, 0)),
        ],
        out_shape=[
            jax.ShapeDtypeStruct((N_HEADS, 2, TB, TB), f32),
            jax.ShapeDtypeStruct((N_HEADS, SUBLEN, past), f32),
            jax.ShapeDtypeStruct((N_HEADS, SUBLEN, SUBLEN), f32),
        ],
        compiler_params=_cparams(("parallel",)),
        name="bias_tiles",
    )(rel_bias)


def _stack_maps(q):
    lane = lax.broadcasted_iota(jnp.int32, q.shape, 1)
    zero = jnp.zeros_like(q)
    return jnp.concatenate([jnp.where(lane < QK_DIM, q, zero), jnp.where(lane >= QK_DIM, q, zero)], axis=0)


def _online_softmax_step(s, v, m_prev, l_prev, acc_prev):
    m_new = jnp.maximum(m_prev, jnp.max(s, axis=-1, keepdims=True))
    alpha = jnp.exp(m_prev - m_new)
    p = jnp.exp(s - m_new)
    l_new = alpha * l_prev + jnp.sum(p, axis=-1, keepdims=True)
    acc_new = alpha * acc_prev + jnp.dot(p.astype(bf16), v, preferred_element_type=f32)
    return m_new, l_new, acc_new


def _diff_combine(l, acc, lam, g, nq, lam_init):
    o = acc[:nq] / l[:nq] - lam * (acc[nq:] / l[nq:])
    ms = jnp.mean(o * o, axis=-1, keepdims=True)
    return o * lax.rsqrt(ms + EPS) * g * (1.0 - lam_init)


def _prompt_attn_body(rb_ref, lam_ref, q_ref, k_ref, v_ref, pb_ref, g_ref, o_ref, m_ref, l_ref, acc_ref,
                      *, lam_init):
    h = pl.program_id(1)
    qi = pl.program_id(2)
    kj = pl.program_id(3)

    @pl.when(kj == 0)
    def _():
        m_ref[...] = jnp.full(m_ref.shape, -jnp.inf, f32)
        l_ref[...] = jnp.zeros(l_ref.shape, f32)
        acc_ref[...] = jnp.zeros(acc_ref.shape, f32)

    @pl.when(kj <= qi)
    def _():
        qq = _stack_maps(q_ref[...])
        s = _dot_nt(qq, k_ref[...].astype(bf16))
        far = rb_ref[N_BUCKETS // 2 - 1, h]
        bias = jnp.where(qi - kj >= 2, far, pb_ref[...])
        s = (s.reshape(2, TB, TB) + bias[None]).reshape(2 * TB, TB)
        m, l, acc = _online_softmax_step(s, v_ref[...].astype(bf16), m_ref[...], l_ref[...], acc_ref[...])
        m_ref[...] = m
        l_ref[...] = l
        acc_ref[...] = acc

    @pl.when(kj == pl.num_programs(3) - 1)
    def _():
        o = _diff_combine(l_ref[...], acc_ref[...], lam_ref[0], g_ref[...], TB, lam_init)
        o_ref[...] = o.astype(o_ref.dtype)


def prompt_attention(qn, kn, proj, pbias, rel_bias, lam, subln_g, *, n_seq, blocks_per_seq, lam_init):
    m = qn.shape[0]
    nb = blocks_per_seq
    vcol = COL_V * (1024 // V_DIM)
    return pl.pallas_call(
        functools.partial(_prompt_attn_body, lam_init=lam_init),
        grid=(n_seq, N_HEADS, nb, nb),
        in_specs=[
            pl.BlockSpec(memory_space=pltpu.SMEM),
            pl.BlockSpec(memory_space=pltpu.SMEM),
            pl.BlockSpec((TB, V_DIM), lambda b, h, i, j: (b * nb + i, h)),
            pl.BlockSpec((TB, V_DIM), lambda b, h, i, j: (b * nb + jnp.minimum(i, j), h)),
            pl.BlockSpec((TB, V_DIM), lambda b, h, i, j: (b * nb + jnp.minimum(i, j), vcol + h)),
            pl.BlockSpec((None, None, TB, TB), lambda b, h, i, j: (h, jnp.clip(i - j, 0, 1), 0, 0)),
            pl.BlockSpec((1, V_DIM), lambda b, h, i, j: (0, 0)),
        ],
        out_specs=pl.BlockSpec((TB, V_DIM), lambda b, h, i, j: (b * nb + i, h)),
        out_shape=jax.ShapeDtypeStruct((m, ATT_WIDTH), bf16),
        scratch_shapes=[pltpu.VMEM((2 * TB, 1), f32), pltpu.VMEM((2 * TB, 1), f32),
                        pltpu.VMEM((2 * TB, V_DIM), f32)],
        compiler_params=_cparams(("parallel", "parallel", "parallel", "arbitrary")),
        name="prompt_attention",
    )(rel_bias, lam, qn, kn, proj, pbias, subln_g.reshape(1, V_DIM))


def _sample_attn_body(lam_ref, q_ref, kn_ref, vn_ref, ck_ref, cv_ref, sbc_ref, sbn_ref, g_ref, o_ref,
                      m_ref, l_ref, acc_ref, *, lam_init):
    kj = pl.program_id(1)
    nq = q_ref.shape[0]
    for h in range(N_HEADS):
        sl = slice(h * V_DIM, (h + 1) * V_DIM)
        qq = _stack_maps(q_ref[:, sl])

        @pl.when(kj == 0)
        def _():
            s = _dot_nt(qq, kn_ref[:, sl].astype(bf16))
            s = (s.reshape(2, nq, nq) + sbn_ref[h][None]).reshape(2 * nq, nq)
            m = jnp.max(s, axis=-1, keepdims=True)
            p = jnp.exp(s - m)
            m_ref[h] = m
            l_ref[h] = jnp.sum(p, axis=-1, keepdims=True)
            acc_ref[h] = jnp.dot(p.astype(bf16), vn_ref[:, sl].astype(bf16), preferred_element_type=f32)

        s = _dot_nt(qq, ck_ref[:, sl].astype(bf16))
        ck = s.shape[1]
        s = (s.reshape(2, nq, ck) + sbc_ref[h][None]).reshape(2 * nq, ck)
        m, l, acc = _online_softmax_step(s, cv_ref[:, sl].astype(bf16), m_ref[h], l_ref[h], acc_ref[h])
        m_ref[h] = m
        l_ref[h] = l
        acc_ref[h] = acc

        @pl.when(kj == pl.num_programs(1) - 1)
        def _():
            o = _diff_combine(l_ref[h], acc_ref[h], lam_ref[0], g_ref[...], nq, lam_init)
            o_ref[:, sl] = o.astype(o_ref.dtype)


def sample_attention(q_s, kn_s, v_s, cache_k, cache_v, sbc, sbn, lam, subln_g, *, lam_init):
    nb, nq, _ = q_s.shape
    past = cache_k.shape[1]
    chunk = min(past, 1024)
    return pl.pallas_call(
        functools.partial(_sample_attn_body, lam_init=lam_init),
        grid=(nb, past // chunk),
        in_specs=[
            pl.BlockSpec(memory_space=pltpu.SMEM),
            pl.BlockSpec((None, nq, ATT_WIDTH), lambda b, j: (b, 0, 0)),
            pl.BlockSpec((None, nq, ATT_WIDTH), lambda b, j: (b, 0, 0)),
            pl.BlockSpec((None, nq, ATT_WIDTH), lambda b, j: (b, 0, 0)),
            pl.BlockSpec((None, chunk, ATT_WIDTH), lambda b, j: (b, j, 0)),
            pl.BlockSpec((None, chunk, ATT_WIDTH), lambda b, j: (b, j, 0)),
            pl.BlockSpec((N_HEADS, nq, chunk), lambda b, j: (0, 0, j)),
            pl.BlockSpec((N_HEADS, nq, nq), lambda b, j: (0, 0, 0)),
            pl.BlockSpec((1, V_DIM), lambda b, j: (0, 0)),
        ],
        out_specs=pl.BlockSpec((None, nq, ATT_WIDTH), lambda b, j: (b, 0, 0)),
        out_shape=jax.ShapeDtypeStruct((nb, nq, ATT_WIDTH), bf16),
        scratch_shapes=[pltpu.VMEM((N_HEADS, 2 * nq, 1), f32), pltpu.VMEM((N_HEADS, 2 * nq, 1), f32),
                        pltpu.VMEM((N_HEADS, 2 * nq, V_DIM), f32)],
        compiler_params=_cparams(("parallel", "arbitrary")),
        name="sample_attention",
    )(lam, q_s, kn_s, v_s, cache_k, cache_v, sbc, sbn, subln_g.reshape(1, V_DIM))


def _chain_carries(c0, mult, add):
    rows = [c0]
    c = c0
    for s in range(NSUB):
        c = mult[s:s + 1] * c + add[s:s + 1]
        if s + 1 < NSUB:
            rows.append(c)
    return jnp.concatenate(rows, axis=0), c


def _rglru_body(xrg_ref, xgate_ref, head_s_ref, h0_s_ref, cw_ref, cb_ref, wa_ref, wx_ref, ba_ref, bx_ref,
                lam_ref, hg_ref, htail_ref, ctail_ref, xbuf_ref, a_ref, b_ref, hcar_ref, ptail_ref,
                *, blocks_per_seq, n_prompt_blocks):
    g = pl.program_id(0)
    is_sample = g >= n_prompt_blocks
    hl = (CONV_W - 1) * NSUB

    @pl.when(g % blocks_per_seq == 0)
    def _():
        hcar_ref[...] = jnp.zeros(hcar_ref.shape, f32)
        ptail_ref[...] = jnp.zeros(ptail_ref.shape, f32)

    x = xrg_ref[...]
    tail = x[TB - hl:]
    sub = lax.broadcasted_iota(jnp.int32, (NSUB, RG_WIDTH), 0)
    for k in range(CONV_W - 1):
        rs = slice(k * NSUB, (k + 1) * NSUB)
        prompt_head = jnp.where(sub == 0, pltpu.roll(ptail_ref[rs, :], 1, 0), pltpu.roll(tail[rs], 1, 0))
        xbuf_ref[rs, :] = jnp.where(is_sample, head_s_ref[rs, :], prompt_head)
    xbuf_ref[hl:, :] = x
    ptail_ref[...] = tail
    ctail_ref[...] = tail

    xc = cb_ref[...] + cw_ref[CONV_W - 1:CONV_W, :] * x
    for j in range(1, CONV_W):
        xc = xc + cw_ref[CONV_W - 1 - j:CONV_W - j, :] * xbuf_ref[hl - j * NSUB:hl - j * NSUB + TB, :]

    xcb = xc.astype(bf16)
    r_parts, i_parts = [], []
    for blk in range(RG_BLOCKS):
        sl = slice(blk * RG_BLOCK, (blk + 1) * RG_BLOCK)
        r_parts.append(jnp.dot(xcb[:, sl], wa_ref[blk], preferred_element_type=f32))
        i_parts.append(jnp.dot(xcb[:, sl], wx_ref[blk], preferred_element_type=f32))
    r = jax.nn.sigmoid(jnp.concatenate(r_parts, axis=1) + ba_ref[...])
    ig = jax.nn.sigmoid(jnp.concatenate(i_parts, axis=1) + bx_ref[...])
    nl = -lam_ref[...]
    softplus = jnp.maximum(nl, 0.0) + jnp.log1p(jnp.exp(-jnp.abs(nl)))
    a = jnp.exp(-RG_C * r * softplus)
    a_ref[...] = a
    b_ref[...] = jnp.sqrt(1.0 - a * a) * (ig * xc)

    def rows(i):
        return pl.ds(pl.multiple_of(i * NSUB, NSUB), NSUB)

    def local(i, carry):
        p, e = carry
        ai = a_ref[rows(i), :]
        return p * ai, ai * e + b_ref[rows(i), :]

    p, e = lax.fori_loop(0, SUBLEN, local,
                         (jnp.ones((NSUB, RG_WIDTH), f32), jnp.zeros((NSUB, RG_WIDTH), f32)), unroll=4)
    entry, exit_state = _chain_carries(hcar_ref[...], p, e)
    hcar_ref[...] = exit_state
    h_init = jnp.where(is_sample, h0_s_ref[...], entry)

    def step(i, hprev):
        hcur = a_ref[rows(i), :] * hprev + b_ref[rows(i), :]
        b_ref[rows(i), :] = hcur
        return hcur

    h_last = lax.fori_loop(0, SUBLEN, step, h_init, unroll=4)
    htail_ref[...] = h_last
    hg_ref[...] = (b_ref[...] * _gelu(xgate_ref[...])).astype(hg_ref.dtype)


def rglru_mixer(proj, head_s, h0_s, conv_w, conv_b, wa, wx, ba, bx, lam, *, blocks_per_seq, n_prompt_blocks):
    m = proj.shape[0]
    nblk = m // TB
    hl = (CONV_W - 1) * NSUB
    row = lambda v: v.reshape(1, RG_WIDTH)
    full = lambda shape: pl.BlockSpec(shape, lambda g: (0,) * len(shape))
    return pl.pallas_call(
        functools.partial(_rglru_body, blocks_per_seq=blocks_per_seq, n_prompt_blocks=n_prompt_blocks),
        grid=(nblk,),
        in_specs=[
            pl.BlockSpec((TB, RG_WIDTH), lambda g: (g, COL_RG)),
            pl.BlockSpec((TB, RG_WIDTH), lambda g: (g, COL_GATE)),
            full((hl, RG_WIDTH)),
            full((NSUB, RG_WIDTH)),
            full((CONV_W, RG_WIDTH)),
            full((1, RG_WIDTH)),
            full((RG_BLOCKS, RG_BLOCK, RG_BLOCK)),
            full((RG_BLOCKS, RG_BLOCK, RG_BLOCK)),
            full((1, RG_WIDTH)),
            full((1, RG_WIDTH)),
            full((1, RG_WIDTH)),
        ],
        out_specs=[
            pl.BlockSpec((TB, RG_WIDTH), lambda g: (g, 0)),
            pl.BlockSpec((None, NSUB, RG_WIDTH), lambda g: (g, 0, 0)),
            pl.BlockSpec((None, hl, RG_WIDTH), lambda g: (g, 0, 0)),
        ],
        out_shape=[
            jax.ShapeDtypeStruct((m, RG_WIDTH), bf16),
            jax.ShapeDtypeStruct((nblk, NSUB, RG_WIDTH), f32),
            jax.ShapeDtypeStruct((nblk, hl, RG_WIDTH), f32),
        ],
        scratch_shapes=[
            pltpu.VMEM((hl + TB, RG_WIDTH), f32),
            pltpu.VMEM((TB, RG_WIDTH), f32),
            pltpu.VMEM((TB, RG_WIDTH), f32),
            pltpu.VMEM((1, RG_WIDTH), f32),
            pltpu.VMEM((hl, RG_WIDTH), f32),
        ],
        compiler_params=_cparams(("arbitrary",)),
        name="rglru_mixer",
    )(proj, proj, head_s, h0_s, conv_w, row(conv_b), wa.astype(bf16), wx.astype(bf16), row(ba), row(bx),
      row(lam))


S5_CHUNKS = 8
S5_CH_IN = S5_WIDTH // S5_CHUNKS
S5_CH_ST = S5_LANES // S5_CHUNKS
S5_SCAN_LANES = 512


def _cmul(ar, ai, br, bi):
    return ar * br - ai * bi, ar * bi + ai * br


def _s5_body(u_ref, s0r_ref, s0i_ref, abr_ref, abi_ref, bdr_ref, bdi_ref, cdr_ref, cdi_ref, d_ref, gw_ref,
             gb_ref, gs_ref, str_ref, sti_ref, xr_ref, xi_ref, car_ref, cai_ref,
             *, blocks_per_seq, n_prompt_blocks):
    g = pl.program_id(0)
    is_sample = g >= n_prompt_blocks

    @pl.when(g % blocks_per_seq == 0)
    def _():
        car_ref[...] = jnp.zeros(car_ref.shape, f32)
        cai_ref[...] = jnp.zeros(cai_ref.shape, f32)

    u = u_ref[...]
    ub = u.astype(bf16)
    for c in range(S5_CHUNKS):
        ci = slice(c * S5_CH_IN, (c + 1) * S5_CH_IN)
        cs = slice(c * S5_CH_ST, (c + 1) * S5_CH_ST)
        xr_ref[:, cs] = jnp.dot(ub[:, ci], bdr_ref[c], preferred_element_type=f32)
        xi_ref[:, cs] = jnp.dot(ub[:, ci], bdi_ref[c], preferred_element_type=f32)

    def rows(i):
        return pl.ds(pl.multiple_of(i * NSUB, NSUB), NSUB)

    for c in range(S5_LANES // S5_SCAN_LANES):
        ls = slice(c * S5_SCAN_LANES, (c + 1) * S5_SCAN_LANES)
        ar1 = abr_ref[:, ls]
        ai1 = abi_ref[:, ls]
        ar = jnp.broadcast_to(ar1, (NSUB, S5_SCAN_LANES))
        ai = jnp.broadcast_to(ai1, (NSUB, S5_SCAN_LANES))

        def local(i, carry, ls=ls, ar=ar, ai=ai):
            er, ei = carry
            nr, ni = _cmul(ar, ai, er, ei)
            return nr + xr_ref[rows(i), ls], ni + xi_ref[rows(i), ls]

        zero = jnp.zeros((NSUB, S5_SCAN_LANES), f32)
        er, ei = lax.fori_loop(0, SUBLEN, local, (zero, zero), unroll=4)
        pr, pi = ar1, ai1
        for _ in range(int(math.log2(SUBLEN))):
            pr, pi = _cmul(pr, pi, pr, pi)
        rows_r, rows_i = [car_ref[:, ls]], [cai_ref[:, ls]]
        cr, ci_ = rows_r[0], rows_i[0]
        for s in range(NSUB):
            mr, mi = _cmul(pr, pi, cr, ci_)
            cr, ci_ = mr + er[s:s + 1], mi + ei[s:s + 1]
            if s + 1 < NSUB:
                rows_r.append(cr)
                rows_i.append(ci_)
        car_ref[:, ls] = cr
        cai_ref[:, ls] = ci_
        init_r = jnp.where(is_sample, s0r_ref[:, ls], jnp.concatenate(rows_r, axis=0))
        init_i = jnp.where(is_sample, s0i_ref[:, ls], jnp.concatenate(rows_i, axis=0))

        def step(i, carry, ls=ls, ar=ar, ai=ai):
            sr, si = carry
            nr, ni = _cmul(ar, ai, sr, si)
            nr = nr + xr_ref[rows(i), ls]
            ni = ni + xi_ref[rows(i), ls]
            xr_ref[rows(i), ls] = nr
            xi_ref[rows(i), ls] = ni
            return nr, ni

        sr, si = lax.fori_loop(0, SUBLEN, step, (init_r, init_i), unroll=4)
        str_ref[:, ls] = sr
        sti_ref[:, ls] = si

    y_parts = []
    for c in range(S5_CHUNKS):
        cs = slice(c * S5_CH_ST, (c + 1) * S5_CH_ST)
        y_parts.append(jnp.dot(xr_ref[:, cs].astype(bf16), cdr_ref[c], preferred_element_type=f32)
                       + jnp.dot(xi_ref[:, cs].astype(bf16), cdi_ref[c], preferred_element_type=f32))
    y = jnp.concatenate(y_parts, axis=1) + d_ref[...] * u
    gy = _gelu(y)
    z = jnp.dot(gy.astype(bf16), gw_ref[...], preferred_element_type=f32) + gb_ref[...]
    gs_ref[...] = (gy * jax.nn.sigmoid(z)).astype(gs_ref.dtype)


def _s5_discretise(lam_re, lam_im, log_step, b_re, b_im, c_re, c_im):
    step = jnp.exp(log_step)[:, None]
    mag = jnp.exp(lam_re * step)
    ab_re, ab_im = mag * jnp.cos(lam_im * step), mag * jnp.sin(lam_im * step)
    den = lam_re * lam_re + lam_im * lam_im
    nr, ni = ab_re - 1.0, ab_im
    co_re = (nr * lam_re + ni * lam_im) / den
    co_im = (ni * lam_re - nr * lam_im) / den
    bb_re = co_re[..., None] * b_re - co_im[..., None] * b_im
    bb_im = co_re[..., None] * b_im + co_im[..., None] * b_re
    gpc = S5_GROUPS // S5_CHUNKS
    eye = jnp.eye(gpc, dtype=f32)

    def in_chunks(bb):
        t = bb.reshape(S5_CHUNKS, gpc, S5_STATE, S5_GROUP)
        return jnp.einsum('cgph,gk->cghkp', t, eye).reshape(S5_CHUNKS, S5_CH_IN, S5_CH_ST).astype(bf16)

    def out_chunks(cc):
        t = cc.reshape(S5_CHUNKS, gpc, S5_GROUP, S5_STATE)
        return jnp.einsum('cghp,gk->cgpkh', t, eye).reshape(S5_CHUNKS, S5_CH_ST, S5_CH_IN).astype(bf16)

    return (ab_re.reshape(1, S5_LANES), ab_im.reshape(1, S5_LANES), in_chunks(bb_re), in_chunks(bb_im),
            out_chunks(c_re), out_chunks(-c_im))


def s5_mixer(proj, s0_re, s0_im, disc, d, glu_w, glu_b, *, blocks_per_seq, n_prompt_blocks):
    m = proj.shape[0]
    nblk = m // TB
    ab_re, ab_im, bd_re, bd_im, cd_re, cd_im = disc
    full = lambda shape: pl.BlockSpec(shape, lambda g: (0,) * len(shape))
    return pl.pallas_call(
        functools.partial(_s5_body, blocks_per_seq=blocks_per_seq, n_prompt_blocks=n_prompt_blocks),
        grid=(nblk,),
        in_specs=[
            pl.BlockSpec((TB, S5_WIDTH), lambda g: (g, COL_S5)),
            full((NSUB, S5_LANES)),
            full((NSUB, S5_LANES)),
            full((1, S5_LANES)),
            full((1, S5_LANES)),
            full((S5_CHUNKS, S5_CH_IN, S5_CH_ST)),
            full((S5_CHUNKS, S5_CH_IN, S5_CH_ST)),
            full((S5_CHUNKS, S5_CH_ST, S5_CH_IN)),
            full((S5_CHUNKS, S5_CH_ST, S5_CH_IN)),
            full((1, S5_WIDTH)),
            full((S5_WIDTH, S5_WIDTH)),
            full((1, S5_WIDTH)),
        ],
        out_specs=[
            pl.BlockSpec((TB, S5_WIDTH), lambda g: (g, 0)),
            pl.BlockSpec((None, NSUB, S5_LANES), lambda g: (g, 0, 0)),
            pl.BlockSpec((None, NSUB, S5_LANES), lambda g: (g, 0, 0)),
        ],
        out_shape=[
            jax.ShapeDtypeStruct((m, S5_WIDTH), bf16),
            jax.ShapeDtypeStruct((nblk, NSUB, S5_LANES), f32),
            jax.ShapeDtypeStruct((nblk, NSUB, S5_LANES), f32),
        ],
        scratch_shapes=[
            pltpu.VMEM((TB, S5_LANES), f32),
            pltpu.VMEM((TB, S5_LANES), f32),
            pltpu.VMEM((1, S5_LANES), f32),
            pltpu.VMEM((1, S5_LANES), f32),
        ],
        compiler_params=_cparams(("arbitrary",)),
        name="s5_mixer",
    )(proj, s0_re, s0_im, ab_re, ab_im, bd_re, bd_im, cd_re, cd_im, d.reshape(1, S5_WIDTH),
      glu_w, glu_b.reshape(1, S5_WIDTH))


def _merge_body(att_ref, rg_ref, s5_ref, wa_ref, wr_ref, ws_ref, g0_ref, g1_ref, g2_ref, b0_ref, b1_ref,
                b2_ref, o_ref):
    acc = jax.nn.sigmoid(g0_ref[...] + b0_ref[...]) * jnp.dot(att_ref[...], wa_ref[...],
                                                               preferred_element_type=f32)
    acc = acc + jax.nn.sigmoid(g1_ref[...] + b1_ref[...]) * jnp.dot(rg_ref[...], wr_ref[...],
                                                                     preferred_element_type=f32)
    acc = acc + jax.nn.sigmoid(g2_ref[...] + b2_ref[...]) * jnp.dot(s5_ref[...], ws_ref[...],
                                                                     preferred_element_type=f32)
    o_ref[...] = acc.astype(o_ref.dtype)


def gated_merge(o_att, hg, gs, w_att_o, w_rg_o, w_s5_o, proj, b_gate, *, tm, tn):
    m = o_att.shape[0]
    k = o_att.shape[1]
    gcol = COL_GATES // tn
    bcol = D_MODEL // tn
    act = pl.BlockSpec((tm, k), lambda i, j: (i, 0))
    wsp = pl.BlockSpec((k, tn), lambda i, j: (0, j))
    gate = lambda br: pl.BlockSpec((tm, tn), lambda i, j: (i, gcol + br * bcol + j))
    bias = lambda br: pl.BlockSpec((1, tn), lambda i, j: (0, br * bcol + j))
    bg = b_gate.reshape(1, N_BRANCH * D_MODEL)
    return pl.pallas_call(
        _merge_body,
        grid=(m // tm, D_MODEL // tn),
        in_specs=[act, act, act, wsp, wsp, wsp, gate(0), gate(1), gate(2), bias(0), bias(1), bias(2)],
        out_specs=pl.BlockSpec((tm, tn), lambda i, j: (i, j)),
        out_shape=jax.ShapeDtypeStruct((m, D_MODEL), bf16),
        compiler_params=_cparams(("parallel", "arbitrary")),
        name="gated_merge",
    )(o_att, hg, gs, w_att_o, w_rg_o, w_s5_o, proj, proj, proj, bg, bg, bg)


def _residual_matmul_body(x_ref, a_ref, w_ref, o_ref):
    o_ref[...] = x_ref[...] + jnp.dot(a_ref[...], w_ref[...], preferred_element_type=f32)


def residual_matmul(x, a, w, *, tm, tn):
    m, n = x.shape
    k = a.shape[1]
    return pl.pallas_call(
        _residual_matmul_body,
        grid=(m // tm, n // tn),
        in_specs=[
            pl.BlockSpec((tm, tn), lambda i, j: (i, j)),
            pl.BlockSpec((tm, k), lambda i, j: (i, 0)),
            pl.BlockSpec((k, tn), lambda i, j: (0, j)),
        ],
        out_specs=pl.BlockSpec((tm, tn), lambda i, j: (i, j)),
        out_shape=jax.ShapeDtypeStruct((m, n), f32),
        compiler_params=_cparams(("parallel", "arbitrary")),
        name="residual_matmul",
    )(x, a, w)


PEER_TT = 256
PEER_ET = 1024
PEER_I1 = PEER_ET // N_KEYS


def _top_values(s, k):
    out = []
    for _ in range(k):
        mx = jnp.max(s, axis=0, keepdims=True)
        out.append(mx)
        s = jnp.where(s == mx, -jnp.inf, s)
    return out


def _peer_route_body(q_ref, sk_ref, s2_ref, e2_ref, thr_ref, c1_ref):
    for h in range(PEER_HEADS):
        s1 = _dot_nt(sk_ref[0], q_ref[:, (2 * h) * N_KEYS:(2 * h + 1) * N_KEYS])
        s2 = _dot_nt(sk_ref[1], q_ref[:, (2 * h + 1) * N_KEYS:(2 * h + 2) * N_KEYS])
        v1 = _top_values(s1, PEER_TOPK)
        v2 = _top_values(s2, PEER_TOPK)
        cand = [v1[k1] + v2[k2] for k1 in range(PEER_TOPK) for k2 in range(PEER_TOPK // (k1 + 1))]
        pad = (-len(cand)) % SUBLANES
        cand = jnp.concatenate(cand + [jnp.full_like(cand[0], -jnp.inf)] * pad, axis=0)
        top = _top_values(cand, PEER_TOPK)
        theta = top[-1]
        z = top[0] * 0.0
        for t in top:
            z = z + jnp.exp(t - top[0])
        s2_ref[h] = s2
        e2_ref[h] = jnp.exp(s2 - v2[0])
        thr_ref[h] = theta - s1
        c1_ref[h] = jnp.exp(s1 - v1[0]) / z


def peer_route(q, subkeys):
    m = q.shape[0]
    shp = jax.ShapeDtypeStruct((PEER_HEADS, N_KEYS, m), f32)
    spec = pl.BlockSpec((PEER_HEADS, N_KEYS, PEER_TT), lambda i: (0, 0, i))
    return pl.pallas_call(
        _peer_route_body,
        grid=(m // PEER_TT,),
        in_specs=[
            pl.BlockSpec((PEER_TT, q.shape[1]), lambda i: (i, 0)),
            pl.BlockSpec((2, N_KEYS, N_KEYS), lambda i: (0, 0, 0)),
        ],
        out_specs=[spec] * 4,
        out_shape=[shp] * 4,
        compiler_params=_cparams(("parallel",)),
        name="peer_route",
    )(q, subkeys)


def _peer_expert_body(x_ref, xn_ref, u_ref, v_ref, s2_ref, e2_ref, thr_ref, c1_ref, o_ref):
    @pl.when(pl.program_id(1) == 0)
    def _():
        o_ref[...] = x_ref[...]

    zt = _dot_nt(u_ref[...], xn_ref[...])
    parts = []
    for r in range(PEER_I1):
        w = jnp.zeros((N_KEYS, zt.shape[1]), f32)
        for h in range(PEER_HEADS):
            sel = s2_ref[h] >= thr_ref[h, r:r + 1, :]
            w = w + jnp.where(sel, e2_ref[h] * c1_ref[h, r:r + 1, :], 0.0)
        parts.append((_gelu(zt[r * N_KEYS:(r + 1) * N_KEYS]) * w).astype(bf16))
    act = jnp.concatenate(parts, axis=0)
    o_ref[...] += _dot_tn(act, v_ref[...])


def peer_experts(x, xn, u_tab, v_tab, s2, e2, thr, c1):
    m, d = x.shape
    n_i1 = N_KEYS // PEER_I1
    thr = thr.reshape(PEER_HEADS, n_i1, PEER_I1, m)
    c1 = c1.reshape(PEER_HEADS, n_i1, PEER_I1, m)
    tok = pl.BlockSpec((PEER_HEADS, N_KEYS, PEER_TT), lambda i, j: (0, 0, i))
    row = pl.BlockSpec((PEER_HEADS, None, PEER_I1, PEER_TT), lambda i, j: (0, j, 0, i))
    return pl.pallas_call(
        _peer_expert_body,
        grid=(m // PEER_TT, N_EXPERTS // PEER_ET),
        in_specs=[
            pl.BlockSpec((PEER_TT, d), lambda i, j: (i, 0)),
            pl.BlockSpec((PEER_TT, d), lambda i, j: (i, 0)),
            pl.BlockSpec((PEER_ET, d), lambda i, j: (j, 0)),
            pl.BlockSpec((PEER_ET, d), lambda i, j: (j, 0)),
            tok, tok, row, row,
        ],
        out_specs=pl.BlockSpec((PEER_TT, d), lambda i, j: (i, 0)),
        out_shape=jax.ShapeDtypeStruct((m, d), f32),
        compiler_params=_cparams(("parallel", "arbitrary")),
        name="peer_experts",
    )(x, xn, u_tab, v_tab, s2, e2, thr, c1)


def _to_blocks(x):
    b, t, c = x.shape
    return x.reshape(b, t // TB, NSUB, SUBLEN, c).transpose(0, 1, 3, 2, 4).reshape(b * t, c)


def _from_blocks(y, b):
    m, c = y.shape
    t = m // b
    return y.reshape(b, t // TB, SUBLEN, NSUB, c).transpose(0, 1, 3, 2, 4).reshape(b, t, c)


def _layer(x, l, p, tiles, dims, state):
    n_seq, blocks_per_seq, n_stream = dims
    n_prompt_blocks = n_seq * blocks_per_seq
    mp = n_prompt_blocks * TB
    pbias, sbc, sbn = tiles
    cache_k, cache_v, rg_h0, rg_conv0, s5_re0, s5_im0 = state
    tm = 3 * TB if x.shape[0] % (3 * TB) == 0 else TB
    lam_init = 0.8 - 0.6 * math.exp(-0.3 * l)
    lam = (jnp.exp(jnp.sum(p['lam_q1'] * p['lam_k1'])) - jnp.exp(jnp.sum(p['lam_q2'] * p['lam_k2']))
           + lam_init).reshape(1)

    proj = norm_matmul(x, p['norm1_g'], p['w_in'].astype(bf16), tm=tm, tn=1024, out_dtype=f32)
    qn, kn = qk_norm(proj, p['qn_g'], p['kn_g'], tm=tm)

    o_att = prompt_attention(qn, kn, proj, pbias, p['rel_bias'], lam, p['subln_g'], n_seq=n_seq,
                             blocks_per_seq=blocks_per_seq, lam_init=lam_init)
    stream = lambda a: a[mp:].reshape(SUBLEN, n_stream, -1).transpose(1, 0, 2)
    v_all = proj[:, COL_V * 1024:(COL_V + 1) * 1024]
    past = cache_k.shape[1]
    o_s = sample_attention(stream(qn), stream(kn), stream(v_all), cache_k.reshape(n_stream, past, Q_WIDTH),
                           cache_v.reshape(n_stream, past, ATT_WIDTH), sbc, sbn, lam, p['subln_g'],
                           lam_init=lam_init)
    o_att = lax.dynamic_update_slice(o_att, o_s.transpose(1, 0, 2).reshape(TB, ATT_WIDTH), (mp, 0))

    head_s = rg_conv0.transpose(1, 0, 2).reshape((CONV_W - 1) * NSUB, RG_WIDTH)
    hg, h_tail, c_tail = rglru_mixer(proj, head_s, rg_h0, p['rg_conv_w'], p['rg_conv_b'], p['rg_wa'],
                                     p['rg_wx'], p['rg_ba'], p['rg_bx'], p['rg_lambda'],
                                     blocks_per_seq=blocks_per_seq, n_prompt_blocks=n_prompt_blocks)
    disc = _s5_discretise(p['s5_lam_re'], p['s5_lam_im'], p['s5_log_step'], p['s5_b_re'], p['s5_b_im'],
                          p['s5_c_re'], p['s5_c_im'])
    gs, st_re, st_im = s5_mixer(proj, s5_re0.reshape(NSUB, S5_LANES), s5_im0.reshape(NSUB, S5_LANES), disc,
                                p['s5_d'], p['s5_glu_w'].astype(bf16), p['s5_glu_b'],
                                blocks_per_seq=blocks_per_seq, n_prompt_blocks=n_prompt_blocks)

    merged = gated_merge(o_att, hg, gs, p['w_att_o'].astype(bf16), p['w_rg_o'].astype(bf16),
                         p['w_s5_o'].astype(bf16), proj, p['b_gate'], tm=tm, tn=512)
    x = residual_matmul(x, merged, p['w_out'].astype(bf16), tm=tm, tn=1024)

    q, xn2 = norm_matmul(x, p['norm2_g'], p['peer_wq'].astype(bf16), tm=tm, tn=1024, out_dtype=bf16,
                         emit_xn=True)
    s2, e2, thr, c1 = peer_route(q, p['peer_subkeys'].astype(bf16))
    x = peer_experts(x, xn2, p['peer_u'].astype(bf16), p['peer_v'].astype(bf16), s2, e2, thr, c1)

    last = jnp.arange(n_seq) * blocks_per_seq + blocks_per_seq - 1
    hl = CONV_W - 1
    outs_p = (
        _from_blocks(kn[:mp], n_seq).reshape(n_seq, -1, N_HEADS, 2 * QK_DIM),
        _from_blocks(v_all[:mp], n_seq).reshape(n_seq, -1, N_HEADS, V_DIM),
        h_tail[last, NSUB - 1],
        c_tail[last].reshape(n_seq, hl, NSUB, RG_WIDTH)[:, :, NSUB - 1],
        st_re[last, NSUB - 1].reshape(n_seq, S5_GROUPS, S5_STATE),
        st_im[last, NSUB - 1].reshape(n_seq, S5_GROUPS, S5_STATE),
    )
    outs_s = (
        stream(kn).reshape(n_stream, SUBLEN, N_HEADS, 2 * QK_DIM),
        stream(v_all).reshape(n_stream, SUBLEN, N_HEADS, V_DIM),
        h_tail[n_prompt_blocks],
        c_tail[n_prompt_blocks].reshape(hl, NSUB, RG_WIDTH).transpose(1, 0, 2),
        st_re[n_prompt_blocks].reshape(n_stream, S5_GROUPS, S5_STATE),
        st_im[n_prompt_blocks].reshape(n_stream, S5_GROUPS, S5_STATE),
    )
    return x, outs_p, outs_s


def kernel(x_prompt, x_sample, cache_k, cache_v, state_rg_h, state_rg_conv, state_s5_re, state_s5_im, rel_bias, norm1_g, w_in, b_gate, qn_g, kn_g, lam_q1, lam_k1, lam_q2, lam_k2, subln_g, w_att_o, rg_conv_w, rg_conv_b, rg_wa, rg_ba, rg_wx, rg_bx, rg_lambda, w_rg_o, s5_lam_re, s5_lam_im, s5_log_step, s5_b_re, s5_b_im, s5_c_re, s5_c_im, s5_d, s5_glu_w, s5_glu_b, w_s5_o, w_out, norm2_g, peer_wq, peer_subkeys, peer_u, peer_v):
    n_seq, t_prompt, _ = x_prompt.shape
    n_stream, t_sample, _ = x_sample.shape
    depth = w_in.shape[0]
    assert n_stream == NSUB and t_sample == SUBLEN and t_prompt % TB == 0
    blocks_per_seq = t_prompt // TB
    dims = (n_seq, blocks_per_seq, n_stream)
    past = cache_k.shape[2]

    stacked = dict(norm1_g=norm1_g, w_in=w_in, b_gate=b_gate, qn_g=qn_g, kn_g=kn_g, lam_q1=lam_q1,
                   lam_k1=lam_k1, lam_q2=lam_q2, lam_k2=lam_k2, subln_g=subln_g, w_att_o=w_att_o,
                   rg_conv_w=rg_conv_w, rg_conv_b=rg_conv_b, rg_wa=rg_wa, rg_ba=rg_ba, rg_wx=rg_wx,
                   rg_bx=rg_bx, rg_lambda=rg_lambda, w_rg_o=w_rg_o, s5_lam_re=s5_lam_re,
                   s5_lam_im=s5_lam_im, s5_log_step=s5_log_step, s5_b_re=s5_b_re, s5_b_im=s5_b_im,
                   s5_c_re=s5_c_re, s5_c_im=s5_c_im, s5_d=s5_d, s5_glu_w=s5_glu_w, s5_glu_b=s5_glu_b,
                   w_s5_o=w_s5_o, w_out=w_out, norm2_g=norm2_g, peer_wq=peer_wq,
                   peer_subkeys=peer_subkeys, peer_u=peer_u, peer_v=peer_v)

    tiles = bias_tiles(rel_bias, past)
    x = jnp.concatenate([_to_blocks(x_prompt), x_sample.transpose(1, 0, 2).reshape(TB, D_MODEL)], axis=0)
    outs_p, outs_s = [], []
    for l in range(depth):
        p = {k: v[l] for k, v in stacked.items()}
        p['rel_bias'] = rel_bias
        state = (cache_k[l], cache_v[l], state_rg_h[l], state_rg_conv[l], state_s5_re[l], state_s5_im[l])
        x, op, os_ = _layer(x, l, p, tiles, dims, state)
        outs_p.append(op)
        outs_s.append(os_)

    mp = n_seq * t_prompt
    y_prompt = _from_blocks(x[:mp], n_seq)
    y_sample = x[mp:].reshape(SUBLEN, n_stream, D_MODEL).transpose(1, 0, 2)
    stack = lambda outs, i: jnp.stack([o[i] for o in outs])
    return (y_prompt, y_sample) + tuple(stack(outs_p, i) for i in range(6)) + tuple(stack(outs_s, i)
                                                                                    for i in range(6))
```

```python
import functools
import math

import jax
import jax.numpy as jnp
from jax import lax
from jax.experimental import pallas as pl
from jax.experimental.pallas import tpu as pltpu

f32 = jnp.float32
bf16 = jnp.bfloat16

D_MODEL = 2048
CHUNK = 64
EPS = 1e-6
N_HEADS = 8
QK_DIM = 64
V_DIM = 128
Q_WIDTH = 1024
ATT_WIDTH = 1024
N_BUCKETS = 32
MAX_DISTANCE = 128
RG_WIDTH = 1024
RG_BLOCKS = 8
RG_BLOCK = 128
CONV_W = 4
RG_C = 8.0
S5_WIDTH = 1024
S5_GROUP = 16
S5_GROUPS = 64
S5_STATE = 64
S5_LANES = S5_GROUPS * S5_STATE
PEER_HEADS = 8
N_KEYS = 128
N_EXPERTS = N_KEYS * N_KEYS
PEER_TOPK = 16
N_BRANCH = 3
IN_COLS = 2 * Q_WIDTH + ATT_WIDTH + 2 * RG_WIDTH + S5_WIDTH + N_BRANCH * D_MODEL

SUBLANES = 8
LANES = 128
VMEM_LIMIT = 56 * 1024 * 1024

NSUB = SUBLANES
SUBLEN = 32
TB = NSUB * SUBLEN
MASK_NEG = -1e30

COL_Q, COL_K, COL_V, COL_RG, COL_GATE, COL_S5 = 0, 1, 2, 3, 4, 5
COL_GATES = 6 * 1024


def _cparams(sem):
    return pltpu.CompilerParams(dimension_semantics=sem, vmem_limit_bytes=VMEM_LIMIT)


def _gelu(x):
    return 0.5 * x * (1.0 + lax.erf(x * (2.0 ** -0.5)))


def _dot_nt(a, b):
    return lax.dot_general(a, b, (((1,), (1,)), ((), ())), preferred_element_type=f32)


def _dot_tn(a, b):
    return lax.dot_general(a, b, (((0,), (0,)), ((), ())), preferred_element_type=f32)


def _norm_matmul_body(x_ref, g_ref, w_ref, o_ref, *rest, emit_xn):
    xn_ref = rest[-1]

    @pl.when(pl.program_id(1) == 0)
    def _():
        x = x_ref[...]
        ms = jnp.mean(x * x, axis=-1, keepdims=True)
        xn = (x * lax.rsqrt(ms + EPS) * g_ref[...]).astype(bf16)
        xn_ref[...] = xn
        if emit_xn:
            rest[0][...] = xn

    o_ref[...] = jnp.dot(xn_ref[...], w_ref[...], preferred_element_type=f32).astype(o_ref.dtype)


def norm_matmul(x, g, w, *, tm, tn, out_dtype, emit_xn=False):
    m, d = x.shape
    n = w.shape[1]
    out_shape = [jax.ShapeDtypeStruct((m, n), out_dtype)]
    out_specs = [pl.BlockSpec((tm, tn), lambda i, j: (i, j))]
    if emit_xn:
        out_shape.append(jax.ShapeDtypeStruct((m, d), bf16))
        out_specs.append(pl.BlockSpec((tm, d), lambda i, j: (i, 0)))
    res = pl.pallas_call(
        functools.partial(_norm_matmul_body, emit_xn=emit_xn),
        grid=(m // tm, n // tn),
        in_specs=[
            pl.BlockSpec((tm, d), lambda i, j: (i, 0)),
            pl.BlockSpec((1, d), lambda i, j: (0, 0)),
            pl.BlockSpec((d, tn), lambda i, j: (0, j)),
        ],
        out_specs=out_specs,
        out_shape=out_shape,
        scratch_shapes=[pltpu.VMEM((tm, d), bf16)],
        compiler_params=_cparams(("parallel", "arbitrary")),
        name="norm_matmul",
    )(x, g.reshape(1, d), w)
    return res if emit_xn else res[0]


def _group_mean_sq(x, ones_bd):
    sq = x * x
    hi = sq.astype(bf16)
    lo = (sq - hi.astype(f32)).astype(bf16)
    s = jnp.dot(hi, ones_bd, preferred_element_type=f32) + jnp.dot(lo, ones_bd, preferred_element_type=f32)
    return s * (1.0 / QK_DIM)


def _qknorm_body(q_ref, k_ref, v_ref, qg_ref, kg_ref, qo_ref, ko_ref, kb_ref, vb_ref):
    r = lax.broadcasted_iota(jnp.int32, (LANES, LANES), 0) // QK_DIM
    c = lax.broadcasted_iota(jnp.int32, (LANES, LANES), 1) // QK_DIM
    ones_bd = (r == c).astype(bf16)
    for t in range(Q_WIDTH // LANES):
        sl = slice(t * LANES, (t + 1) * LANES)
        q = q_ref[:, sl]
        k = k_ref[:, sl]
        qn = q * lax.rsqrt(_group_mean_sq(q, ones_bd) + EPS) * qg_ref[:, sl]
        kn = k * lax.rsqrt(_group_mean_sq(k, ones_bd) + EPS) * kg_ref[:, sl]
        qo_ref[:, sl] = (qn * (QK_DIM ** -0.5)).astype(bf16)
        ko_ref[:, sl] = kn
        kb_ref[:, sl] = kn.astype(bf16)
    vb_ref[...] = v_ref[...].astype(bf16)


def qk_norm(proj, qn_g, kn_g, *, tm):
    m = proj.shape[0]
    qg = jnp.tile(qn_g, Q_WIDTH // QK_DIM).reshape(1, Q_WIDTH)
    kg = jnp.tile(kn_g, Q_WIDTH // QK_DIM).reshape(1, Q_WIDTH)
    col = lambda c: pl.BlockSpec((tm, Q_WIDTH), lambda i: (i, c))
    gain = pl.BlockSpec((1, Q_WIDTH), lambda i: (0, 0))
    return pl.pallas_call(
        _qknorm_body,
        grid=(m // tm,),
        in_specs=[col(COL_Q), col(COL_K), col(COL_V), gain, gain],
        out_specs=[col(0)] * 4,
        out_shape=[jax.ShapeDtypeStruct((m, Q_WIDTH), bf16), jax.ShapeDtypeStruct((m, Q_WIDTH), f32),
                   jax.ShapeDtypeStruct((m, Q_WIDTH), bf16), jax.ShapeDtypeStruct((m, ATT_WIDTH), bf16)],
        compiler_params=_cparams(("parallel",)),
        name="qk_norm",
    )(proj, proj, proj, qg, kg)


def _bucket(rel):
    half = N_BUCKETS // 2
    max_exact = half // 2
    ret = jnp.where(rel > 0, half, 0)
    n = jnp.abs(rel)
    nf = jnp.maximum(n, 1).astype(f32)
    large = max_exact + (jnp.log(nf / max_exact) / math.log(MAX_DISTANCE / max_exact)
                         * (half - max_exact)).astype(jnp.int32)
    large = jnp.minimum(large, half - 1)
    return ret + jnp.where(n < max_exact, n, large)


def _bias_from_bucket(bucket, rb_ref, h):
    out = jnp.zeros(bucket.shape, f32)
    for j in range(N_BUCKETS):
        out = jnp.where(bucket == j, rb_ref[j, h], out)
    return out


def _block_time(r):
    return (r % NSUB) * SUBLEN + r // NSUB


def _bias_body(rb_ref, pb_ref, sbc_ref, sbn_ref, *, past):
    h = pl.program_id(0)
    tq = _block_time(lax.broadcasted_iota(jnp.int32, (TB, TB), 0))
    tk = _block_time(lax.broadcasted_iota(jnp.int32, (TB, TB), 1))
    b0 = _bias_from_bucket(_bucket(tk - tq), rb_ref, h)
    pb_ref[0] = jnp.where(tk // CHUNK <= tq // CHUNK, b0, MASK_NEG)
    pb_ref[1] = _bias_from_bucket(_bucket(tk - tq - TB), rb_ref, h)
    pb_ref[2] = _bias_from_bucket(_bucket(tk - tq - 2 * TB), rb_ref, h)
    nq = sbc_ref.shape[0]
    qpos = past + lax.broadcasted_iota(jnp.int32, (nq, past), 0)
    kpos = lax.broadcasted_iota(jnp.int32, (nq, past), 1)
    sbc_ref[...] = _bias_from_bucket(_bucket(kpos - qpos), rb_ref, h)
    qn = lax.broadcasted_iota(jnp.int32, (nq, nq), 0)
    kn = lax.broadcasted_iota(jnp.int32, (nq, nq), 1)
    sbn_ref[...] = _bias_from_bucket(_bucket(kn - qn), rb_ref, h)


def bias_tiles(rel_bias, past):
    return pl.pallas_call(
        functools.partial(_bias_body, past=past),
        grid=(N_HEADS,),
        in_specs=[pl.BlockSpec(memory_space=pltpu.SMEM)],
        out_specs=[
            pl.BlockSpec((None, 3, TB, TB), lambda h: (h, 0, 0, 0)),
            pl.BlockSpec((None, SUBLEN, past), lambda h: (h, 0, 0)),
            pl.BlockSpec((None, SUBLEN, SUBLEN), lambda h: (h, 0, 0)),
        ],
        out_shape=[
            jax.ShapeDtypeStruct((N_HEADS, 3, TB, TB), f32),
            jax.ShapeDtypeStruct((N_HEADS, SUBLEN, past), f32),
            jax.ShapeDtypeStruct((N_HEADS, SUBLEN, SUBLEN), f32),
        ],
        compiler_params=_cparams(("parallel",)),
        name="bias_tiles",
    )(rel_bias)


def _stack_maps(q):
    lane = lax.broadcasted_iota(jnp.int32, q.shape, 1)
    zero = jnp.zeros_like(q)
    return jnp.concatenate([jnp.where(lane < QK_DIM, q, zero), jnp.where(lane >= QK_DIM, q, zero)], axis=0)


def _online_softmax_step(s, v, m_prev, l_prev, acc_prev):
    m_new = jnp.maximum(m_prev, jnp.max(s, axis=-1, keepdims=True))
    alpha = jnp.exp(m_prev - m_new)
    p = jnp.exp(s - m_new)
    l_new = alpha * l_prev + jnp.sum(p, axis=-1, keepdims=True)
    acc_new = alpha * acc_prev + jnp.dot(p.astype(bf16), v, preferred_element_type=f32)
    return m_new, l_new, acc_new


def _diff_combine(l, acc, lam, g, nq, lam_init):
    o = acc[:nq] / l[:nq] - lam * (acc[nq:] / l[nq:])
    ms = jnp.mean(o * o, axis=-1, keepdims=True)
    return o * lax.rsqrt(ms + EPS) * g * (1.0 - lam_init)


def _prompt_attn_body(lam_ref, q_ref, k_ref, v_ref, pb_ref, g_ref, o_ref, m_ref, l_ref, acc_ref, *, lam_init):
    qi = pl.program_id(1)
    kj = pl.program_id(2)

    @pl.when(kj == 0)
    def _():
        m_ref[...] = jnp.full(m_ref.shape, -jnp.inf, f32)
        l_ref[...] = jnp.zeros(l_ref.shape, f32)
        acc_ref[...] = jnp.zeros(acc_ref.shape, f32)

    @pl.when(kj <= qi)
    def _():
        for h in range(N_HEADS):
            sl = slice(h * V_DIM, (h + 1) * V_DIM)
            qq = _stack_maps(q_ref[:, sl])
            s = _dot_nt(qq, k_ref[:, sl])
            s = (s.reshape(2, TB, TB) + pb_ref[h][None]).reshape(2 * TB, TB)
            m, l, acc = _online_softmax_step(s, v_ref[:, sl], m_ref[h], l_ref[h], acc_ref[h])
            m_ref[h] = m
            l_ref[h] = l
            acc_ref[h] = acc

    @pl.when(kj == pl.num_programs(2) - 1)
    def _():
        for h in range(N_HEADS):
            o = _diff_combine(l_ref[h], acc_ref[h], lam_ref[0], g_ref[...], TB, lam_init)
            o_ref[:, h * V_DIM:(h + 1) * V_DIM] = o.astype(o_ref.dtype)


def prompt_attention(qn, kb, vb, pbias, lam, subln_g, *, n_seq, blocks_per_seq, lam_init):
    m = qn.shape[0]
    nb = blocks_per_seq
    kv = pl.BlockSpec((TB, ATT_WIDTH), lambda b, i, j: (b * nb + jnp.minimum(i, j), 0))
    return pl.pallas_call(
        functools.partial(_prompt_attn_body, lam_init=lam_init),
        grid=(n_seq, nb, nb),
        in_specs=[
            pl.BlockSpec(memory_space=pltpu.SMEM),
            pl.BlockSpec((TB, Q_WIDTH), lambda b, i, j: (b * nb + i, 0)),
            kv, kv,
            pl.BlockSpec((N_HEADS, None, TB, TB), lambda b, i, j: (0, jnp.clip(i - j, 0, 2), 0, 0)),
            pl.BlockSpec((1, V_DIM), lambda b, i, j: (0, 0)),
        ],
        out_specs=pl.BlockSpec((TB, ATT_WIDTH), lambda b, i, j: (b * nb + i, 0)),
        out_shape=jax.ShapeDtypeStruct((m, ATT_WIDTH), bf16),
        scratch_shapes=[pltpu.VMEM((N_HEADS, 2 * TB, 1), f32), pltpu.VMEM((N_HEADS, 2 * TB, 1), f32),
                        pltpu.VMEM((N_HEADS, 2 * TB, V_DIM), f32)],
        compiler_params=_cparams(("parallel", "parallel", "arbitrary")),
        name="prompt_attention",
    )(lam, qn, kb, vb, pbias, subln_g.reshape(1, V_DIM))


def _sample_attn_body(lam_ref, q_ref, kn_ref, vn_ref, ck_ref, cv_ref, sbc_ref, sbn_ref, g_ref, o_ref,
                      m_ref, l_ref, acc_ref, *, lam_init):
    kj = pl.program_id(1)
    nq = q_ref.shape[0]
    for h in range(N_HEADS):
        sl = slice(h * V_DIM, (h + 1) * V_DIM)
        qq = _stack_maps(q_ref[:, sl])

        @pl.when(kj == 0)
        def _():
            s = _dot_nt(qq, kn_ref[:, sl].astype(bf16))
            s = (s.reshape(2, nq, nq) + sbn_ref[h][None]).reshape(2 * nq, nq)
            m = jnp.max(s, axis=-1, keepdims=True)
            p = jnp.exp(s - m)
            m_ref[h] = m
            l_ref[h] = jnp.sum(p, axis=-1, keepdims=True)
            acc_ref[h] = jnp.dot(p.astype(bf16), vn_ref[:, sl].astype(bf16), preferred_element_type=f32)

        s = _dot_nt(qq, ck_ref[:, sl].astype(bf16))
        ck = s.shape[1]
        s = (s.reshape(2, nq, ck) + sbc_ref[h][None]).reshape(2 * nq, ck)
        m, l, acc = _online_softmax_step(s, cv_ref[:, sl].astype(bf16), m_ref[h], l_ref[h], acc_ref[h])
        m_ref[h] = m
        l_ref[h] = l
        acc_ref[h] = acc

        @pl.when(kj == pl.num_programs(1) - 1)
        def _():
            o = _diff_combine(l_ref[h], acc_ref[h], lam_ref[0], g_ref[...], nq, lam_init)
            o_ref[:, sl] = o.astype(o_ref.dtype)


def sample_attention(q_s, kn_s, v_s, cache_k, cache_v, sbc, sbn, lam, subln_g, *, layer, lam_init):
    nb, nq, _ = q_s.shape
    past = cache_k.shape[2]
    chunk = min(past, 1024)
    return pl.pallas_call(
        functools.partial(_sample_attn_body, lam_init=lam_init),
        grid=(nb, past // chunk),
        in_specs=[
            pl.BlockSpec(memory_space=pltpu.SMEM),
            pl.BlockSpec((None, nq, ATT_WIDTH), lambda b, j: (b, 0, 0)),
            pl.BlockSpec((None, nq, ATT_WIDTH), lambda b, j: (b, 0, 0)),
            pl.BlockSpec((None, nq, ATT_WIDTH), lambda b, j: (b, 0, 0)),
            pl.BlockSpec((None, None, chunk, ATT_WIDTH), lambda b, j: (layer, b, j, 0)),
            pl.BlockSpec((None, None, chunk, ATT_WIDTH), lambda b, j: (layer, b, j, 0)),
            pl.BlockSpec((N_HEADS, nq, chunk), lambda b, j: (0, 0, j)),
            pl.BlockSpec((N_HEADS, nq, nq), lambda b, j: (0, 0, 0)),
            pl.BlockSpec((1, V_DIM), lambda b, j: (0, 0)),
        ],
        out_specs=pl.BlockSpec((None, nq, ATT_WIDTH), lambda b, j: (b, 0, 0)),
        out_shape=jax.ShapeDtypeStruct((nb, nq, ATT_WIDTH), bf16),
        scratch_shapes=[pltpu.VMEM((N_HEADS, 2 * nq, 1), f32), pltpu.VMEM((N_HEADS, 2 * nq, 1), f32),
                        pltpu.VMEM((N_HEADS, 2 * nq, V_DIM), f32)],
        compiler_params=_cparams(("parallel", "arbitrary")),
        name="sample_attention",
    )(lam, q_s, kn_s, v_s, cache_k, cache_v, sbc, sbn, subln_g.reshape(1, V_DIM))


def _chain_carries(c0, mult, add):
    rows = [c0]
    c = c0
    for s in range(NSUB):
        c = mult[s:s + 1] * c + add[s:s + 1]
        if s + 1 < NSUB:
            rows.append(c)
    return jnp.concatenate(rows, axis=0), c


def _rglru_body(xrg_ref, xgate_ref, head_s_ref, h0_s_ref, cw_ref, cb_ref, wa_ref, wx_ref, ba_ref, bx_ref,
                lam_ref, hg_ref, htail_ref, ctail_ref, xbuf_ref, a_ref, b_ref, hcar_ref, ptail_ref,
                *, blocks_per_seq, n_prompt_blocks):
    g = pl.program_id(0)
    is_sample = g >= n_prompt_blocks
    hl = (CONV_W - 1) * NSUB

    @pl.when(g % blocks_per_seq == 0)
    def _():
        hcar_ref[...] = jnp.zeros(hcar_ref.shape, f32)
        ptail_ref[...] = jnp.zeros(ptail_ref.shape, f32)

    x = xrg_ref[...]
    tail = x[TB - hl:]
    sub = lax.broadcasted_iota(jnp.int32, (NSUB, RG_WIDTH), 0)
    for k in range(CONV_W - 1):
        rs = slice(k * NSUB, (k + 1) * NSUB)
        prompt_head = jnp.where(sub == 0, pltpu.roll(ptail_ref[rs, :], 1, 0), pltpu.roll(tail[rs], 1, 0))
        xbuf_ref[rs, :] = jnp.where(is_sample, head_s_ref[rs, :], prompt_head)
    xbuf_ref[hl:, :] = x
    ptail_ref[...] = tail
    ctail_ref[...] = tail

    xc = cb_ref[...] + cw_ref[CONV_W - 1:CONV_W, :] * x
    for j in range(1, CONV_W):
        xc = xc + cw_ref[CONV_W - 1 - j:CONV_W - j, :] * xbuf_ref[hl - j * NSUB:hl - j * NSUB + TB, :]

    xcb = xc.astype(bf16)
    r_parts, i_parts = [], []
    for blk in range(RG_BLOCKS):
        sl = slice(blk * RG_BLOCK, (blk + 1) * RG_BLOCK)
        r_parts.append(jnp.dot(xcb[:, sl], wa_ref[blk], preferred_element_type=f32))
        i_parts.append(jnp.dot(xcb[:, sl], wx_ref[blk], preferred_element_type=f32))
    r = jax.nn.sigmoid(jnp.concatenate(r_parts, axis=1) + ba_ref[...])
    ig = jax.nn.sigmoid(jnp.concatenate(i_parts, axis=1) + bx_ref[...])
    nl = -lam_ref[...]
    softplus = jnp.maximum(nl, 0.0) + jnp.log1p(jnp.exp(-jnp.abs(nl)))
    a = jnp.exp(-RG_C * r * softplus)
    a_ref[...] = a
    b_ref[...] = jnp.sqrt(1.0 - a * a) * (ig * xc)

    def rows(i):
        return pl.ds(pl.multiple_of(i * NSUB, NSUB), NSUB)

    def local(i, carry):
        p, e = carry
        ai = a_ref[rows(i), :]
        return p * ai, ai * e + b_ref[rows(i), :]

    p, e = lax.fori_loop(0, SUBLEN, local,
                         (jnp.ones((NSUB, RG_WIDTH), f32), jnp.zeros((NSUB, RG_WIDTH), f32)), unroll=4)
    entry, exit_state = _chain_carries(hcar_ref[...], p, e)
    hcar_ref[...] = exit_state
    h_init = jnp.where(is_sample, h0_s_ref[...], entry)

    def step(i, hprev):
        hcur = a_ref[rows(i), :] * hprev + b_ref[rows(i), :]
        b_ref[rows(i), :] = hcur
        return hcur

    h_last = lax.fori_loop(0, SUBLEN, step, h_init, unroll=4)
    htail_ref[...] = h_last
    hg_ref[...] = (b_ref[...] * _gelu(xgate_ref[...])).astype(hg_ref.dtype)


def rglru_mixer(proj, head_s, h0_s, conv_w, conv_b, wa, wx, ba, bx, lam, *, blocks_per_seq, n_prompt_blocks):
    m = proj.shape[0]
    nblk = m // TB
    hl = (CONV_W - 1) * NSUB
    row = lambda v: v.reshape(1, RG_WIDTH)
    full = lambda shape: pl.BlockSpec(shape, lambda g: (0,) * len(shape))
    return pl.pallas_call(
        functools.partial(_rglru_body, blocks_per_seq=blocks_per_seq, n_prompt_blocks=n_prompt_blocks),
        grid=(nblk,),
        in_specs=[
            pl.BlockSpec((TB, RG_WIDTH), lambda g: (g, COL_RG)),
            pl.BlockSpec((TB, RG_WIDTH), lambda g: (g, COL_GATE)),
            full((hl, RG_WIDTH)),
            full((NSUB, RG_WIDTH)),
            full((CONV_W, RG_WIDTH)),
            full((1, RG_WIDTH)),
            full((RG_BLOCKS, RG_BLOCK, RG_BLOCK)),
            full((RG_BLOCKS, RG_BLOCK, RG_BLOCK)),
            full((1, RG_WIDTH)),
            full((1, RG_WIDTH)),
            full((1, RG_WIDTH)),
        ],
        out_specs=[
            pl.BlockSpec((TB, RG_WIDTH), lambda g: (g, 0)),
            pl.BlockSpec((None, NSUB, RG_WIDTH), lambda g: (g, 0, 0)),
            pl.BlockSpec((None, hl, RG_WIDTH), lambda g: (g, 0, 0)),
        ],
        out_shape=[
            jax.ShapeDtypeStruct((m, RG_WIDTH), bf16),
            jax.ShapeDtypeStruct((nblk, NSUB, RG_WIDTH), f32),
            jax.ShapeDtypeStruct((nblk, hl, RG_WIDTH), f32),
        ],
        scratch_shapes=[
            pltpu.VMEM((hl + TB, RG_WIDTH), f32),
            pltpu.VMEM((TB, RG_WIDTH), f32),
            pltpu.VMEM((TB, RG_WIDTH), f32),
            pltpu.VMEM((1, RG_WIDTH), f32),
            pltpu.VMEM((hl, RG_WIDTH), f32),
        ],
        compiler_params=_cparams(("arbitrary",)),
        name="rglru_mixer",
    )(proj, proj, head_s, h0_s, conv_w, row(conv_b), wa.astype(bf16), wx.astype(bf16), row(ba), row(bx),
      row(lam))


S5_CHUNKS = 8
S5_CH_IN = S5_WIDTH // S5_CHUNKS
S5_CH_ST = S5_LANES // S5_CHUNKS
S5_SCAN_LANES = 512


def _cmul(ar, ai, br, bi):
    return ar * br - ai * bi, ar * bi + ai * br


def _s5_body(u_ref, s0r_ref, s0i_ref, abr_ref, abi_ref, bdr_ref, bdi_ref, cdr_ref, cdi_ref, d_ref, gw_ref,
             gb_ref, gs_ref, str_ref, sti_ref, xr_ref, xi_ref, car_ref, cai_ref,
             *, blocks_per_seq, n_prompt_blocks):
    g = pl.program_id(0)
    is_sample = g >= n_prompt_blocks

    @pl.when(g % blocks_per_seq == 0)
    def _():
        car_ref[...] = jnp.zeros(car_ref.shape, f32)
        cai_ref[...] = jnp.zeros(cai_ref.shape, f32)

    u = u_ref[...]
    ub = u.astype(bf16)
    for c in range(S5_CHUNKS):
        ci = slice(c * S5_CH_IN, (c + 1) * S5_CH_IN)
        cs = slice(c * S5_CH_ST, (c + 1) * S5_CH_ST)
        xr_ref[:, cs] = jnp.dot(ub[:, ci], bdr_ref[c], preferred_element_type=f32)
        xi_ref[:, cs] = jnp.dot(ub[:, ci], bdi_ref[c], preferred_element_type=f32)

    def rows(i):
        return pl.ds(pl.multiple_of(i * NSUB, NSUB), NSUB)

    for c in range(S5_LANES // S5_SCAN_LANES):
        ls = slice(c * S5_SCAN_LANES, (c + 1) * S5_SCAN_LANES)
        ar1 = abr_ref[:, ls]
        ai1 = abi_ref[:, ls]
        ar = jnp.broadcast_to(ar1, (NSUB, S5_SCAN_LANES))
        ai = jnp.broadcast_to(ai1, (NSUB, S5_SCAN_LANES))

        def local(i, carry, ls=ls, ar=ar, ai=ai):
            er, ei = carry
            nr, ni = _cmul(ar, ai, er, ei)
            return nr + xr_ref[rows(i), ls], ni + xi_ref[rows(i), ls]

        zero = jnp.zeros((NSUB, S5_SCAN_LANES), f32)
        er, ei = lax.fori_loop(0, SUBLEN, local, (zero, zero), unroll=4)
        pr, pi = ar1, ai1
        for _ in range(int(math.log2(SUBLEN))):
            pr, pi = _cmul(pr, pi, pr, pi)
        rows_r, rows_i = [car_ref[:, ls]], [cai_ref[:, ls]]
        cr, ci_ = rows_r[0], rows_i[0]
        for s in range(NSUB):
            mr, mi = _cmul(pr, pi, cr, ci_)
            cr, ci_ = mr + er[s:s + 1], mi + ei[s:s + 1]
            if s + 1 < NSUB:
                rows_r.append(cr)
                rows_i.append(ci_)
        car_ref[:, ls] = cr
        cai_ref[:, ls] = ci_
        init_r = jnp.where(is_sample, s0r_ref[:, ls], jnp.concatenate(rows_r, axis=0))
        init_i = jnp.where(is_sample, s0i_ref[:, ls], jnp.concatenate(rows_i, axis=0))

        def step(i, carry, ls=ls, ar=ar, ai=ai):
            sr, si = carry
            nr, ni = _cmul(ar, ai, sr, si)
            nr = nr + xr_ref[rows(i), ls]
            ni = ni + xi_ref[rows(i), ls]
            xr_ref[rows(i), ls] = nr
            xi_ref[rows(i), ls] = ni
            return nr, ni

        sr, si = lax.fori_loop(0, SUBLEN, step, (init_r, init_i), unroll=4)
        str_ref[:, ls] = sr
        sti_ref[:, ls] = si

    y_parts = []
    for c in range(S5_CHUNKS):
        cs = slice(c * S5_CH_ST, (c + 1) * S5_CH_ST)
        y_parts.append(jnp.dot(xr_ref[:, cs].astype(bf16), cdr_ref[c], preferred_element_type=f32)
                       + jnp.dot(xi_ref[:, cs].astype(bf16), cdi_ref[c], preferred_element_type=f32))
    y = jnp.concatenate(y_parts, axis=1) + d_ref[...] * u
    gy = _gelu(y)
    z = jnp.dot(gy.astype(bf16), gw_ref[...], preferred_element_type=f32) + gb_ref[...]
    gs_ref[...] = (gy * jax.nn.sigmoid(z)).astype(gs_ref.dtype)


def _s5_discretise(lam_re, lam_im, log_step, b_re, b_im, c_re, c_im):
    step = jnp.exp(log_step)[:, None]
    mag = jnp.exp(lam_re * step)
    ab_re, ab_im = mag * jnp.cos(lam_im * step), mag * jnp.sin(lam_im * step)
    den = lam_re * lam_re + lam_im * lam_im
    nr, ni = ab_re - 1.0, ab_im
    co_re = (nr * lam_re + ni * lam_im) / den
    co_im = (ni * lam_re - nr * lam_im) / den
    bb_re = co_re[..., None] * b_re - co_im[..., None] * b_im
    bb_im = co_re[..., None] * b_im + co_im[..., None] * b_re
    gpc = S5_GROUPS // S5_CHUNKS
    eye = jnp.eye(gpc, dtype=f32)

    def in_chunks(bb):
        t = bb.reshape(S5_CHUNKS, gpc, S5_STATE, S5_GROUP)
        return jnp.einsum('cgph,gk->cghkp', t, eye).reshape(S5_CHUNKS, S5_CH_IN, S5_CH_ST).astype(bf16)

    def out_chunks(cc):
        t = cc.reshape(S5_CHUNKS, gpc, S5_GROUP, S5_STATE)
        return jnp.einsum('cghp,gk->cgpkh', t, eye).reshape(S5_CHUNKS, S5_CH_ST, S5_CH_IN).astype(bf16)

    return (ab_re.reshape(1, S5_LANES), ab_im.reshape(1, S5_LANES), in_chunks(bb_re), in_chunks(bb_im),
            out_chunks(c_re), out_chunks(-c_im))


def s5_mixer(proj, s0_re, s0_im, disc, d, glu_w, glu_b, *, blocks_per_seq, n_prompt_blocks):
    m = proj.shape[0]
    nblk = m // TB
    ab_re, ab_im, bd_re, bd_im, cd_re, cd_im = disc
    full = lambda shape: pl.BlockSpec(shape, lambda g: (0,) * len(shape))
    return pl.pallas_call(
        functools.partial(_s5_body, blocks_per_seq=blocks_per_seq, n_prompt_blocks=n_prompt_blocks),
        grid=(nblk,),
        in_specs=[
            pl.BlockSpec((TB, S5_WIDTH), lambda g: (g, COL_S5)),
            full((NSUB, S5_LANES)),
            full((NSUB, S5_LANES)),
            full((1, S5_LANES)),
            full((1, S5_LANES)),
            full((S5_CHUNKS, S5_CH_IN, S5_CH_ST)),
            full((S5_CHUNKS, S5_CH_IN, S5_CH_ST)),
            full((S5_CHUNKS, S5_CH_ST, S5_CH_IN)),
            full((S5_CHUNKS, S5_CH_ST, S5_CH_IN)),
            full((1, S5_WIDTH)),
            full((S5_WIDTH, S5_WIDTH)),
            full((1, S5_WIDTH)),
        ],
        out_specs=[
            pl.BlockSpec((TB, S5_WIDTH), lambda g: (g, 0)),
            pl.BlockSpec((None, NSUB, S5_LANES), lambda g: (g, 0, 0)),
            pl.BlockSpec((None, NSUB, S5_LANES), lambda g: (g, 0, 0)),
        ],
        out_shape=[
            jax.ShapeDtypeStruct((m, S5_WIDTH), bf16),
            jax.ShapeDtypeStruct((nblk, NSUB, S5_LANES), f32),
            jax.ShapeDtypeStruct((nblk, NSUB, S5_LANES), f32),
        ],
        scratch_shapes=[
            pltpu.VMEM((TB, S5_LANES), f32),
            pltpu.VMEM((TB, S5_LANES), f32),
            pltpu.VMEM((1, S5_LANES), f32),
            pltpu.VMEM((1, S5_LANES), f32),
        ],
        compiler_params=_cparams(("arbitrary",)),
        name="s5_mixer",
    )(proj, s0_re, s0_im, ab_re, ab_im, bd_re, bd_im, cd_re, cd_im, d.reshape(1, S5_WIDTH),
      glu_w, glu_b.reshape(1, S5_WIDTH))


def _merge_body(att_ref, rg_ref, s5_ref, wa_ref, wr_ref, ws_ref, g0_ref, g1_ref, g2_ref, b0_ref, b1_ref,
                b2_ref, o_ref):
    acc = jax.nn.sigmoid(g0_ref[...] + b0_ref[...]) * jnp.dot(att_ref[...], wa_ref[...],
                                                               preferred_element_type=f32)
    acc = acc + jax.nn.sigmoid(g1_ref[...] + b1_ref[...]) * jnp.dot(rg_ref[...], wr_ref[...],
                                                                     preferred_element_type=f32)
    acc = acc + jax.nn.sigmoid(g2_ref[...] + b2_ref[...]) * jnp.dot(s5_ref[...], ws_ref[...],
                                                                     preferred_element_type=f32)
    o_ref[...] = acc.astype(o_ref.dtype)


def gated_merge(o_att, hg, gs, w_att_o, w_rg_o, w_s5_o, proj, b_gate, *, tm, tn):
    m = o_att.shape[0]
    k = o_att.shape[1]
    gcol = COL_GATES // tn
    bcol = D_MODEL // tn
    act = pl.BlockSpec((tm, k), lambda i, j: (i, 0))
    wsp = pl.BlockSpec((k, tn), lambda i, j: (0, j))
    gate = lambda br: pl.BlockSpec((tm, tn), lambda i, j: (i, gcol + br * bcol + j))
    bias = lambda br: pl.BlockSpec((1, tn), lambda i, j: (0, br * bcol + j))
    bg = b_gate.reshape(1, N_BRANCH * D_MODEL)
    return pl.pallas_call(
        _merge_body,
        grid=(m // tm, D_MODEL // tn),
        in_specs=[act, act, act, wsp, wsp, wsp, gate(0), gate(1), gate(2), bias(0), bias(1), bias(2)],
        out_specs=pl.BlockSpec((tm, tn), lambda i, j: (i, j)),
        out_shape=jax.ShapeDtypeStruct((m, D_MODEL), bf16),
        compiler_params=_cparams(("parallel", "arbitrary")),
        name="gated_merge",
    )(o_att, hg, gs, w_att_o, w_rg_o, w_s5_o, proj, proj, proj, bg, bg, bg)


def _residual_matmul_body(x_ref, a_ref, w_ref, o_ref):
    o_ref[...] = x_ref[...] + jnp.dot(a_ref[...], w_ref[...], preferred_element_type=f32)


def residual_matmul(x, a, w, *, tm, tn):
    m, n = x.shape
    k = a.shape[1]
    return pl.pallas_call(
        _residual_matmul_body,
        grid=(m // tm, n // tn),
        in_specs=[
            pl.BlockSpec((tm, tn), lambda i, j: (i, j)),
            pl.BlockSpec((tm, k), lambda i, j: (i, 0)),
            pl.BlockSpec((k, tn), lambda i, j: (0, j)),
        ],
        out_specs=pl.BlockSpec((tm, tn), lambda i, j: (i, j)),
        out_shape=jax.ShapeDtypeStruct((m, n), f32),
        compiler_params=_cparams(("parallel", "arbitrary")),
        name="residual_matmul",
    )(x, a, w)


PEER_TT = 256
PEER_ET = 512
PEER_I1 = PEER_ET // N_KEYS


def _top_values(s, k):
    out = []
    for _ in range(k):
        mx = jnp.max(s, axis=0, keepdims=True)
        out.append(mx)
        s = jnp.where(s == mx, -jnp.inf, s)
    return out


def _peer_route_body(q_ref, sk_ref, s2_ref, e2_ref, thr_ref, c1_ref):
    for h in range(PEER_HEADS):
        s1 = _dot_nt(sk_ref[0], q_ref[:, (2 * h) * N_KEYS:(2 * h + 1) * N_KEYS])
        s2 = _dot_nt(sk_ref[1], q_ref[:, (2 * h + 1) * N_KEYS:(2 * h + 2) * N_KEYS])
        v1 = _top_values(s1, PEER_TOPK)
        v2 = _top_values(s2, PEER_TOPK)
        cand = [v1[k1] + v2[k2] for k1 in range(PEER_TOPK) for k2 in range(PEER_TOPK // (k1 + 1))]
        pad = (-len(cand)) % SUBLANES
        cand = jnp.concatenate(cand + [jnp.full_like(cand[0], -jnp.inf)] * pad, axis=0)
        top = _top_values(cand, PEER_TOPK)
        theta = top[-1]
        z = top[0] * 0.0
        for t in top:
            z = z + jnp.exp(t - top[0])
        s2_ref[h] = s2
        e2_ref[h] = jnp.exp(s2 - v2[0])
        thr_ref[h] = theta - s1
        c1_ref[h] = jnp.exp(s1 - v1[0]) / z


def peer_route(q, subkeys):
    m = q.shape[0]
    shp = jax.ShapeDtypeStruct((PEER_HEADS, N_KEYS, m), f32)
    spec = pl.BlockSpec((PEER_HEADS, N_KEYS, PEER_TT), lambda i: (0, 0, i))
    return pl.pallas_call(
        _peer_route_body,
        grid=(m // PEER_TT,),
        in_specs=[
            pl.BlockSpec((PEER_TT, q.shape[1]), lambda i: (i, 0)),
            pl.BlockSpec((2, N_KEYS, N_KEYS), lambda i: (0, 0, 0)),
        ],
        out_specs=[spec] * 4,
        out_shape=[shp] * 4,
        compiler_params=_cparams(("parallel",)),
        name="peer_route",
    )(q, subkeys)


def _peer_expert_body(x_ref, xnt_ref, u_ref, vt_ref, s2_ref, e2_ref, thr_ref, c1_ref, o_ref, acc_ref, act_ref):
    j = pl.program_id(1)

    @pl.when(j == 0)
    def _():
        acc_ref[...] = jnp.zeros(acc_ref.shape, f32)

    zt = jnp.dot(u_ref[...], xnt_ref[...], preferred_element_type=f32)
    for tc in range(zt.shape[1] // LANES):
        ts = slice(tc * LANES, (tc + 1) * LANES)
        for r in range(PEER_I1):
            w = jnp.zeros((N_KEYS, LANES), f32)
            for h in range(PEER_HEADS):
                sel = s2_ref[h, :, ts] >= thr_ref[h, r:r + 1, ts]
                w = w + jnp.where(sel, e2_ref[h, :, ts] * c1_ref[h, r:r + 1, ts], 0.0)
            rs = slice(r * N_KEYS, (r + 1) * N_KEYS)
            act_ref[rs, ts] = (_gelu(zt[rs, ts]) * w).astype(bf16)
    acc_ref[...] += jnp.dot(vt_ref[...], act_ref[...], preferred_element_type=f32)

    @pl.when(j == pl.num_programs(1) - 1)
    def _():
        o_ref[...] = x_ref[...] + acc_ref[...].T


def peer_experts(x, xnt, u_tab, vt_tab, s2, e2, thr, c1, *, tt):
    m, d = x.shape
    n_i1 = N_KEYS // PEER_I1
    thr = thr.reshape(PEER_HEADS, n_i1, PEER_I1, m)
    c1 = c1.reshape(PEER_HEADS, n_i1, PEER_I1, m)
    once = pl.Buffered(1)
    tok = pl.BlockSpec((PEER_HEADS, N_KEYS, tt), lambda i, j: (0, 0, i), pipeline_mode=once)
    row = pl.BlockSpec((PEER_HEADS, None, PEER_I1, tt), lambda i, j: (0, j, 0, i))
    return pl.pallas_call(
        _peer_expert_body,
        grid=(m // tt, N_EXPERTS // PEER_ET),
        in_specs=[
            pl.BlockSpec((tt, d), lambda i, j: (i, 0), pipeline_mode=once),
            pl.BlockSpec((d, tt), lambda i, j: (0, i), pipeline_mode=once),
            pl.BlockSpec((PEER_ET, d), lambda i, j: (j, 0)),
            pl.BlockSpec((d, PEER_ET), lambda i, j: (0, j)),
            tok, tok, row, row,
        ],
        out_specs=pl.BlockSpec((tt, d), lambda i, j: (i, 0)),
        out_shape=jax.ShapeDtypeStruct((m, d), f32),
        scratch_shapes=[pltpu.VMEM((d, tt), f32), pltpu.VMEM((PEER_ET, tt), bf16)],
        compiler_params=_cparams(("parallel", "arbitrary")),
        name="peer_experts",
    )(x, xnt, u_tab, vt_tab, s2, e2, thr, c1)


def _to_blocks(x):
    b, t, c = x.shape
    return x.reshape(b, t // TB, NSUB, SUBLEN, c).transpose(0, 1, 3, 2, 4).reshape(b * t, c)


def _from_blocks(y, b):
    m, c = y.shape
    t = m // b
    return y.reshape(b, t // TB, SUBLEN, NSUB, c).transpose(0, 1, 3, 2, 4).reshape(b, t, c)


def _layer(x, l, p, tiles, dims, state):
    n_seq, blocks_per_seq, n_stream = dims
    n_prompt_blocks = n_seq * blocks_per_seq
    mp = n_prompt_blocks * TB
    pbias, sbc, sbn = tiles
    cache_k, cache_v, rg_h0, rg_conv0, s5_re0, s5_im0 = state
    tm = 3 * TB if x.shape[0] % (3 * TB) == 0 else TB
    lam_init = 0.8 - 0.6 * math.exp(-0.3 * l)
    lam = (jnp.exp(jnp.sum(p['lam_q1'] * p['lam_k1'])) - jnp.exp(jnp.sum(p['lam_q2'] * p['lam_k2']))
           + lam_init).reshape(1)

    proj = norm_matmul(x, p['norm1_g'], p['w_in'].astype(bf16), tm=tm, tn=1024, out_dtype=f32)
    qn, kn, kb, vb = qk_norm(proj, p['qn_g'], p['kn_g'], tm=tm)

    o_att = prompt_attention(qn, kb, vb, pbias, lam, p['subln_g'], n_seq=n_seq,
                             blocks_per_seq=blocks_per_seq, lam_init=lam_init)
    stream = lambda a: a[mp:].reshape(SUBLEN, n_stream, -1).transpose(1, 0, 2)
    v_all = proj[:, COL_V * 1024:(COL_V + 1) * 1024]
    depth, _, past = cache_k.shape[:3]
    o_s = sample_attention(stream(qn), stream(kn), stream(v_all),
                           cache_k.reshape(depth, n_stream, past, Q_WIDTH),
                           cache_v.reshape(depth, n_stream, past, ATT_WIDTH), sbc, sbn, lam, p['subln_g'],
                           layer=l, lam_init=lam_init)
    o_att = lax.dynamic_update_slice(o_att, o_s.transpose(1, 0, 2).reshape(TB, ATT_WIDTH), (mp, 0))

    head_s = rg_conv0.transpose(1, 0, 2).reshape((CONV_W - 1) * NSUB, RG_WIDTH)
    hg, h_tail, c_tail = rglru_mixer(proj, head_s, rg_h0, p['rg_conv_w'], p['rg_conv_b'], p['rg_wa'],
                                     p['rg_wx'], p['rg_ba'], p['rg_bx'], p['rg_lambda'],
                                     blocks_per_seq=blocks_per_seq, n_prompt_blocks=n_prompt_blocks)
    disc = _s5_discretise(p['s5_lam_re'], p['s5_lam_im'], p['s5_log_step'], p['s5_b_re'], p['s5_b_im'],
                          p['s5_c_re'], p['s5_c_im'])
    gs, st_re, st_im = s5_mixer(proj, s5_re0.reshape(NSUB, S5_LANES), s5_im0.reshape(NSUB, S5_LANES), disc,
                                p['s5_d'], p['s5_glu_w'].astype(bf16), p['s5_glu_b'],
                                blocks_per_seq=blocks_per_seq, n_prompt_blocks=n_prompt_blocks)

    merged = gated_merge(o_att, hg, gs, p['w_att_o'].astype(bf16), p['w_rg_o'].astype(bf16),
                         p['w_s5_o'].astype(bf16), proj, p['b_gate'], tm=tm, tn=512)
    x = residual_matmul(x, merged, p['w_out'].astype(bf16), tm=tm, tn=1024)

    q, xn2 = norm_matmul(x, p['norm2_g'], p['peer_wq'].astype(bf16), tm=tm, tn=1024, out_dtype=bf16,
                         emit_xn=True)
    s2, e2, thr, c1 = peer_route(q, p['peer_subkeys'].astype(bf16))
    x = peer_experts(x, xn2.T, p['peer_u'].astype(bf16), p['peer_v'].T.astype(bf16), s2, e2, thr, c1, tt=tm)

    last = jnp.arange(n_seq) * blocks_per_seq + blocks_per_seq - 1
    hl = CONV_W - 1
    outs_p = (
        _from_blocks(kn[:mp], n_seq).reshape(n_seq, -1, N_HEADS, 2 * QK_DIM),
        _from_blocks(v_all[:mp], n_seq).reshape(n_seq, -1, N_HEADS, V_DIM),
        h_tail[last, NSUB - 1],
        c_tail[last].reshape(n_seq, hl, NSUB, RG_WIDTH)[:, :, NSUB - 1],
        st_re[last, NSUB - 1].reshape(n_seq, S5_GROUPS, S5_STATE),
        st_im[last, NSUB - 1].reshape(n_seq, S5_GROUPS, S5_STATE),
    )
    outs_s = (
        stream(kn).reshape(n_stream, SUBLEN, N_HEADS, 2 * QK_DIM),
        stream(v_all).reshape(n_stream, SUBLEN, N_HEADS, V_DIM),
        h_tail[n_prompt_blocks],
        c_tail[n_prompt_blocks].reshape(hl, NSUB, RG_WIDTH).transpose(1, 0, 2),
        st_re[n_prompt_blocks].reshape(n_stream, S5_GROUPS, S5_STATE),
        st_im[n_prompt_blocks].reshape(n_stream, S5_GROUPS, S5_STATE),
    )
    return x, outs_p, outs_s


def kernel(x_prompt, x_sample, cache_k, cache_v, state_rg_h, state_rg_conv, state_s5_re, state_s5_im, rel_bias, norm1_g, w_in, b_gate, qn_g, kn_g, lam_q1, lam_k1, lam_q2, lam_k2, subln_g, w_att_o, rg_conv_w, rg_conv_b, rg_wa, rg_ba, rg_wx, rg_bx, rg_lambda, w_rg_o, s5_lam_re, s5_lam_im, s5_log_step, s5_b_re, s5_b_im, s5_c_re, s5_c_im, s5_d, s5_glu_w, s5_glu_b, w_s5_o, w_out, norm2_g, peer_wq, peer_subkeys, peer_u, peer_v):
    n_seq, t_prompt, _ = x_prompt.shape
    n_stream, t_sample, _ = x_sample.shape
    depth = w_in.shape[0]
    assert n_stream == NSUB and t_sample == SUBLEN and t_prompt % TB == 0
    blocks_per_seq = t_prompt // TB
    dims = (n_seq, blocks_per_seq, n_stream)
    past = cache_k.shape[2]

    stacked = dict(norm1_g=norm1_g, w_in=w_in, b_gate=b_gate, qn_g=qn_g, kn_g=kn_g, lam_q1=lam_q1,
                   lam_k1=lam_k1, lam_q2=lam_q2, lam_k2=lam_k2, subln_g=subln_g, w_att_o=w_att_o,
                   rg_conv_w=rg_conv_w, rg_conv_b=rg_conv_b, rg_wa=rg_wa, rg_ba=rg_ba, rg_wx=rg_wx,
                   rg_bx=rg_bx, rg_lambda=rg_lambda, w_rg_o=w_rg_o, s5_lam_re=s5_lam_re,
                   s5_lam_im=s5_lam_im, s5_log_step=s5_log_step, s5_b_re=s5_b_re, s5_b_im=s5_b_im,
                   s5_c_re=s5_c_re, s5_c_im=s5_c_im, s5_d=s5_d, s5_glu_w=s5_glu_w, s5_glu_b=s5_glu_b,
                   w_s5_o=w_s5_o, w_out=w_out, norm2_g=norm2_g, peer_wq=peer_wq,
                   peer_subkeys=peer_subkeys, peer_u=peer_u, peer_v=peer_v)

    tiles = bias_tiles(rel_bias, past)
    x = jnp.concatenate([_to_blocks(x_prompt), x_sample.transpose(1, 0, 2).reshape(TB, D_MODEL)], axis=0)
    outs_p, outs_s = [], []
    for l in range(depth):
        p = {k: v[l] for k, v in stacked.items()}
        p['rel_bias'] = rel_bias
        state = (cache_k, cache_v, state_rg_h[l], state_rg_conv[l], state_s5_re[l], state_s5_im[l])
        x, op, os_ = _layer(x, l, p, tiles, dims, state)
        outs_p.append(op)
        outs_s.append(os_)

    mp = n_seq * t_prompt
    y_prompt = _from_blocks(x[:mp], n_seq)
    y_sample = x[mp:].reshape(SUBLEN, n_stream, D_MODEL).transpose(1, 0, 2)
    stack = lambda outs, i: jnp.stack([o[i] for o in outs])
    return (y_prompt, y_sample) + tuple(stack(outs_p, i) for i in range(6)) + tuple(stack(outs_s, i)
                                                                                    for i in range(6))
```

```python
import functools
import math

import jax
import jax.numpy as jnp
from jax import lax
from jax.experimental import pallas as pl
from jax.experimental.pallas import tpu as pltpu

f32 = jnp.float32
bf16 = jnp.bfloat16

D_MODEL = 2048
CHUNK = 64
EPS = 1e-6
N_HEADS = 8
QK_DIM = 64
V_DIM = 128
Q_WIDTH = 1024
ATT_WIDTH = 1024
N_BUCKETS = 32
MAX_DISTANCE = 128
RG_WIDTH = 1024
RG_BLOCKS = 8
RG_BLOCK = 128
CONV_W = 4
RG_C = 8.0
S5_WIDTH = 1024
S5_GROUP = 16
S5_GROUPS = 64
S5_STATE = 64
S5_LANES = S5_GROUPS * S5_STATE
PEER_HEADS = 8
N_KEYS = 128
N_EXPERTS = N_KEYS * N_KEYS
PEER_TOPK = 16
N_BRANCH = 3
IN_COLS = 2 * Q_WIDTH + ATT_WIDTH + 2 * RG_WIDTH + S5_WIDTH + N_BRANCH * D_MODEL

SUBLANES = 8
LANES = 128
VMEM_LIMIT = 56 * 1024 * 1024

NSUB = SUBLANES
SUBLEN = 32
TB = NSUB * SUBLEN
MASK_NEG = -1e30

COL_Q, COL_K, COL_V, COL_RG, COL_GATE, COL_S5 = 0, 1, 2, 3, 4, 5
COL_GATES = 6 * 1024


def _cparams(sem):
    return pltpu.CompilerParams(dimension_semantics=sem, vmem_limit_bytes=VMEM_LIMIT)


def _gelu(x):
    return 0.5 * x * (1.0 + lax.erf(x * (2.0 ** -0.5)))


def _dot_nt(a, b):
    return lax.dot_general(a, b, (((1,), (1,)), ((), ())), preferred_element_type=f32)


def _dot_tn(a, b):
    return lax.dot_general(a, b, (((0,), (0,)), ((), ())), preferred_element_type=f32)


def _norm_matmul_body(x_ref, g_ref, w_ref, o_ref, *rest, emit_xn):
    xn_ref = rest[-1]

    @pl.when(pl.program_id(1) == 0)
    def _():
        x = x_ref[...]
        ms = jnp.mean(x * x, axis=-1, keepdims=True)
        xn = (x * lax.rsqrt(ms + EPS) * g_ref[...]).astype(bf16)
        xn_ref[...] = xn
        if emit_xn:
            rest[0][...] = xn

    o_ref[...] = jnp.dot(xn_ref[...], w_ref[...], preferred_element_type=f32).astype(o_ref.dtype)


def norm_matmul(x, g, w, *, layer, tm, tn, out_dtype, emit_xn=False):
    m, d = x.shape
    n = w.shape[2]
    out_shape = [jax.ShapeDtypeStruct((m, n), out_dtype)]
    out_specs = [pl.BlockSpec((tm, tn), lambda i, j: (i, j))]
    if emit_xn:
        out_shape.append(jax.ShapeDtypeStruct((m, d), bf16))
        out_specs.append(pl.BlockSpec((tm, d), lambda i, j: (i, 0)))
    res = pl.pallas_call(
        functools.partial(_norm_matmul_body, emit_xn=emit_xn),
        grid=(m // tm, n // tn),
        in_specs=[
            pl.BlockSpec((tm, d), lambda i, j: (i, 0)),
            pl.BlockSpec((1, d), lambda i, j: (0, 0)),
            pl.BlockSpec((None, d, tn), lambda i, j: (layer, 0, j)),
        ],
        out_specs=out_specs,
        out_shape=out_shape,
        scratch_shapes=[pltpu.VMEM((tm, d), bf16)],
        compiler_params=_cparams(("parallel", "arbitrary")),
        name="norm_matmul",
    )(x, g.reshape(1, d), w)
    return res if emit_xn else res[0]


def _group_mean_sq(x, ones_bd):
    sq = x * x
    hi = sq.astype(bf16)
    lo = (sq - hi.astype(f32)).astype(bf16)
    s = jnp.dot(hi, ones_bd, preferred_element_type=f32) + jnp.dot(lo, ones_bd, preferred_element_type=f32)
    return s * (1.0 / QK_DIM)


def _qknorm_body(q_ref, k_ref, v_ref, qg_ref, kg_ref, qo_ref, ko_ref, kb_ref, vb_ref):
    r = lax.broadcasted_iota(jnp.int32, (LANES, LANES), 0) // QK_DIM
    c = lax.broadcasted_iota(jnp.int32, (LANES, LANES), 1) // QK_DIM
    ones_bd = (r == c).astype(bf16)
    for t in range(Q_WIDTH // LANES):
        sl = slice(t * LANES, (t + 1) * LANES)
        q = q_ref[:, sl]
        k = k_ref[:, sl]
        qn = q * lax.rsqrt(_group_mean_sq(q, ones_bd) + EPS) * qg_ref[:, sl]
        kn = k * lax.rsqrt(_group_mean_sq(k, ones_bd) + EPS) * kg_ref[:, sl]
        qo_ref[:, sl] = (qn * (QK_DIM ** -0.5)).astype(bf16)
        ko_ref[:, sl] = kn
        kb_ref[:, sl] = kn.astype(bf16)
    vb_ref[...] = v_ref[...].astype(bf16)


def qk_norm(proj, qn_g, kn_g, *, tm):
    m = proj.shape[0]
    qg = jnp.tile(qn_g, Q_WIDTH // QK_DIM).reshape(1, Q_WIDTH)
    kg = jnp.tile(kn_g, Q_WIDTH // QK_DIM).reshape(1, Q_WIDTH)
    col = lambda c: pl.BlockSpec((tm, Q_WIDTH), lambda i: (i, c))
    gain = pl.BlockSpec((1, Q_WIDTH), lambda i: (0, 0))
    return pl.pallas_call(
        _qknorm_body,
        grid=(m // tm,),
        in_specs=[col(COL_Q), col(COL_K), col(COL_V), gain, gain],
        out_specs=[col(0)] * 4,
        out_shape=[jax.ShapeDtypeStruct((m, Q_WIDTH), bf16), jax.ShapeDtypeStruct((m, Q_WIDTH), f32),
                   jax.ShapeDtypeStruct((m, Q_WIDTH), bf16), jax.ShapeDtypeStruct((m, ATT_WIDTH), bf16)],
        compiler_params=_cparams(("parallel",)),
        name="qk_norm",
    )(proj, proj, proj, qg, kg)


def _bucket(rel):
    half = N_BUCKETS // 2
    max_exact = half // 2
    ret = jnp.where(rel > 0, half, 0)
    n = jnp.abs(rel)
    nf = jnp.maximum(n, 1).astype(f32)
    large = max_exact + (jnp.log(nf / max_exact) / math.log(MAX_DISTANCE / max_exact)
                         * (half - max_exact)).astype(jnp.int32)
    large = jnp.minimum(large, half - 1)
    return ret + jnp.where(n < max_exact, n, large)


def _bias_from_bucket(bucket, rb_ref, h):
    out = jnp.zeros(bucket.shape, f32)
    for j in range(N_BUCKETS):
        out = jnp.where(bucket == j, rb_ref[j, h], out)
    return out


def _block_time(r):
    return (r % NSUB) * SUBLEN + r // NSUB


def _bias_body(rb_ref, pb_ref, sbc_ref, sbn_ref, *, past):
    h = pl.program_id(0)
    tq = _block_time(lax.broadcasted_iota(jnp.int32, (TB, TB), 0))
    tk = _block_time(lax.broadcasted_iota(jnp.int32, (TB, TB), 1))
    b0 = _bias_from_bucket(_bucket(tk - tq), rb_ref, h)
    pb_ref[0] = jnp.where(tk // CHUNK <= tq // CHUNK, b0, MASK_NEG)
    pb_ref[1] = _bias_from_bucket(_bucket(tk - tq - TB), rb_ref, h)
    pb_ref[2] = _bias_from_bucket(_bucket(tk - tq - 2 * TB), rb_ref, h)
    nq = sbc_ref.shape[0]
    qpos = past + lax.broadcasted_iota(jnp.int32, (nq, past), 0)
    kpos = lax.broadcasted_iota(jnp.int32, (nq, past), 1)
    sbc_ref[...] = _bias_from_bucket(_bucket(kpos - qpos), rb_ref, h)
    qn = lax.broadcasted_iota(jnp.int32, (nq, nq), 0)
    kn = lax.broadcasted_iota(jnp.int32, (nq, nq), 1)
    sbn_ref[...] = _bias_from_bucket(_bucket(kn - qn), rb_ref, h)


def bias_tiles(rel_bias, past):
    return pl.pallas_call(
        functools.partial(_bias_body, past=past),
        grid=(N_HEADS,),
        in_specs=[pl.BlockSpec(memory_space=pltpu.SMEM)],
        out_specs=[
            pl.BlockSpec((None, 3, TB, TB), lambda h: (h, 0, 0, 0)),
            pl.BlockSpec((None, SUBLEN, past), lambda h: (h, 0, 0)),
            pl.BlockSpec((None, SUBLEN, SUBLEN), lambda h: (h, 0, 0)),
        ],
        out_shape=[
            jax.ShapeDtypeStruct((N_HEADS, 3, TB, TB), f32),
            jax.ShapeDtypeStruct((N_HEADS, SUBLEN, past), f32),
            jax.ShapeDtypeStruct((N_HEADS, SUBLEN, SUBLEN), f32),
        ],
        compiler_params=_cparams(("parallel",)),
        name="bias_tiles",
    )(rel_bias)


def _stack_maps(q):
    lane = lax.broadcasted_iota(jnp.int32, q.shape, 1)
    zero = jnp.zeros_like(q)
    return jnp.concatenate([jnp.where(lane < QK_DIM, q, zero), jnp.where(lane >= QK_DIM, q, zero)], axis=0)


def _online_softmax_step(s, v, m_prev, l_prev, acc_prev):
    m_new = jnp.maximum(m_prev, jnp.max(s, axis=-1, keepdims=True))
    alpha = jnp.exp(m_prev - m_new)
    p = jnp.exp(s - m_new)
    l_new = alpha * l_prev + jnp.sum(p, axis=-1, keepdims=True)
    acc_new = alpha * acc_prev + jnp.dot(p.astype(bf16), v, preferred_element_type=f32)
    return m_new, l_new, acc_new


def _diff_combine(l, acc, lam, g, nq, lam_init):
    o = acc[:nq] / l[:nq] - lam * (acc[nq:] / l[nq:])
    ms = jnp.mean(o * o, axis=-1, keepdims=True)
    return o * lax.rsqrt(ms + EPS) * g * (1.0 - lam_init)


def _prompt_attn_body(lam_ref, q_ref, k_ref, v_ref, pb_ref, g_ref, o_ref, m_ref, l_ref, acc_ref, *, lam_init):
    qi = pl.program_id(1)
    kj = pl.program_id(2)

    @pl.when(kj == 0)
    def _():
        m_ref[...] = jnp.full(m_ref.shape, -jnp.inf, f32)
        l_ref[...] = jnp.zeros(l_ref.shape, f32)
        acc_ref[...] = jnp.zeros(acc_ref.shape, f32)

    @pl.when(kj <= qi)
    def _():
        for h in range(N_HEADS):
            sl = slice(h * V_DIM, (h + 1) * V_DIM)
            qq = _stack_maps(q_ref[:, sl])
            s = _dot_nt(qq, k_ref[:, sl])
            s = (s.reshape(2, TB, TB) + pb_ref[h][None]).reshape(2 * TB, TB)
            m, l, acc = _online_softmax_step(s, v_ref[:, sl], m_ref[h], l_ref[h], acc_ref[h])
            m_ref[h] = m
            l_ref[h] = l
            acc_ref[h] = acc

    @pl.when(kj == pl.num_programs(2) - 1)
    def _():
        for h in range(N_HEADS):
            o = _diff_combine(l_ref[h], acc_ref[h], lam_ref[0], g_ref[...], TB, lam_init)
            o_ref[:, h * V_DIM:(h + 1) * V_DIM] = o.astype(o_ref.dtype)


def prompt_attention(qn, kb, vb, pbias, lam, subln_g, *, n_seq, blocks_per_seq, lam_init):
    m = qn.shape[0]
    nb = blocks_per_seq
    kv = pl.BlockSpec((TB, ATT_WIDTH), lambda b, i, j: (b * nb + jnp.minimum(i, j), 0))
    return pl.pallas_call(
        functools.partial(_prompt_attn_body, lam_init=lam_init),
        grid=(n_seq, nb, nb),
        in_specs=[
            pl.BlockSpec(memory_space=pltpu.SMEM),
            pl.BlockSpec((TB, Q_WIDTH), lambda b, i, j: (b * nb + i, 0)),
            kv, kv,
            pl.BlockSpec((N_HEADS, None, TB, TB), lambda b, i, j: (0, jnp.clip(i - j, 0, 2), 0, 0)),
            pl.BlockSpec((1, V_DIM), lambda b, i, j: (0, 0)),
        ],
        out_specs=pl.BlockSpec((TB, ATT_WIDTH), lambda b, i, j: (b * nb + i, 0)),
        out_shape=jax.ShapeDtypeStruct((m, ATT_WIDTH), bf16),
        scratch_shapes=[pltpu.VMEM((N_HEADS, 2 * TB, 1), f32), pltpu.VMEM((N_HEADS, 2 * TB, 1), f32),
                        pltpu.VMEM((N_HEADS, 2 * TB, V_DIM), f32)],
        compiler_params=_cparams(("parallel", "parallel", "arbitrary")),
        name="prompt_attention",
    )(lam, qn, kb, vb, pbias, subln_g.reshape(1, V_DIM))


def _sample_attn_body(lam_ref, q_ref, kn_ref, vn_ref, ck_ref, cv_ref, sbc_ref, sbn_ref, g_ref, o_ref,
                      m_ref, l_ref, acc_ref, *, lam_init):
    kj = pl.program_id(1)
    nq = q_ref.shape[0]
    for h in range(N_HEADS):
        sl = slice(h * V_DIM, (h + 1) * V_DIM)
        qq = _stack_maps(q_ref[:, sl])

        @pl.when(kj == 0)
        def _():
            s = _dot_nt(qq, kn_ref[:, sl].astype(bf16))
            s = (s.reshape(2, nq, nq) + sbn_ref[h][None]).reshape(2 * nq, nq)
            m = jnp.max(s, axis=-1, keepdims=True)
            p = jnp.exp(s - m)
            m_ref[h] = m
            l_ref[h] = jnp.sum(p, axis=-1, keepdims=True)
            acc_ref[h] = jnp.dot(p.astype(bf16), vn_ref[:, sl].astype(bf16), preferred_element_type=f32)

        ck = ck_ref.shape[0] // N_HEADS
        head_rows = pl.ds(h, ck, stride=N_HEADS)
        s = _dot_nt(qq, ck_ref[head_rows, :].astype(bf16))
        s = (s.reshape(2, nq, ck) + sbc_ref[h][None]).reshape(2 * nq, ck)
        m, l, acc = _online_softmax_step(s, cv_ref[head_rows, :].astype(bf16), m_ref[h], l_ref[h], acc_ref[h])
        m_ref[h] = m
        l_ref[h] = l
        acc_ref[h] = acc

        @pl.when(kj == pl.num_programs(1) - 1)
        def _():
            o = _diff_combine(l_ref[h], acc_ref[h], lam_ref[0], g_ref[...], nq, lam_init)
            o_ref[:, sl] = o.astype(o_ref.dtype)


def sample_attention(q_s, kn_s, v_s, cache_k, cache_v, sbc, sbn, lam, subln_g, *, layer, lam_init):
    nb, nq, _ = q_s.shape
    past = cache_k.shape[2] // N_HEADS
    chunk = min(past, 1024)
    return pl.pallas_call(
        functools.partial(_sample_attn_body, lam_init=lam_init),
        grid=(nb, past // chunk),
        in_specs=[
            pl.BlockSpec(memory_space=pltpu.SMEM),
            pl.BlockSpec((None, nq, ATT_WIDTH), lambda b, j: (b, 0, 0)),
            pl.BlockSpec((None, nq, ATT_WIDTH), lambda b, j: (b, 0, 0)),
            pl.BlockSpec((None, nq, ATT_WIDTH), lambda b, j: (b, 0, 0)),
            pl.BlockSpec((None, None, chunk * N_HEADS, V_DIM), lambda b, j: (layer, b, j, 0)),
            pl.BlockSpec((None, None, chunk * N_HEADS, V_DIM), lambda b, j: (layer, b, j, 0)),
            pl.BlockSpec((N_HEADS, nq, chunk), lambda b, j: (0, 0, j)),
            pl.BlockSpec((N_HEADS, nq, nq), lambda b, j: (0, 0, 0)),
            pl.BlockSpec((1, V_DIM), lambda b, j: (0, 0)),
        ],
        out_specs=pl.BlockSpec((None, nq, ATT_WIDTH), lambda b, j: (b, 0, 0)),
        out_shape=jax.ShapeDtypeStruct((nb, nq, ATT_WIDTH), bf16),
        scratch_shapes=[pltpu.VMEM((N_HEADS, 2 * nq, 1), f32), pltpu.VMEM((N_HEADS, 2 * nq, 1), f32),
                        pltpu.VMEM((N_HEADS, 2 * nq, V_DIM), f32)],
        compiler_params=_cparams(("parallel", "arbitrary")),
        name="sample_attention",
    )(lam, q_s, kn_s, v_s, cache_k, cache_v, sbc, sbn, subln_g.reshape(1, V_DIM))


def _chain_carries(c0, mult, add):
    rows = [c0]
    c = c0
    for s in range(NSUB):
        c = mult[s:s + 1] * c + add[s:s + 1]
        if s + 1 < NSUB:
            rows.append(c)
    return jnp.concatenate(rows, axis=0), c


def _rglru_body(xrg_ref, xgate_ref, head_s_ref, h0_s_ref, cw_ref, cb_ref, wa_ref, wx_ref, ba_ref, bx_ref,
                lam_ref, hg_ref, htail_ref, ctail_ref, xbuf_ref, a_ref, b_ref, hcar_ref, ptail_ref,
                *, blocks_per_seq, n_prompt_blocks):
    g = pl.program_id(0)
    is_sample = g >= n_prompt_blocks
    hl = (CONV_W - 1) * NSUB

    @pl.when(g % blocks_per_seq == 0)
    def _():
        hcar_ref[...] = jnp.zeros(hcar_ref.shape, f32)
        ptail_ref[...] = jnp.zeros(ptail_ref.shape, f32)

    x = xrg_ref[...]
    tail = x[TB - hl:]
    sub = lax.broadcasted_iota(jnp.int32, (NSUB, RG_WIDTH), 0)
    for k in range(CONV_W - 1):
        rs = slice(k * NSUB, (k + 1) * NSUB)
        prompt_head = jnp.where(sub == 0, pltpu.roll(ptail_ref[rs, :], 1, 0), pltpu.roll(tail[rs], 1, 0))
        xbuf_ref[rs, :] = jnp.where(is_sample, head_s_ref[rs, :], prompt_head)
    xbuf_ref[hl:, :] = x
    ptail_ref[...] = tail
    ctail_ref[...] = tail

    xc = cb_ref[...] + cw_ref[CONV_W - 1:CONV_W, :] * x
    for j in range(1, CONV_W):
        xc = xc + cw_ref[CONV_W - 1 - j:CONV_W - j, :] * xbuf_ref[hl - j * NSUB:hl - j * NSUB + TB, :]

    xcb = xc.astype(bf16)
    r_parts, i_parts = [], []
    for blk in range(RG_BLOCKS):
        sl = slice(blk * RG_BLOCK, (blk + 1) * RG_BLOCK)
        r_parts.append(jnp.dot(xcb[:, sl], wa_ref[blk], preferred_element_type=f32))
        i_parts.append(jnp.dot(xcb[:, sl], wx_ref[blk], preferred_element_type=f32))
    r = jax.nn.sigmoid(jnp.concatenate(r_parts, axis=1) + ba_ref[...])
    ig = jax.nn.sigmoid(jnp.concatenate(i_parts, axis=1) + bx_ref[...])
    nl = -lam_ref[...]
    softplus = jnp.maximum(nl, 0.0) + jnp.log1p(jnp.exp(-jnp.abs(nl)))
    a = jnp.exp(-RG_C * r * softplus)
    a_ref[...] = a
    b_ref[...] = jnp.sqrt(1.0 - a * a) * (ig * xc)

    def rows(i):
        return pl.ds(pl.multiple_of(i * NSUB, NSUB), NSUB)

    def local(i, carry):
        p, e = carry
        ai = a_ref[rows(i), :]
        return p * ai, ai * e + b_ref[rows(i), :]

    p, e = lax.fori_loop(0, SUBLEN, local,
                         (jnp.ones((NSUB, RG_WIDTH), f32), jnp.zeros((NSUB, RG_WIDTH), f32)), unroll=4)
    entry, exit_state = _chain_carries(hcar_ref[...], p, e)
    hcar_ref[...] = exit_state
    h_init = jnp.where(is_sample, h0_s_ref[...], entry)

    def step(i, hprev):
        hcur = a_ref[rows(i), :] * hprev + b_ref[rows(i), :]
        b_ref[rows(i), :] = hcur
        return hcur

    h_last = lax.fori_loop(0, SUBLEN, step, h_init, unroll=4)
    htail_ref[...] = h_last
    hg_ref[...] = (b_ref[...] * _gelu(xgate_ref[...])).astype(hg_ref.dtype)


def rglru_mixer(proj, head_s, h0_s, conv_w, conv_b, wa, wx, ba, bx, lam, *, blocks_per_seq, n_prompt_blocks):
    m = proj.shape[0]
    nblk = m // TB
    hl = (CONV_W - 1) * NSUB
    row = lambda v: v.reshape(1, RG_WIDTH)
    full = lambda shape: pl.BlockSpec(shape, lambda g: (0,) * len(shape))
    return pl.pallas_call(
        functools.partial(_rglru_body, blocks_per_seq=blocks_per_seq, n_prompt_blocks=n_prompt_blocks),
        grid=(nblk,),
        in_specs=[
            pl.BlockSpec((TB, RG_WIDTH), lambda g: (g, COL_RG)),
            pl.BlockSpec((TB, RG_WIDTH), lambda g: (g, COL_GATE)),
            full((hl, RG_WIDTH)),
            full((NSUB, RG_WIDTH)),
            full((CONV_W, RG_WIDTH)),
            full((1, RG_WIDTH)),
            full((RG_BLOCKS, RG_BLOCK, RG_BLOCK)),
            full((RG_BLOCKS, RG_BLOCK, RG_BLOCK)),
            full((1, RG_WIDTH)),
            full((1, RG_WIDTH)),
            full((1, RG_WIDTH)),
        ],
        out_specs=[
            pl.BlockSpec((TB, RG_WIDTH), lambda g: (g, 0)),
            pl.BlockSpec((None, NSUB, RG_WIDTH), lambda g: (g, 0, 0)),
            pl.BlockSpec((None, hl, RG_WIDTH), lambda g: (g, 0, 0)),
        ],
        out_shape=[
            jax.ShapeDtypeStruct((m, RG_WIDTH), bf16),
            jax.ShapeDtypeStruct((nblk, NSUB, RG_WIDTH), f32),
            jax.ShapeDtypeStruct((nblk, hl, RG_WIDTH), f32),
        ],
        scratch_shapes=[
            pltpu.VMEM((hl + TB, RG_WIDTH), f32),
            pltpu.VMEM((TB, RG_WIDTH), f32),
            pltpu.VMEM((TB, RG_WIDTH), f32),
            pltpu.VMEM((1, RG_WIDTH), f32),
            pltpu.VMEM((hl, RG_WIDTH), f32),
        ],
        compiler_params=_cparams(("arbitrary",)),
        name="rglru_mixer",
    )(proj, proj, head_s, h0_s, conv_w, row(conv_b), wa.astype(bf16), wx.astype(bf16), row(ba), row(bx),
      row(lam))


S5_CHUNKS = 8
S5_CH_IN = S5_WIDTH // S5_CHUNKS
S5_CH_ST = S5_LANES // S5_CHUNKS
S5_SCAN_LANES = 512


def _cmul(ar, ai, br, bi):
    return ar * br - ai * bi, ar * bi + ai * br


def _s5_body(u_ref, s0r_ref, s0i_ref, abr_ref, abi_ref, bdr_ref, bdi_ref, cdr_ref, cdi_ref, d_ref, gw_ref,
             gb_ref, gs_ref, str_ref, sti_ref, xr_ref, xi_ref, car_ref, cai_ref,
             *, blocks_per_seq, n_prompt_blocks):
    g = pl.program_id(0)
    is_sample = g >= n_prompt_blocks

    @pl.when(g % blocks_per_seq == 0)
    def _():
        car_ref[...] = jnp.zeros(car_ref.shape, f32)
        cai_ref[...] = jnp.zeros(cai_ref.shape, f32)

    u = u_ref[...]
    ub = u.astype(bf16)
    for c in range(S5_CHUNKS):
        ci = slice(c * S5_CH_IN, (c + 1) * S5_CH_IN)
        cs = slice(c * S5_CH_ST, (c + 1) * S5_CH_ST)
        xr_ref[:, cs] = jnp.dot(ub[:, ci], bdr_ref[c], preferred_element_type=f32)
        xi_ref[:, cs] = jnp.dot(ub[:, ci], bdi_ref[c], preferred_element_type=f32)

    def rows(i):
        return pl.ds(pl.multiple_of(i * NSUB, NSUB), NSUB)

    for c in range(S5_LANES // S5_SCAN_LANES):
        ls = slice(c * S5_SCAN_LANES, (c + 1) * S5_SCAN_LANES)
        ar1 = abr_ref[:, ls]
        ai1 = abi_ref[:, ls]
        ar = jnp.broadcast_to(ar1, (NSUB, S5_SCAN_LANES))
        ai = jnp.broadcast_to(ai1, (NSUB, S5_SCAN_LANES))

        def local(i, carry, ls=ls, ar=ar, ai=ai):
            er, ei = carry
            nr, ni = _cmul(ar, ai, er, ei)
            return nr + xr_ref[rows(i), ls], ni + xi_ref[rows(i), ls]

        zero = jnp.zeros((NSUB, S5_SCAN_LANES), f32)
        er, ei = lax.fori_loop(0, SUBLEN, local, (zero, zero), unroll=4)
        pr, pi = ar1, ai1
        for _ in range(int(math.log2(SUBLEN))):
            pr, pi = _cmul(pr, pi, pr, pi)
        rows_r, rows_i = [car_ref[:, ls]], [cai_ref[:, ls]]
        cr, ci_ = rows_r[0], rows_i[0]
        for s in range(NSUB):
            mr, mi = _cmul(pr, pi, cr, ci_)
            cr, ci_ = mr + er[s:s + 1], mi + ei[s:s + 1]
            if s + 1 < NSUB:
                rows_r.append(cr)
                rows_i.append(ci_)
        car_ref[:, ls] = cr
        cai_ref[:, ls] = ci_
        init_r = jnp.where(is_sample, s0r_ref[:, ls], jnp.concatenate(rows_r, axis=0))
        init_i = jnp.where(is_sample, s0i_ref[:, ls], jnp.concatenate(rows_i, axis=0))

        def step(i, carry, ls=ls, ar=ar, ai=ai):
            sr, si = carry
            nr, ni = _cmul(ar, ai, sr, si)
            nr = nr + xr_ref[rows(i), ls]
            ni = ni + xi_ref[rows(i), ls]
            xr_ref[rows(i), ls] = nr
            xi_ref[rows(i), ls] = ni
            return nr, ni

        sr, si = lax.fori_loop(0, SUBLEN, step, (init_r, init_i), unroll=4)
        str_ref[:, ls] = sr
        sti_ref[:, ls] = si

    y_parts = []
    for c in range(S5_CHUNKS):
        cs = slice(c * S5_CH_ST, (c + 1) * S5_CH_ST)
        y_parts.append(jnp.dot(xr_ref[:, cs].astype(bf16), cdr_ref[c], preferred_element_type=f32)
                       + jnp.dot(xi_ref[:, cs].astype(bf16), cdi_ref[c], preferred_element_type=f32))
    y = jnp.concatenate(y_parts, axis=1) + d_ref[...] * u
    gy = _gelu(y)
    z = jnp.dot(gy.astype(bf16), gw_ref[...], preferred_element_type=f32) + gb_ref[...]
    gs_ref[...] = (gy * jax.nn.sigmoid(z)).astype(gs_ref.dtype)


def _s5_discretise(lam_re, lam_im, log_step, b_re, b_im, c_re, c_im):
    step = jnp.exp(log_step)[:, None]
    mag = jnp.exp(lam_re * step)
    ab_re, ab_im = mag * jnp.cos(lam_im * step), mag * jnp.sin(lam_im * step)
    den = lam_re * lam_re + lam_im * lam_im
    nr, ni = ab_re - 1.0, ab_im
    co_re = (nr * lam_re + ni * lam_im) / den
    co_im = (ni * lam_re - nr * lam_im) / den
    bb_re = co_re[..., None] * b_re - co_im[..., None] * b_im
    bb_im = co_re[..., None] * b_im + co_im[..., None] * b_re
    gpc = S5_GROUPS // S5_CHUNKS
    eye = jnp.eye(gpc, dtype=f32)

    def in_chunks(bb):
        t = bb.reshape(S5_CHUNKS, gpc, S5_STATE, S5_GROUP)
        return jnp.einsum('cgph,gk->cghkp', t, eye).reshape(S5_CHUNKS, S5_CH_IN, S5_CH_ST).astype(bf16)

    def out_chunks(cc):
        t = cc.reshape(S5_CHUNKS, gpc, S5_GROUP, S5_STATE)
        return jnp.einsum('cghp,gk->cgpkh', t, eye).reshape(S5_CHUNKS, S5_CH_ST, S5_CH_IN).astype(bf16)

    return (ab_re.reshape(1, S5_LANES), ab_im.reshape(1, S5_LANES), in_chunks(bb_re), in_chunks(bb_im),
            out_chunks(c_re), out_chunks(-c_im))


def s5_mixer(proj, s0_re, s0_im, disc, d, glu_w, glu_b, *, layer, blocks_per_seq, n_prompt_blocks):
    m = proj.shape[0]
    nblk = m // TB
    ab_re, ab_im, bd_re, bd_im, cd_re, cd_im = disc
    full = lambda shape: pl.BlockSpec(shape, lambda g: (0,) * len(shape))
    return pl.pallas_call(
        functools.partial(_s5_body, blocks_per_seq=blocks_per_seq, n_prompt_blocks=n_prompt_blocks),
        grid=(nblk,),
        in_specs=[
            pl.BlockSpec((TB, S5_WIDTH), lambda g: (g, COL_S5)),
            full((NSUB, S5_LANES)),
            full((NSUB, S5_LANES)),
            full((1, S5_LANES)),
            full((1, S5_LANES)),
            full((S5_CHUNKS, S5_CH_IN, S5_CH_ST)),
            full((S5_CHUNKS, S5_CH_IN, S5_CH_ST)),
            full((S5_CHUNKS, S5_CH_ST, S5_CH_IN)),
            full((S5_CHUNKS, S5_CH_ST, S5_CH_IN)),
            full((1, S5_WIDTH)),
            pl.BlockSpec((None, S5_WIDTH, S5_WIDTH), lambda g: (layer, 0, 0)),
            full((1, S5_WIDTH)),
        ],
        out_specs=[
            pl.BlockSpec((TB, S5_WIDTH), lambda g: (g, 0)),
            pl.BlockSpec((None, NSUB, S5_LANES), lambda g: (g, 0, 0)),
            pl.BlockSpec((None, NSUB, S5_LANES), lambda g: (g, 0, 0)),
        ],
        out_shape=[
            jax.ShapeDtypeStruct((m, S5_WIDTH), bf16),
            jax.ShapeDtypeStruct((nblk, NSUB, S5_LANES), f32),
            jax.ShapeDtypeStruct((nblk, NSUB, S5_LANES), f32),
        ],
        scratch_shapes=[
            pltpu.VMEM((TB, S5_LANES), f32),
            pltpu.VMEM((TB, S5_LANES), f32),
            pltpu.VMEM((1, S5_LANES), f32),
            pltpu.VMEM((1, S5_LANES), f32),
        ],
        compiler_params=_cparams(("arbitrary",)),
        name="s5_mixer",
    )(proj, s0_re, s0_im, ab_re, ab_im, bd_re, bd_im, cd_re, cd_im, d.reshape(1, S5_WIDTH),
      glu_w, glu_b.reshape(1, S5_WIDTH))


def _merge_body(att_ref, rg_ref, s5_ref, wa_ref, wr_ref, ws_ref, g0_ref, g1_ref, g2_ref, b0_ref, b1_ref,
                b2_ref, o_ref):
    acc = jax.nn.sigmoid(g0_ref[...] + b0_ref[...]) * jnp.dot(att_ref[...], wa_ref[...],
                                                               preferred_element_type=f32)
    acc = acc + jax.nn.sigmoid(g1_ref[...] + b1_ref[...]) * jnp.dot(rg_ref[...], wr_ref[...],
                                                                     preferred_element_type=f32)
    acc = acc + jax.nn.sigmoid(g2_ref[...] + b2_ref[...]) * jnp.dot(s5_ref[...], ws_ref[...],
                                                                     preferred_element_type=f32)
    o_ref[...] = acc.astype(o_ref.dtype)


def gated_merge(o_att, hg, gs, w_att_o, w_rg_o, w_s5_o, proj, b_gate, *, layer, tm, tn):
    m = o_att.shape[0]
    k = o_att.shape[1]
    gcol = COL_GATES // tn
    bcol = D_MODEL // tn
    act = pl.BlockSpec((tm, k), lambda i, j: (i, 0))
    wsp = pl.BlockSpec((None, k, tn), lambda i, j: (layer, 0, j))
    gate = lambda br: pl.BlockSpec((tm, tn), lambda i, j: (i, gcol + br * bcol + j))
    bias = lambda br: pl.BlockSpec((1, tn), lambda i, j: (0, br * bcol + j))
    bg = b_gate.reshape(1, N_BRANCH * D_MODEL)
    return pl.pallas_call(
        _merge_body,
        grid=(m // tm, D_MODEL // tn),
        in_specs=[act, act, act, wsp, wsp, wsp, gate(0), gate(1), gate(2), bias(0), bias(1), bias(2)],
        out_specs=pl.BlockSpec((tm, tn), lambda i, j: (i, j)),
        out_shape=jax.ShapeDtypeStruct((m, D_MODEL), bf16),
        compiler_params=_cparams(("parallel", "arbitrary")),
        name="gated_merge",
    )(o_att, hg, gs, w_att_o, w_rg_o, w_s5_o, proj, proj, proj, bg, bg, bg)


def _residual_matmul_body(x_ref, a_ref, w_ref, o_ref):
    o_ref[...] = x_ref[...] + jnp.dot(a_ref[...], w_ref[...], preferred_element_type=f32)


def residual_matmul(x, a, w, *, layer, tm, tn):
    m, n = x.shape
    k = a.shape[1]
    return pl.pallas_call(
        _residual_matmul_body,
        grid=(m // tm, n // tn),
        in_specs=[
            pl.BlockSpec((tm, tn), lambda i, j: (i, j)),
            pl.BlockSpec((tm, k), lambda i, j: (i, 0)),
            pl.BlockSpec((None, k, tn), lambda i, j: (layer, 0, j)),
        ],
        out_specs=pl.BlockSpec((tm, tn), lambda i, j: (i, j)),
        out_shape=jax.ShapeDtypeStruct((m, n), f32),
        compiler_params=_cparams(("parallel", "arbitrary")),
        name="residual_matmul",
    )(x, a, w)


PEER_TT = 256
PEER_ET = 1024
PEER_I1 = PEER_ET // N_KEYS
PEER_PIECE = 32


def _top_values(s, k):
    out = []
    for _ in range(k):
        mx = jnp.max(s, axis=0, keepdims=True)
        out.append(mx)
        s = jnp.where(s == mx, -jnp.inf, s)
    return out


def _peer_route_body(q_ref, sk_ref, s2_ref, e2_ref, thr_ref, c1_ref):
    for h in range(PEER_HEADS):
        s1 = _dot_nt(sk_ref[0], q_ref[:, (2 * h) * N_KEYS:(2 * h + 1) * N_KEYS])
        s2 = _dot_nt(sk_ref[1], q_ref[:, (2 * h + 1) * N_KEYS:(2 * h + 2) * N_KEYS])
        v1 = _top_values(s1, PEER_TOPK)
        v2 = _top_values(s2, PEER_TOPK)
        cand = [v1[k1] + v2[k2] for k1 in range(PEER_TOPK) for k2 in range(PEER_TOPK // (k1 + 1))]
        pad = (-len(cand)) % SUBLANES
        cand = jnp.concatenate(cand + [jnp.full_like(cand[0], -jnp.inf)] * pad, axis=0)
        top = _top_values(cand, PEER_TOPK)
        theta = top[-1]
        z = top[0] * 0.0
        for t in top:
            z = z + jnp.exp(t - top[0])
        s2_ref[h] = s2
        e2_ref[h] = jnp.exp(s2 - v2[0])
        thr_ref[h] = theta - s1
        c1_ref[h] = jnp.exp(s1 - v1[0]) / z


def peer_route(q, subkeys):
    m = q.shape[0]
    shp = jax.ShapeDtypeStruct((PEER_HEADS, N_KEYS, m), f32)
    spec = pl.BlockSpec((PEER_HEADS, N_KEYS, PEER_TT), lambda i: (0, 0, i))
    return pl.pallas_call(
        _peer_route_body,
        grid=(m // PEER_TT,),
        in_specs=[
            pl.BlockSpec((PEER_TT, q.shape[1]), lambda i: (i, 0)),
            pl.BlockSpec((2, N_KEYS, N_KEYS), lambda i: (0, 0, 0)),
        ],
        out_specs=[spec] * 4,
        out_shape=[shp] * 4,
        compiler_params=_cparams(("parallel",)),
        name="peer_route",
    )(q, subkeys)


def _peer_expert_body(xnt_ref, u_ref, vt_ref, s2_ref, e2_ref, thr_ref, c1_ref, o_ref, act_ref, zt_ref):
    j = pl.program_id(1)

    @pl.when(j == 0)
    def _():
        o_ref[...] = jnp.zeros(o_ref.shape, f32)

    zt_ref[...] = jnp.dot(u_ref[...], xnt_ref[...], preferred_element_type=f32)
    assert PEER_I1 == SUBLANES
    group = pl.ds(pl.multiple_of(j * PEER_I1, PEER_I1), PEER_I1)
    for tc in range(zt_ref.shape[1] // LANES):
        ts = slice(tc * LANES, (tc + 1) * LANES)
        for pc in range(N_KEYS // PEER_PIECE):
            ks = slice(pc * PEER_PIECE, (pc + 1) * PEER_PIECE)
            w = [jnp.zeros((PEER_PIECE, LANES), f32) for _ in range(PEER_I1)]
            for h in range(PEER_HEADS):
                s2 = s2_ref[h, ks, ts]
                e2 = e2_ref[h, ks, ts]
                thr = thr_ref[h, group, ts]
                c1 = c1_ref[h, group, ts]
                for r in range(PEER_I1):
                    w[r] = w[r] + jnp.where(s2 >= thr[r:r + 1], e2 * c1[r:r + 1], 0.0)
            for r in range(PEER_I1):
                rs = slice(r * N_KEYS + pc * PEER_PIECE, r * N_KEYS + (pc + 1) * PEER_PIECE)
                act_ref[rs, ts] = (_gelu(zt_ref[rs, ts]) * w[r]).astype(bf16)
    o_ref[...] += jnp.dot(vt_ref[...], act_ref[...], preferred_element_type=f32)


def peer_experts(xnt, u_tab, vt_tab, s2, e2, thr, c1, *, layer, tt):
    d, m = xnt.shape
    once = pl.Buffered(1)
    tok = pl.BlockSpec((PEER_HEADS, N_KEYS, tt), lambda i, j: (0, 0, i), pipeline_mode=once)
    return pl.pallas_call(
        _peer_expert_body,
        grid=(m // tt, N_EXPERTS // PEER_ET),
        in_specs=[
            pl.BlockSpec((d, tt), lambda i, j: (0, i), pipeline_mode=once),
            pl.BlockSpec((None, PEER_ET, d), lambda i, j: (layer, j, 0)),
            pl.BlockSpec((None, d, PEER_ET), lambda i, j: (layer, 0, j)),
            tok, tok, tok, tok,
        ],
        out_specs=pl.BlockSpec((d, tt), lambda i, j: (0, i)),
        out_shape=jax.ShapeDtypeStruct((d, m), f32),
        scratch_shapes=[pltpu.VMEM((PEER_ET, tt), bf16), pltpu.VMEM((PEER_ET, tt), f32)],
        compiler_params=_cparams(("parallel", "arbitrary")),
        name="peer_experts",
    )(xnt, u_tab, vt_tab, s2, e2, thr, c1)


def _add_transposed_body(x_ref, dt_ref, o_ref):
    o_ref[...] = x_ref[...] + dt_ref[...].T


def add_transposed(x, dt, *, tm):
    m, d = x.shape
    return pl.pallas_call(
        _add_transposed_body,
        grid=(m // tm,),
        in_specs=[pl.BlockSpec((tm, d), lambda i: (i, 0)), pl.BlockSpec((d, tm), lambda i: (0, i))],
        out_specs=pl.BlockSpec((tm, d), lambda i: (i, 0)),
        out_shape=jax.ShapeDtypeStruct((m, d), f32),
        compiler_params=_cparams(("parallel",)),
        name="add_transposed",
    )(x, dt)


def _to_blocks(x):
    b, t, c = x.shape
    return x.reshape(b, t // TB, NSUB, SUBLEN, c).transpose(0, 1, 3, 2, 4).reshape(b * t, c)


def _from_blocks(y, b):
    m, c = y.shape
    t = m // b
    return y.reshape(b, t // TB, SUBLEN, NSUB, c).transpose(0, 1, 3, 2, 4).reshape(b, t, c)


def _layer(x, l, p, wb, tiles, dims, state):
    n_seq, blocks_per_seq, n_stream = dims
    n_prompt_blocks = n_seq * blocks_per_seq
    mp = n_prompt_blocks * TB
    pbias, sbc, sbn = tiles
    cache_k, cache_v, rg_h0, rg_conv0, s5_re0, s5_im0 = state
    tm = 3 * TB if x.shape[0] % (3 * TB) == 0 else TB
    lam_init = 0.8 - 0.6 * math.exp(-0.3 * l)
    lam = (jnp.exp(jnp.sum(p['lam_q1'] * p['lam_k1'])) - jnp.exp(jnp.sum(p['lam_q2'] * p['lam_k2']))
           + lam_init).reshape(1)

    proj = norm_matmul(x, p['norm1_g'], wb['w_in'], layer=l, tm=tm, tn=1024, out_dtype=f32)
    qn, kn, kb, vb = qk_norm(proj, p['qn_g'], p['kn_g'], tm=tm)

    o_att = prompt_attention(qn, kb, vb, pbias, lam, p['subln_g'], n_seq=n_seq,
                             blocks_per_seq=blocks_per_seq, lam_init=lam_init)
    stream = lambda a: a[mp:].reshape(SUBLEN, n_stream, -1).transpose(1, 0, 2)
    v_all = proj[:, COL_V * 1024:(COL_V + 1) * 1024]
    rows = lambda c: c.reshape(c.shape[0], c.shape[1], -1, V_DIM)
    o_s = sample_attention(stream(qn), stream(kn), stream(v_all), rows(cache_k), rows(cache_v), sbc, sbn, lam,
                           p['subln_g'], layer=l, lam_init=lam_init)
    o_att = lax.dynamic_update_slice(o_att, o_s.transpose(1, 0, 2).reshape(TB, ATT_WIDTH), (mp, 0))

    head_s = rg_conv0.transpose(1, 0, 2).reshape((CONV_W - 1) * NSUB, RG_WIDTH)
    hg, h_tail, c_tail = rglru_mixer(proj, head_s, rg_h0, p['rg_conv_w'], p['rg_conv_b'], p['rg_wa'],
                                     p['rg_wx'], p['rg_ba'], p['rg_bx'], p['rg_lambda'],
                                     blocks_per_seq=blocks_per_seq, n_prompt_blocks=n_prompt_blocks)
    disc = _s5_discretise(p['s5_lam_re'], p['s5_lam_im'], p['s5_log_step'], p['s5_b_re'], p['s5_b_im'],
                          p['s5_c_re'], p['s5_c_im'])
    gs, st_re, st_im = s5_mixer(proj, s5_re0.reshape(NSUB, S5_LANES), s5_im0.reshape(NSUB, S5_LANES), disc,
                                p['s5_d'], wb['s5_glu_w'], p['s5_glu_b'], layer=l,
                                blocks_per_seq=blocks_per_seq, n_prompt_blocks=n_prompt_blocks)

    merged = gated_merge(o_att, hg, gs, wb['w_att_o'], wb['w_rg_o'], wb['w_s5_o'], proj, p['b_gate'],
                         layer=l, tm=tm, tn=512)
    x = residual_matmul(x, merged, wb['w_out'], layer=l, tm=tm, tn=1024)

    q, xn2 = norm_matmul(x, p['norm2_g'], wb['peer_wq'], layer=l, tm=tm, tn=1024, out_dtype=bf16,
                         emit_xn=True)
    s2, e2, thr, c1 = peer_route(q, p['peer_subkeys'].astype(bf16))
    peer_t = peer_experts(xn2.T, wb['peer_u'], wb['peer_vt'], s2, e2, thr, c1, layer=l, tt=tm)
    x = add_transposed(x, peer_t, tm=TB)

    last = jnp.arange(n_seq) * blocks_per_seq + blocks_per_seq - 1
    hl = CONV_W - 1
    outs_p = (
        _from_blocks(kn[:mp], n_seq).reshape(n_seq, -1, N_HEADS, 2 * QK_DIM),
        _from_blocks(v_all[:mp], n_seq).reshape(n_seq, -1, N_HEADS, V_DIM),
        h_tail[last, NSUB - 1],
        c_tail[last].reshape(n_seq, hl, NSUB, RG_WIDTH)[:, :, NSUB - 1],
        st_re[last, NSUB - 1].reshape(n_seq, S5_GROUPS, S5_STATE),
        st_im[last, NSUB - 1].reshape(n_seq, S5_GROUPS, S5_STATE),
    )
    outs_s = (
        stream(kn).reshape(n_stream, SUBLEN, N_HEADS, 2 * QK_DIM),
        stream(v_all).reshape(n_stream, SUBLEN, N_HEADS, V_DIM),
        h_tail[n_prompt_blocks],
        c_tail[n_prompt_blocks].reshape(hl, NSUB, RG_WIDTH).transpose(1, 0, 2),
        st_re[n_prompt_blocks].reshape(n_stream, S5_GROUPS, S5_STATE),
        st_im[n_prompt_blocks].reshape(n_stream, S5_GROUPS, S5_STATE),
    )
    return x, outs_p, outs_s


def kernel(x_prompt, x_sample, cache_k, cache_v, state_rg_h, state_rg_conv, state_s5_re, state_s5_im, rel_bias, norm1_g, w_in, b_gate, qn_g, kn_g, lam_q1, lam_k1, lam_q2, lam_k2, subln_g, w_att_o, rg_conv_w, rg_conv_b, rg_wa, rg_ba, rg_wx, rg_bx, rg_lambda, w_rg_o, s5_lam_re, s5_lam_im, s5_log_step, s5_b_re, s5_b_im, s5_c_re, s5_c_im, s5_d, s5_glu_w, s5_glu_b, w_s5_o, w_out, norm2_g, peer_wq, peer_subkeys, peer_u, peer_v):
    n_seq, t_prompt, _ = x_prompt.shape
    n_stream, t_sample, _ = x_sample.shape
    depth = w_in.shape[0]
    assert n_stream == NSUB and t_sample == SUBLEN and t_prompt % TB == 0
    blocks_per_seq = t_prompt // TB
    dims = (n_seq, blocks_per_seq, n_stream)
    past = cache_k.shape[2]

    small = dict(norm1_g=norm1_g, b_gate=b_gate, qn_g=qn_g, kn_g=kn_g, lam_q1=lam_q1,
                 lam_k1=lam_k1, lam_q2=lam_q2, lam_k2=lam_k2, subln_g=subln_g,
                 rg_conv_w=rg_conv_w, rg_conv_b=rg_conv_b, rg_wa=rg_wa, rg_ba=rg_ba, rg_wx=rg_wx,
                 rg_bx=rg_bx, rg_lambda=rg_lambda, s5_lam_re=s5_lam_re,
                 s5_lam_im=s5_lam_im, s5_log_step=s5_log_step, s5_b_re=s5_b_re, s5_b_im=s5_b_im,
                 s5_c_re=s5_c_re, s5_c_im=s5_c_im, s5_d=s5_d, s5_glu_b=s5_glu_b,
                 norm2_g=norm2_g, peer_subkeys=peer_subkeys)
    wb = dict(w_in=w_in.astype(bf16), w_att_o=w_att_o.astype(bf16), w_rg_o=w_rg_o.astype(bf16),
              w_s5_o=w_s5_o.astype(bf16), w_out=w_out.astype(bf16), s5_glu_w=s5_glu_w.astype(bf16),
              peer_wq=peer_wq.astype(bf16), peer_u=peer_u.astype(bf16),
              peer_vt=jnp.swapaxes(peer_v, 1, 2).astype(bf16))

    tiles = bias_tiles(rel_bias, past)
    x = jnp.concatenate([_to_blocks(x_prompt), x_sample.transpose(1, 0, 2).reshape(TB, D_MODEL)], axis=0)
    outs_p, outs_s = [], []
    for l in range(depth):
        p = {k: v[l] for k, v in small.items()}
        state = (cache_k, cache_v, state_rg_h[l], state_rg_conv[l], state_s5_re[l], state_s5_im[l])
        x, op, os_ = _layer(x, l, p, wb, tiles, dims, state)
        outs_p.append(op)
        outs_s.append(os_)

    mp = n_seq * t_prompt
    y_prompt = _from_blocks(x[:mp], n_seq)
    y_sample = x[mp:].reshape(SUBLEN, n_stream, D_MODEL).transpose(1, 0, 2)
    stack = lambda outs, i: jnp.stack([o[i] for o in outs])
    return (y_prompt, y_sample) + tuple(stack(outs_p, i) for i in range(6)) + tuple(stack(outs_s, i)
                                                                                    for i in range(6))
```

```python
import functools
import math

import jax
import jax.numpy as jnp
from jax import lax
from jax.experimental import pallas as pl
from jax.experimental.pallas import tpu as pltpu

f32 = jnp.float32
bf16 = jnp.bfloat16

D_MODEL = 2048
CHUNK = 64
EPS = 1e-6
N_HEADS = 8
QK_DIM = 64
V_DIM = 128
Q_WIDTH = 1024
ATT_WIDTH = 1024
N_BUCKETS = 32
MAX_DISTANCE = 128
RG_WIDTH = 1024
RG_BLOCKS = 8
RG_BLOCK = 128
CONV_W = 4
RG_C = 8.0
S5_WIDTH = 1024
S5_GROUP = 16
S5_GROUPS = 64
S5_STATE = 64
S5_LANES = S5_GROUPS * S5_STATE
PEER_HEADS = 8
N_KEYS = 128
N_EXPERTS = N_KEYS * N_KEYS
PEER_TOPK = 16
N_BRANCH = 3
IN_COLS = 2 * Q_WIDTH + ATT_WIDTH + 2 * RG_WIDTH + S5_WIDTH + N_BRANCH * D_MODEL

SUBLANES = 8
LANES = 128
VMEM_LIMIT = 56 * 1024 * 1024

NSUB = SUBLANES
SUBLEN = 32
TB = NSUB * SUBLEN
MASK_NEG = -1e30

COL_Q, COL_K, COL_V, COL_RG, COL_GATE, COL_S5 = 0, 1, 2, 3, 4, 5
COL_GATES = 6 * 1024


def _cparams(sem):
    return pltpu.CompilerParams(dimension_semantics=sem, vmem_limit_bytes=VMEM_LIMIT)


def _gelu(x):
    return 0.5 * x * (1.0 + lax.erf(x * (2.0 ** -0.5)))


def _dot_nt(a, b):
    return lax.dot_general(a, b, (((1,), (1,)), ((), ())), preferred_element_type=f32)


def _dot_tn(a, b):
    return lax.dot_general(a, b, (((0,), (0,)), ((), ())), preferred_element_type=f32)


def _norm_matmul_body(x_ref, g_ref, w_ref, o_ref, *rest, emit_xn):
    xn_ref = rest[-1]

    @pl.when(pl.program_id(1) == 0)
    def _():
        x = x_ref[...]
        ms = jnp.mean(x * x, axis=-1, keepdims=True)
        xn = (x * lax.rsqrt(ms + EPS) * g_ref[...]).astype(bf16)
        xn_ref[...] = xn
        if emit_xn:
            rest[0][...] = xn

    o_ref[...] = jnp.dot(xn_ref[...], w_ref[...], preferred_element_type=f32).astype(o_ref.dtype)


def norm_matmul(x, g, w, *, layer, tm, tn, out_dtype, emit_xn=False):
    m, d = x.shape
    n = w.shape[2]
    out_shape = [jax.ShapeDtypeStruct((m, n), out_dtype)]
    out_specs = [pl.BlockSpec((tm, tn), lambda i, j: (i, j))]
    if emit_xn:
        out_shape.append(jax.ShapeDtypeStruct((m, d), bf16))
        out_specs.append(pl.BlockSpec((tm, d), lambda i, j: (i, 0)))
    res = pl.pallas_call(
        functools.partial(_norm_matmul_body, emit_xn=emit_xn),
        grid=(m // tm, n // tn),
        in_specs=[
            pl.BlockSpec((tm, d), lambda i, j: (i, 0)),
            pl.BlockSpec((1, d), lambda i, j: (0, 0)),
            pl.BlockSpec((None, d, tn), lambda i, j: (layer, 0, j)),
        ],
        out_specs=out_specs,
        out_shape=out_shape,
        scratch_shapes=[pltpu.VMEM((tm, d), bf16)],
        compiler_params=_cparams(("parallel", "arbitrary")),
        name="norm_matmul",
    )(x, g.reshape(1, d), w)
    return res if emit_xn else res[0]


def _group_mean_sq(x, ones_bd):
    sq = x * x
    hi = sq.astype(bf16)
    lo = (sq - hi.astype(f32)).astype(bf16)
    s = jnp.dot(hi, ones_bd, preferred_element_type=f32) + jnp.dot(lo, ones_bd, preferred_element_type=f32)
    return s * (1.0 / QK_DIM)


def _qknorm_body(q_ref, k_ref, v_ref, qg_ref, kg_ref, qo_ref, ko_ref, kb_ref, vb_ref):
    r = lax.broadcasted_iota(jnp.int32, (LANES, LANES), 0) // QK_DIM
    c = lax.broadcasted_iota(jnp.int32, (LANES, LANES), 1) // QK_DIM
    ones_bd = (r == c).astype(bf16)
    for t in range(Q_WIDTH // LANES):
        sl = slice(t * LANES, (t + 1) * LANES)
        q = q_ref[:, sl]
        k = k_ref[:, sl]
        qn = q * lax.rsqrt(_group_mean_sq(q, ones_bd) + EPS) * qg_ref[:, sl]
        kn = k * lax.rsqrt(_group_mean_sq(k, ones_bd) + EPS) * kg_ref[:, sl]
        qo_ref[:, sl] = (qn * (QK_DIM ** -0.5)).astype(bf16)
        ko_ref[:, sl] = kn
        kb_ref[:, sl] = kn.astype(bf16)
    vb_ref[...] = v_ref[...].astype(bf16)


def qk_norm(proj, qn_g, kn_g, *, tm):
    m = proj.shape[0]
    qg = jnp.tile(qn_g, Q_WIDTH // QK_DIM).reshape(1, Q_WIDTH)
    kg = jnp.tile(kn_g, Q_WIDTH // QK_DIM).reshape(1, Q_WIDTH)
    col = lambda c: pl.BlockSpec((tm, Q_WIDTH), lambda i: (i, c))
    gain = pl.BlockSpec((1, Q_WIDTH), lambda i: (0, 0))
    return pl.pallas_call(
        _qknorm_body,
        grid=(m // tm,),
        in_specs=[col(COL_Q), col(COL_K), col(COL_V), gain, gain],
        out_specs=[col(0)] * 4,
        out_shape=[jax.ShapeDtypeStruct((m, Q_WIDTH), bf16), jax.ShapeDtypeStruct((m, Q_WIDTH), f32),
                   jax.ShapeDtypeStruct((m, Q_WIDTH), bf16), jax.ShapeDtypeStruct((m, ATT_WIDTH), bf16)],
        compiler_params=_cparams(("parallel",)),
        name="qk_norm",
    )(proj, proj, proj, qg, kg)


def _bucket(rel):
    half = N_BUCKETS // 2
    max_exact = half // 2
    ret = jnp.where(rel > 0, half, 0)
    n = jnp.abs(rel)
    nf = jnp.maximum(n, 1).astype(f32)
    large = max_exact + (jnp.log(nf / max_exact) / math.log(MAX_DISTANCE / max_exact)
                         * (half - max_exact)).astype(jnp.int32)
    large = jnp.minimum(large, half - 1)
    return ret + jnp.where(n < max_exact, n, large)


def _bias_from_bucket(bucket, rb_ref, h):
    out = jnp.zeros(bucket.shape, f32)
    for j in range(N_BUCKETS):
        out = jnp.where(bucket == j, rb_ref[j, h], out)
    return out


def _block_time(r):
    return (r % NSUB) * SUBLEN + r // NSUB


def _bias_body(rb_ref, pb_ref, sbc_ref, sbn_ref, *, past):
    h = pl.program_id(0)
    tq = _block_time(lax.broadcasted_iota(jnp.int32, (TB, TB), 0))
    tk = _block_time(lax.broadcasted_iota(jnp.int32, (TB, TB), 1))
    b0 = _bias_from_bucket(_bucket(tk - tq), rb_ref, h)
    pb_ref[0] = jnp.where(tk // CHUNK <= tq // CHUNK, b0, MASK_NEG)
    pb_ref[1] = _bias_from_bucket(_bucket(tk - tq - TB), rb_ref, h)
    pb_ref[2] = _bias_from_bucket(_bucket(tk - tq - 2 * TB), rb_ref, h)
    nq = sbc_ref.shape[0]
    qpos = past + lax.broadcasted_iota(jnp.int32, (nq, past), 0)
    kpos = lax.broadcasted_iota(jnp.int32, (nq, past), 1)
    sbc_ref[...] = _bias_from_bucket(_bucket(kpos - qpos), rb_ref, h)
    qn = lax.broadcasted_iota(jnp.int32, (nq, nq), 0)
    kn = lax.broadcasted_iota(jnp.int32, (nq, nq), 1)
    sbn_ref[...] = _bias_from_bucket(_bucket(kn - qn), rb_ref, h)


def bias_tiles(rel_bias, past):
    return pl.pallas_call(
        functools.partial(_bias_body, past=past),
        grid=(N_HEADS,),
        in_specs=[pl.BlockSpec(memory_space=pltpu.SMEM)],
        out_specs=[
            pl.BlockSpec((None, 3, TB, TB), lambda h: (h, 0, 0, 0)),
            pl.BlockSpec((None, SUBLEN, past), lambda h: (h, 0, 0)),
            pl.BlockSpec((None, SUBLEN, SUBLEN), lambda h: (h, 0, 0)),
        ],
        out_shape=[
            jax.ShapeDtypeStruct((N_HEADS, 3, TB, TB), f32),
            jax.ShapeDtypeStruct((N_HEADS, SUBLEN, past), f32),
            jax.ShapeDtypeStruct((N_HEADS, SUBLEN, SUBLEN), f32),
        ],
        compiler_params=_cparams(("parallel",)),
        name="bias_tiles",
    )(rel_bias)


def _stack_maps(q):
    lane = lax.broadcasted_iota(jnp.int32, q.shape, 1)
    zero = jnp.zeros_like(q)
    return jnp.concatenate([jnp.where(lane < QK_DIM, q, zero), jnp.where(lane >= QK_DIM, q, zero)], axis=0)


def _online_softmax_step(s, v, m_prev, l_prev, acc_prev):
    m_new = jnp.maximum(m_prev, jnp.max(s, axis=-1, keepdims=True))
    alpha = jnp.exp(m_prev - m_new)
    p = jnp.exp(s - jnp.tile(m_new, (1, s.shape[1] // LANES)))
    l_new = alpha * l_prev + jnp.sum(p, axis=-1, keepdims=True)
    acc_new = alpha * acc_prev + jnp.dot(p.astype(bf16), v, preferred_element_type=f32)
    return m_new, l_new, acc_new


def _diff_combine(l, acc, lam, g, nq, lam_init):
    o = acc[:nq] / l[:nq] - lam * (acc[nq:] / l[nq:])
    ms = jnp.mean(o * o, axis=-1, keepdims=True)
    return o * lax.rsqrt(ms + EPS) * g * (1.0 - lam_init)


def _prompt_attn_body(lam_ref, q_ref, k_ref, v_ref, pb_ref, g_ref, o_ref, m_ref, l_ref, acc_ref, *, lam_init):
    qi = pl.program_id(1)
    kj = pl.program_id(2)

    @pl.when(kj == 0)
    def _():
        m_ref[...] = jnp.full(m_ref.shape, -jnp.inf, f32)
        l_ref[...] = jnp.zeros(l_ref.shape, f32)
        acc_ref[...] = jnp.zeros(acc_ref.shape, f32)

    @pl.when(kj <= qi)
    def _():
        for h in range(N_HEADS):
            sl = slice(h * V_DIM, (h + 1) * V_DIM)
            qq = _stack_maps(q_ref[:, sl])
            s = _dot_nt(qq, k_ref[:, sl])
            s = (s.reshape(2, TB, TB) + pb_ref[h][None]).reshape(2 * TB, TB)
            m, l, acc = _online_softmax_step(s, v_ref[:, sl], m_ref[h], l_ref[h], acc_ref[h])
            m_ref[h] = m
            l_ref[h] = l
            acc_ref[h] = acc

    @pl.when(kj == pl.num_programs(2) - 1)
    def _():
        for h in range(N_HEADS):
            o = _diff_combine(l_ref[h], acc_ref[h], lam_ref[0], g_ref[...], TB, lam_init)
            o_ref[:, h * V_DIM:(h + 1) * V_DIM] = o.astype(o_ref.dtype)


def prompt_attention(qn, kb, vb, pbias, lam, subln_g, *, n_seq, blocks_per_seq, lam_init):
    nb = blocks_per_seq
    m = n_seq * nb * TB
    kv = pl.BlockSpec((TB, ATT_WIDTH), lambda b, i, j: (b * nb + jnp.minimum(i, j), 0))
    return pl.pallas_call(
        functools.partial(_prompt_attn_body, lam_init=lam_init),
        grid=(n_seq, nb, nb),
        in_specs=[
            pl.BlockSpec(memory_space=pltpu.SMEM),
            pl.BlockSpec((TB, Q_WIDTH), lambda b, i, j: (b * nb + i, 0)),
            kv, kv,
            pl.BlockSpec((N_HEADS, None, TB, TB), lambda b, i, j: (0, jnp.clip(i - j, 0, 2), 0, 0)),
            pl.BlockSpec((1, V_DIM), lambda b, i, j: (0, 0)),
        ],
        out_specs=pl.BlockSpec((TB, ATT_WIDTH), lambda b, i, j: (b * nb + i, 0)),
        out_shape=jax.ShapeDtypeStruct((m, ATT_WIDTH), bf16),
        scratch_shapes=[pltpu.VMEM((N_HEADS, 2 * TB, LANES), f32), pltpu.VMEM((N_HEADS, 2 * TB, LANES), f32),
                        pltpu.VMEM((N_HEADS, 2 * TB, V_DIM), f32)],
        compiler_params=_cparams(("parallel", "parallel", "arbitrary")),
        name="prompt_attention",
    )(lam, qn, kb, vb, pbias, subln_g.reshape(1, V_DIM))


def _sample_attn_body(lam_ref, q_ref, kn_ref, vn_ref, ck_ref, cv_ref, sbc_ref, sbn_ref, g_ref, o_ref,
                      m_ref, l_ref, acc_ref, *, lam_init):
    kj = pl.program_id(1)
    nq = q_ref.shape[0]
    for h in range(N_HEADS):
        sl = slice(h * V_DIM, (h + 1) * V_DIM)
        qq = _stack_maps(q_ref[:, sl])

        @pl.when(kj == 0)
        def _():
            s = _dot_nt(qq, kn_ref[:, sl].astype(bf16))
            s = (s.reshape(2, nq, nq) + sbn_ref[h][None]).reshape(2 * nq, nq)
            m = jnp.max(s, axis=-1, keepdims=True)
            p = jnp.exp(s - m)
            m_ref[h] = jnp.broadcast_to(m, m_ref.shape[1:])
            l_ref[h] = jnp.broadcast_to(jnp.sum(p, axis=-1, keepdims=True), l_ref.shape[1:])
            acc_ref[h] = jnp.dot(p.astype(bf16), vn_ref[:, sl].astype(bf16), preferred_element_type=f32)

        ck = ck_ref.shape[0] // N_HEADS
        head_rows = pl.ds(h, ck, stride=N_HEADS)
        s = _dot_nt(qq, ck_ref[head_rows, :].astype(bf16))
        s = (s.reshape(2, nq, ck) + sbc_ref[h][None]).reshape(2 * nq, ck)
        m, l, acc = _online_softmax_step(s, cv_ref[head_rows, :].astype(bf16), m_ref[h], l_ref[h], acc_ref[h])
        m_ref[h] = m
        l_ref[h] = l
        acc_ref[h] = acc

        @pl.when(kj == pl.num_programs(1) - 1)
        def _():
            o = _diff_combine(l_ref[h], acc_ref[h], lam_ref[0], g_ref[...], nq, lam_init)
            o_ref[:, sl] = o.astype(o_ref.dtype)


def sample_attention(q_s, kn_s, v_s, cache_k, cache_v, sbc, sbn, lam, subln_g, *, layer, lam_init):
    nb, nq, _ = q_s.shape
    past = cache_k.shape[2] // N_HEADS
    chunk = min(past, 1024)
    return pl.pallas_call(
        functools.partial(_sample_attn_body, lam_init=lam_init),
        grid=(nb, past // chunk),
        in_specs=[
            pl.BlockSpec(memory_space=pltpu.SMEM),
            pl.BlockSpec((None, nq, ATT_WIDTH), lambda b, j: (b, 0, 0)),
            pl.BlockSpec((None, nq, ATT_WIDTH), lambda b, j: (b, 0, 0)),
            pl.BlockSpec((None, nq, ATT_WIDTH), lambda b, j: (b, 0, 0)),
            pl.BlockSpec((None, None, chunk * N_HEADS, V_DIM), lambda b, j: (layer, b, j, 0)),
            pl.BlockSpec((None, None, chunk * N_HEADS, V_DIM), lambda b, j: (layer, b, j, 0)),
            pl.BlockSpec((N_HEADS, nq, chunk), lambda b, j: (0, 0, j)),
            pl.BlockSpec((N_HEADS, nq, nq), lambda b, j: (0, 0, 0)),
            pl.BlockSpec((1, V_DIM), lambda b, j: (0, 0)),
        ],
        out_specs=pl.BlockSpec((None, nq, ATT_WIDTH), lambda b, j: (b, 0, 0)),
        out_shape=jax.ShapeDtypeStruct((nb, nq, ATT_WIDTH), bf16),
        scratch_shapes=[pltpu.VMEM((N_HEADS, 2 * nq, LANES), f32), pltpu.VMEM((N_HEADS, 2 * nq, LANES), f32),
                        pltpu.VMEM((N_HEADS, 2 * nq, V_DIM), f32)],
        compiler_params=_cparams(("parallel", "arbitrary")),
        name="sample_attention",
    )(lam, q_s, kn_s, v_s, cache_k, cache_v, sbc, sbn, subln_g.reshape(1, V_DIM))


def _chain_carries(c0, mult, add):
    rows = [c0]
    c = c0
    for s in range(NSUB):
        c = mult[s:s + 1] * c + add[s:s + 1]
        if s + 1 < NSUB:
            rows.append(c)
    return jnp.concatenate(rows, axis=0), c


def _rglru_body(xrg_ref, xgate_ref, head_s_ref, h0_s_ref, cw_ref, cb_ref, wa_ref, wx_ref, ba_ref, bx_ref,
                lam_ref, hg_ref, htail_ref, ctail_ref, xbuf_ref, a_ref, b_ref, hcar_ref, ptail_ref,
                *, blocks_per_seq, n_prompt_blocks):
    g = pl.program_id(0)
    is_sample = g >= n_prompt_blocks
    hl = (CONV_W - 1) * NSUB

    @pl.when(g % blocks_per_seq == 0)
    def _():
        hcar_ref[...] = jnp.zeros(hcar_ref.shape, f32)
        ptail_ref[...] = jnp.zeros(ptail_ref.shape, f32)

    x = xrg_ref[...]
    tail = x[TB - hl:]
    sub = lax.broadcasted_iota(jnp.int32, (NSUB, RG_WIDTH), 0)
    for k in range(CONV_W - 1):
        rs = slice(k * NSUB, (k + 1) * NSUB)
        prompt_head = jnp.where(sub == 0, pltpu.roll(ptail_ref[rs, :], 1, 0), pltpu.roll(tail[rs], 1, 0))
        xbuf_ref[rs, :] = jnp.where(is_sample, head_s_ref[rs, :], prompt_head)
    xbuf_ref[hl:, :] = x
    ptail_ref[...] = tail
    ctail_ref[...] = tail

    xc = cb_ref[...] + cw_ref[CONV_W - 1:CONV_W, :] * x
    for j in range(1, CONV_W):
        xc = xc + cw_ref[CONV_W - 1 - j:CONV_W - j, :] * xbuf_ref[hl - j * NSUB:hl - j * NSUB + TB, :]

    xcb = xc.astype(bf16)
    r_parts, i_parts = [], []
    for blk in range(RG_BLOCKS):
        sl = slice(blk * RG_BLOCK, (blk + 1) * RG_BLOCK)
        r_parts.append(jnp.dot(xcb[:, sl], wa_ref[blk], preferred_element_type=f32))
        i_parts.append(jnp.dot(xcb[:, sl], wx_ref[blk], preferred_element_type=f32))
    r = jax.nn.sigmoid(jnp.concatenate(r_parts, axis=1) + ba_ref[...])
    ig = jax.nn.sigmoid(jnp.concatenate(i_parts, axis=1) + bx_ref[...])
    nl = -lam_ref[...]
    softplus = jnp.maximum(nl, 0.0) + jnp.log1p(jnp.exp(-jnp.abs(nl)))
    a = jnp.exp(-RG_C * r * softplus)
    a_ref[...] = a
    b_ref[...] = jnp.sqrt(1.0 - a * a) * (ig * xc)

    def rows(i):
        return pl.ds(pl.multiple_of(i * NSUB, NSUB), NSUB)

    def local(i, carry):
        p, e = carry
        ai = a_ref[rows(i), :]
        return p * ai, ai * e + b_ref[rows(i), :]

    p, e = lax.fori_loop(0, SUBLEN, local,
                         (jnp.ones((NSUB, RG_WIDTH), f32), jnp.zeros((NSUB, RG_WIDTH), f32)), unroll=4)
    entry, exit_state = _chain_carries(hcar_ref[...], p, e)
    hcar_ref[...] = exit_state
    h_init = jnp.where(is_sample, h0_s_ref[...], entry)

    def step(i, hprev):
        hcur = a_ref[rows(i), :] * hprev + b_ref[rows(i), :]
        b_ref[rows(i), :] = hcur
        return hcur

    h_last = lax.fori_loop(0, SUBLEN, step, h_init, unroll=4)
    htail_ref[...] = h_last
    hg_ref[...] = (b_ref[...] * _gelu(xgate_ref[...])).astype(hg_ref.dtype)


def rglru_mixer(proj, head_s, h0_s, conv_w, conv_b, wa, wx, ba, bx, lam, *, blocks_per_seq, n_prompt_blocks):
    m = proj.shape[0]
    nblk = m // TB
    hl = (CONV_W - 1) * NSUB
    row = lambda v: v.reshape(1, RG_WIDTH)
    full = lambda shape: pl.BlockSpec(shape, lambda g: (0,) * len(shape))
    return pl.pallas_call(
        functools.partial(_rglru_body, blocks_per_seq=blocks_per_seq, n_prompt_blocks=n_prompt_blocks),
        grid=(nblk,),
        in_specs=[
            pl.BlockSpec((TB, RG_WIDTH), lambda g: (g, COL_RG)),
            pl.BlockSpec((TB, RG_WIDTH), lambda g: (g, COL_GATE)),
            full((hl, RG_WIDTH)),
            full((NSUB, RG_WIDTH)),
            full((CONV_W, RG_WIDTH)),
            full((1, RG_WIDTH)),
            full((RG_BLOCKS, RG_BLOCK, RG_BLOCK)),
            full((RG_BLOCKS, RG_BLOCK, RG_BLOCK)),
            full((1, RG_WIDTH)),
            full((1, RG_WIDTH)),
            full((1, RG_WIDTH)),
        ],
        out_specs=[
            pl.BlockSpec((TB, RG_WIDTH), lambda g: (g, 0)),
            pl.BlockSpec((None, NSUB, RG_WIDTH), lambda g: (g, 0, 0)),
            pl.BlockSpec((None, hl, RG_WIDTH), lambda g: (g, 0, 0)),
        ],
        out_shape=[
            jax.ShapeDtypeStruct((m, RG_WIDTH), bf16),
            jax.ShapeDtypeStruct((nblk, NSUB, RG_WIDTH), f32),
            jax.ShapeDtypeStruct((nblk, hl, RG_WIDTH), f32),
        ],
        scratch_shapes=[
            pltpu.VMEM((hl + TB, RG_WIDTH), f32),
            pltpu.VMEM((TB, RG_WIDTH), f32),
            pltpu.VMEM((TB, RG_WIDTH), f32),
            pltpu.VMEM((1, RG_WIDTH), f32),
            pltpu.VMEM((hl, RG_WIDTH), f32),
        ],
        compiler_params=_cparams(("arbitrary",)),
        name="rglru_mixer",
    )(proj, proj, head_s, h0_s, conv_w, row(conv_b), wa.astype(bf16), wx.astype(bf16), row(ba), row(bx),
      row(lam))


S5_CHUNKS = 8
S5_CH_IN = S5_WIDTH // S5_CHUNKS
S5_CH_ST = S5_LANES // S5_CHUNKS
S5_SCAN_LANES = 512


def _cmul(ar, ai, br, bi):
    return ar * br - ai * bi, ar * bi + ai * br


def _s5_body(u_ref, s0r_ref, s0i_ref, abr_ref, abi_ref, bdr_ref, bdi_ref, cdr_ref, cdi_ref, d_ref, gw_ref,
             gb_ref, gs_ref, str_ref, sti_ref, xr_ref, xi_ref, car_ref, cai_ref,
             *, blocks_per_seq, n_prompt_blocks):
    g = pl.program_id(0)
    is_sample = g >= n_prompt_blocks

    @pl.when(g % blocks_per_seq == 0)
    def _():
        car_ref[...] = jnp.zeros(car_ref.shape, f32)
        cai_ref[...] = jnp.zeros(cai_ref.shape, f32)

    u = u_ref[...]
    ub = u.astype(bf16)
    for c in range(S5_CHUNKS):
        ci = slice(c * S5_CH_IN, (c + 1) * S5_CH_IN)
        cs = slice(c * S5_CH_ST, (c + 1) * S5_CH_ST)
        xr_ref[:, cs] = jnp.dot(ub[:, ci], bdr_ref[c], preferred_element_type=f32)
        xi_ref[:, cs] = jnp.dot(ub[:, ci], bdi_ref[c], preferred_element_type=f32)

    def rows(i):
        return pl.ds(pl.multiple_of(i * NSUB, NSUB), NSUB)

    for c in range(S5_LANES // S5_SCAN_LANES):
        ls = slice(c * S5_SCAN_LANES, (c + 1) * S5_SCAN_LANES)
        ar1 = abr_ref[:, ls]
        ai1 = abi_ref[:, ls]
        ar = jnp.broadcast_to(ar1, (NSUB, S5_SCAN_LANES))
        ai = jnp.broadcast_to(ai1, (NSUB, S5_SCAN_LANES))

        def local(i, carry, ls=ls, ar=ar, ai=ai):
            er, ei = carry
            nr, ni = _cmul(ar, ai, er, ei)
            return nr + xr_ref[rows(i), ls], ni + xi_ref[rows(i), ls]

        zero = jnp.zeros((NSUB, S5_SCAN_LANES), f32)
        er, ei = lax.fori_loop(0, SUBLEN, local, (zero, zero), unroll=4)
        pr, pi = ar1, ai1
        for _ in range(int(math.log2(SUBLEN))):
            pr, pi = _cmul(pr, pi, pr, pi)
        rows_r, rows_i = [car_ref[:, ls]], [cai_ref[:, ls]]
        cr, ci_ = rows_r[0], rows_i[0]
        for s in range(NSUB):
            mr, mi = _cmul(pr, pi, cr, ci_)
            cr, ci_ = mr + er[s:s + 1], mi + ei[s:s + 1]
            if s + 1 < NSUB:
                rows_r.append(cr)
                rows_i.append(ci_)
        car_ref[:, ls] = cr
        cai_ref[:, ls] = ci_
        init_r = jnp.where(is_sample, s0r_ref[:, ls], jnp.concatenate(rows_r, axis=0))
        init_i = jnp.where(is_sample, s0i_ref[:, ls], jnp.concatenate(rows_i, axis=0))

        def step(i, carry, ls=ls, ar=ar, ai=ai):
            sr, si = carry
            nr, ni = _cmul(ar, ai, sr, si)
            nr = nr + xr_ref[rows(i), ls]
            ni = ni + xi_ref[rows(i), ls]
            xr_ref[rows(i), ls] = nr
            xi_ref[rows(i), ls] = ni
            return nr, ni

        sr, si = lax.fori_loop(0, SUBLEN, step, (init_r, init_i), unroll=4)
        str_ref[:, ls] = sr
        sti_ref[:, ls] = si

    y_parts = []
    for c in range(S5_CHUNKS):
        cs = slice(c * S5_CH_ST, (c + 1) * S5_CH_ST)
        y_parts.append(jnp.dot(xr_ref[:, cs].astype(bf16), cdr_ref[c], preferred_element_type=f32)
                       + jnp.dot(xi_ref[:, cs].astype(bf16), cdi_ref[c], preferred_element_type=f32))
    y = jnp.concatenate(y_parts, axis=1) + d_ref[...] * u
    gy = _gelu(y)
    z = jnp.dot(gy.astype(bf16), gw_ref[...], preferred_element_type=f32) + gb_ref[...]
    gs_ref[...] = (gy * jax.nn.sigmoid(z)).astype(gs_ref.dtype)


def _s5_discretise(lam_re, lam_im, log_step, b_re, b_im, c_re, c_im):
    step = jnp.exp(log_step)[:, None]
    mag = jnp.exp(lam_re * step)
    ab_re, ab_im = mag * jnp.cos(lam_im * step), mag * jnp.sin(lam_im * step)
    den = lam_re * lam_re + lam_im * lam_im
    nr, ni = ab_re - 1.0, ab_im
    co_re = (nr * lam_re + ni * lam_im) / den
    co_im = (ni * lam_re - nr * lam_im) / den
    bb_re = co_re[..., None] * b_re - co_im[..., None] * b_im
    bb_im = co_re[..., None] * b_im + co_im[..., None] * b_re
    gpc = S5_GROUPS // S5_CHUNKS
    eye = jnp.eye(gpc, dtype=f32)

    def in_chunks(bb):
        t = bb.reshape(S5_CHUNKS, gpc, S5_STATE, S5_GROUP)
        return jnp.einsum('cgph,gk->cghkp', t, eye).reshape(S5_CHUNKS, S5_CH_IN, S5_CH_ST).astype(bf16)

    def out_chunks(cc):
        t = cc.reshape(S5_CHUNKS, gpc, S5_GROUP, S5_STATE)
        return jnp.einsum('cghp,gk->cgpkh', t, eye).reshape(S5_CHUNKS, S5_CH_ST, S5_CH_IN).astype(bf16)

    return (ab_re.reshape(1, S5_LANES), ab_im.reshape(1, S5_LANES), in_chunks(bb_re), in_chunks(bb_im),
            out_chunks(c_re), out_chunks(-c_im))


def s5_mixer(proj, s0_re, s0_im, disc, d, glu_w, glu_b, *, layer, blocks_per_seq, n_prompt_blocks):
    m = proj.shape[0]
    nblk = m // TB
    ab_re, ab_im, bd_re, bd_im, cd_re, cd_im = disc
    full = lambda shape: pl.BlockSpec(shape, lambda g: (0,) * len(shape))
    return pl.pallas_call(
        functools.partial(_s5_body, blocks_per_seq=blocks_per_seq, n_prompt_blocks=n_prompt_blocks),
        grid=(nblk,),
        in_specs=[
            pl.BlockSpec((TB, S5_WIDTH), lambda g: (g, COL_S5)),
            full((NSUB, S5_LANES)),
            full((NSUB, S5_LANES)),
            full((1, S5_LANES)),
            full((1, S5_LANES)),
            full((S5_CHUNKS, S5_CH_IN, S5_CH_ST)),
            full((S5_CHUNKS, S5_CH_IN, S5_CH_ST)),
            full((S5_CHUNKS, S5_CH_ST, S5_CH_IN)),
            full((S5_CHUNKS, S5_CH_ST, S5_CH_IN)),
            full((1, S5_WIDTH)),
            pl.BlockSpec((None, S5_WIDTH, S5_WIDTH), lambda g: (layer, 0, 0)),
            full((1, S5_WIDTH)),
        ],
        out_specs=[
            pl.BlockSpec((TB, S5_WIDTH), lambda g: (g, 0)),
            pl.BlockSpec((None, NSUB, S5_LANES), lambda g: (g, 0, 0)),
            pl.BlockSpec((None, NSUB, S5_LANES), lambda g: (g, 0, 0)),
        ],
        out_shape=[
            jax.ShapeDtypeStruct((m, S5_WIDTH), bf16),
            jax.ShapeDtypeStruct((nblk, NSUB, S5_LANES), f32),
            jax.ShapeDtypeStruct((nblk, NSUB, S5_LANES), f32),
        ],
        scratch_shapes=[
            pltpu.VMEM((TB, S5_LANES), f32),
            pltpu.VMEM((TB, S5_LANES), f32),
            pltpu.VMEM((1, S5_LANES), f32),
            pltpu.VMEM((1, S5_LANES), f32),
        ],
        compiler_params=_cparams(("arbitrary",)),
        name="s5_mixer",
    )(proj, s0_re, s0_im, ab_re, ab_im, bd_re, bd_im, cd_re, cd_im, d.reshape(1, S5_WIDTH),
      glu_w, glu_b.reshape(1, S5_WIDTH))


def _merge_body(att_ref, rg_ref, s5_ref, wa_ref, wr_ref, ws_ref, g0_ref, g1_ref, g2_ref, b0_ref, b1_ref,
                b2_ref, o_ref):
    acc = jax.nn.sigmoid(g0_ref[...] + b0_ref[...]) * jnp.dot(att_ref[...], wa_ref[...],
                                                               preferred_element_type=f32)
    acc = acc + jax.nn.sigmoid(g1_ref[...] + b1_ref[...]) * jnp.dot(rg_ref[...], wr_ref[...],
                                                                     preferred_element_type=f32)
    acc = acc + jax.nn.sigmoid(g2_ref[...] + b2_ref[...]) * jnp.dot(s5_ref[...], ws_ref[...],
                                                                     preferred_element_type=f32)
    o_ref[...] = acc.astype(o_ref.dtype)


def gated_merge(o_att, hg, gs, w_att_o, w_rg_o, w_s5_o, proj, b_gate, *, layer, tm, tn):
    m = o_att.shape[0]
    k = o_att.shape[1]
    gcol = COL_GATES // tn
    bcol = D_MODEL // tn
    act = pl.BlockSpec((tm, k), lambda i, j: (i, 0))
    wsp = pl.BlockSpec((None, k, tn), lambda i, j: (layer, 0, j))
    gate = lambda br: pl.BlockSpec((tm, tn), lambda i, j: (i, gcol + br * bcol + j))
    bias = lambda br: pl.BlockSpec((1, tn), lambda i, j: (0, br * bcol + j))
    bg = b_gate.reshape(1, N_BRANCH * D_MODEL)
    return pl.pallas_call(
        _merge_body,
        grid=(m // tm, D_MODEL // tn),
        in_specs=[act, act, act, wsp, wsp, wsp, gate(0), gate(1), gate(2), bias(0), bias(1), bias(2)],
        out_specs=pl.BlockSpec((tm, tn), lambda i, j: (i, j)),
        out_shape=jax.ShapeDtypeStruct((m, D_MODEL), bf16),
        compiler_params=_cparams(("parallel", "arbitrary")),
        name="gated_merge",
    )(o_att, hg, gs, w_att_o, w_rg_o, w_s5_o, proj, proj, proj, bg, bg, bg)


def _residual_matmul_body(x_ref, a_ref, w_ref, o_ref):
    o_ref[...] = x_ref[...] + jnp.dot(a_ref[...], w_ref[...], preferred_element_type=f32)


def residual_matmul(x, a, w, *, layer, tm, tn):
    m, n = x.shape
    k = a.shape[1]
    return pl.pallas_call(
        _residual_matmul_body,
        grid=(m // tm, n // tn),
        in_specs=[
            pl.BlockSpec((tm, tn), lambda i, j: (i, j)),
            pl.BlockSpec((tm, k), lambda i, j: (i, 0)),
            pl.BlockSpec((None, k, tn), lambda i, j: (layer, 0, j)),
        ],
        out_specs=pl.BlockSpec((tm, tn), lambda i, j: (i, j)),
        out_shape=jax.ShapeDtypeStruct((m, n), f32),
        compiler_params=_cparams(("parallel", "arbitrary")),
        name="residual_matmul",
    )(x, a, w)


PEER_TT = 256
PEER_ET = 1024
PEER_I1 = PEER_ET // N_KEYS
PEER_PIECE = 32
PEER_RG = 2


def _top_values(s, k):
    out = []
    for _ in range(k):
        mx = jnp.max(s, axis=0, keepdims=True)
        out.append(mx)
        s = jnp.where(s == mx, -jnp.inf, s)
    return out


def _peer_route_body(q_ref, sk_ref, s2_ref, e2_ref, thr_ref, c1_ref):
    for h in range(PEER_HEADS):
        s1 = _dot_nt(sk_ref[0], q_ref[:, (2 * h) * N_KEYS:(2 * h + 1) * N_KEYS])
        s2 = _dot_nt(sk_ref[1], q_ref[:, (2 * h + 1) * N_KEYS:(2 * h + 2) * N_KEYS])
        v1 = _top_values(s1, PEER_TOPK)
        v2 = _top_values(s2, PEER_TOPK)
        cand = [v1[k1] + v2[k2] for k1 in range(PEER_TOPK) for k2 in range(PEER_TOPK // (k1 + 1))]
        pad = (-len(cand)) % SUBLANES
        cand = jnp.concatenate(cand + [jnp.full_like(cand[0], -jnp.inf)] * pad, axis=0)
        top = _top_values(cand, PEER_TOPK)
        theta = top[-1]
        z = top[0] * 0.0
        for t in top:
            z = z + jnp.exp(t - top[0])
        s2_ref[h] = s2
        e2_ref[h] = jnp.exp(s2 - v2[0])
        thr_ref[h] = theta - s1
        c1_ref[h] = jnp.exp(s1 - v1[0]) / z


def peer_route(q, subkeys):
    m = q.shape[0]
    shp = jax.ShapeDtypeStruct((PEER_HEADS, N_KEYS, m), f32)
    spec = pl.BlockSpec((PEER_HEADS, N_KEYS, PEER_TT), lambda i: (0, 0, i))
    return pl.pallas_call(
        _peer_route_body,
        grid=(m // PEER_TT,),
        in_specs=[
            pl.BlockSpec((PEER_TT, q.shape[1]), lambda i: (i, 0)),
            pl.BlockSpec((2, N_KEYS, N_KEYS), lambda i: (0, 0, 0)),
        ],
        out_specs=[spec] * 4,
        out_shape=[shp] * 4,
        compiler_params=_cparams(("parallel",)),
        name="peer_route",
    )(q, subkeys)


def _peer_expert_body(xnt_ref, u_ref, vt_ref, s2_ref, e2_ref, thr_ref, c1_ref, o_ref, act_ref, zt_ref):
    j = pl.program_id(1)

    @pl.when(j == 0)
    def _():
        o_ref[...] = jnp.zeros(o_ref.shape, f32)

    zt_ref[...] = jnp.dot(u_ref[...], xnt_ref[...], preferred_element_type=f32)
    n_pc = N_KEYS // PEER_PIECE
    for tc in range(zt_ref.shape[1] // LANES):
        ts = slice(tc * LANES, (tc + 1) * LANES)
        for r0 in range(0, PEER_I1, PEER_RG):
            w = [[jnp.zeros((PEER_PIECE, LANES), f32) for _ in range(n_pc)] for _ in range(PEER_RG)]
            for h in range(PEER_HEADS):
                thr = [thr_ref[h, r0 + r:r0 + r + 1, ts] for r in range(PEER_RG)]
                c1 = [c1_ref[h, r0 + r:r0 + r + 1, ts] for r in range(PEER_RG)]
                for pc in range(n_pc):
                    ks = slice(pc * PEER_PIECE, (pc + 1) * PEER_PIECE)
                    s2 = s2_ref[h, ks, ts]
                    e2 = e2_ref[h, ks, ts]
                    for r in range(PEER_RG):
                        w[r][pc] = w[r][pc] + jnp.where(s2 >= thr[r], e2 * c1[r], 0.0)
            for r in range(PEER_RG):
                for pc in range(n_pc):
                    base = (r0 + r) * N_KEYS + pc * PEER_PIECE
                    rs = slice(base, base + PEER_PIECE)
                    act_ref[rs, ts] = (_gelu(zt_ref[rs, ts]) * w[r][pc]).astype(bf16)
    o_ref[...] += jnp.dot(vt_ref[...], act_ref[...], preferred_element_type=f32)


def peer_experts(xnt, u_tab, vt_tab, s2, e2, thr, c1, *, layer, tt):
    d, m = xnt.shape
    assert PEER_I1 == SUBLANES
    thr = thr.reshape(PEER_HEADS, N_KEYS // PEER_I1, PEER_I1, m)
    c1 = c1.reshape(PEER_HEADS, N_KEYS // PEER_I1, PEER_I1, m)
    once = pl.Buffered(1)
    tok = pl.BlockSpec((PEER_HEADS, N_KEYS, tt), lambda i, j: (0, 0, i), pipeline_mode=once)
    row = pl.BlockSpec((PEER_HEADS, None, PEER_I1, tt), lambda i, j: (0, j, 0, i))
    return pl.pallas_call(
        _peer_expert_body,
        grid=(m // tt, N_EXPERTS // PEER_ET),
        in_specs=[
            pl.BlockSpec((d, tt), lambda i, j: (0, i), pipeline_mode=once),
            pl.BlockSpec((None, PEER_ET, d), lambda i, j: (layer, j, 0)),
            pl.BlockSpec((None, d, PEER_ET), lambda i, j: (layer, 0, j)),
            tok, tok, row, row,
        ],
        out_specs=pl.BlockSpec((d, tt), lambda i, j: (0, i)),
        out_shape=jax.ShapeDtypeStruct((d, m), f32),
        scratch_shapes=[pltpu.VMEM((PEER_ET, tt), bf16), pltpu.VMEM((PEER_ET, tt), f32)],
        compiler_params=_cparams(("parallel", "arbitrary")),
        name="peer_experts",
    )(xnt, u_tab, vt_tab, s2, e2, thr, c1)


def _add_transposed_body(x_ref, dt_ref, o_ref):
    o_ref[...] = x_ref[...] + dt_ref[...].T


def add_transposed(x, dt, *, tm):
    m, d = x.shape
    return pl.pallas_call(
        _add_transposed_body,
        grid=(m // tm,),
        in_specs=[pl.BlockSpec((tm, d), lambda i: (i, 0)), pl.BlockSpec((d, tm), lambda i: (0, i))],
        out_specs=pl.BlockSpec((tm, d), lambda i: (i, 0)),
        out_shape=jax.ShapeDtypeStruct((m, d), f32),
        compiler_params=_cparams(("parallel",)),
        name="add_transposed",
    )(x, dt)


def _to_blocks(x):
    b, t, c = x.shape
    return x.reshape(b, t // TB, NSUB, SUBLEN, c).transpose(0, 1, 3, 2, 4).reshape(b * t, c)


def _from_blocks(y, b):
    m, c = y.shape
    t = m // b
    return y.reshape(b, t // TB, SUBLEN, NSUB, c).transpose(0, 1, 3, 2, 4).reshape(b, t, c)


def _layer(x, l, p, wb, tiles, dims, state):
    n_seq, blocks_per_seq, n_stream = dims
    n_prompt_blocks = n_seq * blocks_per_seq
    mp = n_prompt_blocks * TB
    pbias, sbc, sbn = tiles
    cache_k, cache_v, rg_h0, rg_conv0, s5_re0, s5_im0 = state
    tm = 3 * TB if x.shape[0] % (3 * TB) == 0 else TB
    tm_big = 1408 if x.shape[0] % 1408 == 0 else tm
    lam_init = 0.8 - 0.6 * math.exp(-0.3 * l)
    lam = (jnp.exp(jnp.sum(p['lam_q1'] * p['lam_k1'])) - jnp.exp(jnp.sum(p['lam_q2'] * p['lam_k2']))
           + lam_init).reshape(1)

    proj = norm_matmul(x, p['norm1_g'], wb['w_in'], layer=l, tm=tm_big, tn=1024, out_dtype=f32)
    qn, kn, kb, vb = qk_norm(proj, p['qn_g'], p['kn_g'], tm=tm)

    o_att = prompt_attention(qn, kb, vb, pbias, lam, p['subln_g'], n_seq=n_seq,
                             blocks_per_seq=blocks_per_seq, lam_init=lam_init)
    stream = lambda a: a[mp:].reshape(SUBLEN, n_stream, -1).transpose(1, 0, 2)
    v_all = proj[:, COL_V * 1024:(COL_V + 1) * 1024]
    rows = lambda c: c.reshape(c.shape[0], c.shape[1], -1, V_DIM)
    o_s = sample_attention(stream(qn), stream(kn), stream(v_all), rows(cache_k), rows(cache_v), sbc, sbn, lam,
                           p['subln_g'], layer=l, lam_init=lam_init)
    o_att = jnp.concatenate([o_att, o_s.transpose(1, 0, 2).reshape(TB, ATT_WIDTH)], axis=0)

    head_s = rg_conv0.transpose(1, 0, 2).reshape((CONV_W - 1) * NSUB, RG_WIDTH)
    hg, h_tail, c_tail = rglru_mixer(proj, head_s, rg_h0, p['rg_conv_w'], p['rg_conv_b'], p['rg_wa'],
                                     p['rg_wx'], p['rg_ba'], p['rg_bx'], p['rg_lambda'],
                                     blocks_per_seq=blocks_per_seq, n_prompt_blocks=n_prompt_blocks)
    disc = _s5_discretise(p['s5_lam_re'], p['s5_lam_im'], p['s5_log_step'], p['s5_b_re'], p['s5_b_im'],
                          p['s5_c_re'], p['s5_c_im'])
    gs, st_re, st_im = s5_mixer(proj, s5_re0.reshape(NSUB, S5_LANES), s5_im0.reshape(NSUB, S5_LANES), disc,
                                p['s5_d'], wb['s5_glu_w'], p['s5_glu_b'], layer=l,
                                blocks_per_seq=blocks_per_seq, n_prompt_blocks=n_prompt_blocks)

    merged = gated_merge(o_att, hg, gs, wb['w_att_o'], wb['w_rg_o'], wb['w_s5_o'], proj, p['b_gate'],
                         layer=l, tm=tm_big, tn=512)
    x = residual_matmul(x, merged, wb['w_out'], layer=l, tm=tm_big, tn=1024)

    q, xn2 = norm_matmul(x, p['norm2_g'], wb['peer_wq'], layer=l, tm=tm, tn=1024, out_dtype=bf16,
                         emit_xn=True)
    s2, e2, thr, c1 = peer_route(q, p['peer_subkeys'].astype(bf16))
    peer_t = peer_experts(xn2.T, wb['peer_u'], wb['peer_vt'], s2, e2, thr, c1, layer=l, tt=tm)
    x = add_transposed(x, peer_t, tm=TB)

    last = jnp.arange(n_seq) * blocks_per_seq + blocks_per_seq - 1
    hl = CONV_W - 1
    outs_p = (
        _from_blocks(kn[:mp], n_seq).reshape(n_seq, -1, N_HEADS, 2 * QK_DIM),
        _from_blocks(v_all[:mp], n_seq).reshape(n_seq, -1, N_HEADS, V_DIM),
        h_tail[last, NSUB - 1],
        c_tail[last].reshape(n_seq, hl, NSUB, RG_WIDTH)[:, :, NSUB - 1],
        st_re[last, NSUB - 1].reshape(n_seq, S5_GROUPS, S5_STATE),
        st_im[last, NSUB - 1].reshape(n_seq, S5_GROUPS, S5_STATE),
    )
    outs_s = (
        stream(kn).reshape(n_stream, SUBLEN, N_HEADS, 2 * QK_DIM),
        stream(v_all).reshape(n_stream, SUBLEN, N_HEADS, V_DIM),
        h_tail[n_prompt_blocks],
        c_tail[n_prompt_blocks].reshape(hl, NSUB, RG_WIDTH).transpose(1, 0, 2),
        st_re[n_prompt_blocks].reshape(n_stream, S5_GROUPS, S5_STATE),
        st_im[n_prompt_blocks].reshape(n_stream, S5_GROUPS, S5_STATE),
    )
    return x, outs_p, outs_s


def kernel(x_prompt, x_sample, cache_k, cache_v, state_rg_h, state_rg_conv, state_s5_re, state_s5_im, rel_bias, norm1_g, w_in, b_gate, qn_g, kn_g, lam_q1, lam_k1, lam_q2, lam_k2, subln_g, w_att_o, rg_conv_w, rg_conv_b, rg_wa, rg_ba, rg_wx, rg_bx, rg_lambda, w_rg_o, s5_lam_re, s5_lam_im, s5_log_step, s5_b_re, s5_b_im, s5_c_re, s5_c_im, s5_d, s5_glu_w, s5_glu_b, w_s5_o, w_out, norm2_g, peer_wq, peer_subkeys, peer_u, peer_v):
    n_seq, t_prompt, _ = x_prompt.shape
    n_stream, t_sample, _ = x_sample.shape
    depth = w_in.shape[0]
    assert n_stream == NSUB and t_sample == SUBLEN and t_prompt % TB == 0
    blocks_per_seq = t_prompt // TB
    dims = (n_seq, blocks_per_seq, n_stream)
    past = cache_k.shape[2]

    small = dict(norm1_g=norm1_g, b_gate=b_gate, qn_g=qn_g, kn_g=kn_g, lam_q1=lam_q1,
                 lam_k1=lam_k1, lam_q2=lam_q2, lam_k2=lam_k2, subln_g=subln_g,
                 rg_conv_w=rg_conv_w, rg_conv_b=rg_conv_b, rg_wa=rg_wa, rg_ba=rg_ba, rg_wx=rg_wx,
                 rg_bx=rg_bx, rg_lambda=rg_lambda, s5_lam_re=s5_lam_re,
                 s5_lam_im=s5_lam_im, s5_log_step=s5_log_step, s5_b_re=s5_b_re, s5_b_im=s5_b_im,
                 s5_c_re=s5_c_re, s5_c_im=s5_c_im, s5_d=s5_d, s5_glu_b=s5_glu_b,
                 norm2_g=norm2_g, peer_subkeys=peer_subkeys)
    wb = dict(w_in=w_in.astype(bf16), w_att_o=w_att_o.astype(bf16), w_rg_o=w_rg_o.astype(bf16),
              w_s5_o=w_s5_o.astype(bf16), w_out=w_out.astype(bf16), s5_glu_w=s5_glu_w.astype(bf16),
              peer_wq=peer_wq.astype(bf16), peer_u=peer_u.astype(bf16),
              peer_vt=jnp.swapaxes(peer_v, 1, 2).astype(bf16))

    tiles = bias_tiles(rel_bias, past)
    x = jnp.concatenate([_to_blocks(x_prompt), x_sample.transpose(1, 0, 2).reshape(TB, D_MODEL)], axis=0)
    outs_p, outs_s = [], []
    for l in range(depth):
        p = {k: v[l] for k, v in small.items()}
        state = (cache_k, cache_v, state_rg_h[l], state_rg_conv[l], state_s5_re[l], state_s5_im[l])
        x, op, os_ = _layer(x, l, p, wb, tiles, dims, state)
        outs_p.append(op)
        outs_s.append(os_)

    mp = n_seq * t_prompt
    y_prompt = _from_blocks(x[:mp], n_seq)
    y_sample = x[mp:].reshape(SUBLEN, n_stream, D_MODEL).transpose(1, 0, 2)
    stack = lambda outs, i: jnp.stack([o[i] for o in outs])
    return (y_prompt, y_sample) + tuple(stack(outs_p, i) for i in range(6)) + tuple(stack(outs_s, i)
                                                                                    for i in range(6))
```

```python
import functools
import math

import jax
import jax.numpy as jnp
from jax import lax
from jax.experimental import pallas as pl
from jax.experimental.pallas import tpu as pltpu

f32 = jnp.float32
bf16 = jnp.bfloat16

D_MODEL = 2048
CHUNK = 64
EPS = 1e-6
N_HEADS = 8
QK_DIM = 64
V_DIM = 128
Q_WIDTH = 1024
ATT_WIDTH = 1024
N_BUCKETS = 32
MAX_DISTANCE = 128
RG_WIDTH = 1024
RG_BLOCKS = 8
RG_BLOCK = 128
CONV_W = 4
RG_C = 8.0
S5_WIDTH = 1024
S5_GROUP = 16
S5_GROUPS = 64
S5_STATE = 64
S5_LANES = S5_GROUPS * S5_STATE
PEER_HEADS = 8
N_KEYS = 128
N_EXPERTS = N_KEYS * N_KEYS
PEER_TOPK = 16
N_BRANCH = 3
IN_COLS = 2 * Q_WIDTH + ATT_WIDTH + 2 * RG_WIDTH + S5_WIDTH + N_BRANCH * D_MODEL

SUBLANES = 8
LANES = 128
VMEM_LIMIT = 56 * 1024 * 1024

NSUB = SUBLANES
SUBLEN = 32
TB = NSUB * SUBLEN
MASK_NEG = -1e30

COL_Q, COL_K, COL_V, COL_RG, COL_GATE, COL_S5 = 0, 1, 2, 3, 4, 5
COL_GATES = 6 * 1024


def _cparams(sem):
    return pltpu.CompilerParams(dimension_semantics=sem, vmem_limit_bytes=VMEM_LIMIT)


def _gelu(x):
    return 0.5 * x * (1.0 + lax.erf(x * (2.0 ** -0.5)))


def _dot_nt(a, b):
    return lax.dot_general(a, b, (((1,), (1,)), ((), ())), preferred_element_type=f32)


def _dot_tn(a, b):
    return lax.dot_general(a, b, (((0,), (0,)), ((), ())), preferred_element_type=f32)


def _norm_matmul_body(x_ref, g_ref, w_ref, o_ref, *rest, emit_xn):
    xn_ref = rest[-1]

    @pl.when(pl.program_id(1) == 0)
    def _():
        x = x_ref[...]
        ms = jnp.mean(x * x, axis=-1, keepdims=True)
        xn = (x * lax.rsqrt(ms + EPS) * g_ref[...]).astype(bf16)
        xn_ref[...] = xn
        if emit_xn:
            rest[0][...] = xn

    o_ref[...] = jnp.dot(xn_ref[...], w_ref[...], preferred_element_type=f32).astype(o_ref.dtype)


def norm_matmul(x, g, w, *, layer, tm, tn, out_dtype, emit_xn=False):
    m, d = x.shape
    n = w.shape[2]
    out_shape = [jax.ShapeDtypeStruct((m, n), out_dtype)]
    out_specs = [pl.BlockSpec((tm, tn), lambda i, j: (i, j))]
    if emit_xn:
        out_shape.append(jax.ShapeDtypeStruct((m, d), bf16))
        out_specs.append(pl.BlockSpec((tm, d), lambda i, j: (i, 0)))
    res = pl.pallas_call(
        functools.partial(_norm_matmul_body, emit_xn=emit_xn),
        grid=(m // tm, n // tn),
        in_specs=[
            pl.BlockSpec((tm, d), lambda i, j: (i, 0)),
            pl.BlockSpec((1, d), lambda i, j: (0, 0)),
            pl.BlockSpec((None, d, tn), lambda i, j: (layer, 0, j)),
        ],
        out_specs=out_specs,
        out_shape=out_shape,
        scratch_shapes=[pltpu.VMEM((tm, d), bf16)],
        compiler_params=_cparams(("parallel", "arbitrary")),
        name="norm_matmul",
    )(x, g.reshape(1, d), w)
    return res if emit_xn else res[0]


def _group_mean_sq(x, ones_bd):
    sq = x * x
    hi = sq.astype(bf16)
    lo = (sq - hi.astype(f32)).astype(bf16)
    s = jnp.dot(hi, ones_bd, preferred_element_type=f32) + jnp.dot(lo, ones_bd, preferred_element_type=f32)
    return s * (1.0 / QK_DIM)


def _qknorm_body(q_ref, k_ref, v_ref, qg_ref, kg_ref, qo_ref, kb_ref, vb_ref, kp_ref, ks_ref, vp_ref, vs_ref,
                 kn_ref, *, n_prompt_blocks):
    r = lax.broadcasted_iota(jnp.int32, (LANES, LANES), 0) // QK_DIM
    c = lax.broadcasted_iota(jnp.int32, (LANES, LANES), 1) // QK_DIM
    ones_bd = (r == c).astype(bf16)
    for t in range(Q_WIDTH // LANES):
        sl = slice(t * LANES, (t + 1) * LANES)
        q = q_ref[:, sl]
        k = k_ref[:, sl]
        qn = q * lax.rsqrt(_group_mean_sq(q, ones_bd) + EPS) * qg_ref[:, sl]
        kn = k * lax.rsqrt(_group_mean_sq(k, ones_bd) + EPS) * kg_ref[:, sl]
        qo_ref[:, sl] = (qn * (QK_DIM ** -0.5)).astype(bf16)
        kn_ref[:, sl] = kn
        kb_ref[:, sl] = kn.astype(bf16)
    vb_ref[...] = v_ref[...].astype(bf16)

    is_sample = pl.program_id(0) >= n_prompt_blocks

    @pl.when(jnp.logical_not(is_sample))
    def _():
        kp_ref[...] = kn_ref[...]
        vp_ref[...] = v_ref[...]

    @pl.when(is_sample)
    def _():
        ks_ref[...] = kn_ref[...]
        vs_ref[...] = v_ref[...]


def qk_norm(proj, qn_g, kn_g, *, n_prompt_blocks):
    m = proj.shape[0]
    mp = n_prompt_blocks * TB
    assert m == mp + TB
    qg = jnp.tile(qn_g, Q_WIDTH // QK_DIM).reshape(1, Q_WIDTH)
    kg = jnp.tile(kn_g, Q_WIDTH // QK_DIM).reshape(1, Q_WIDTH)
    col = lambda c: pl.BlockSpec((TB, Q_WIDTH), lambda i: (i, c))
    gain = pl.BlockSpec((1, Q_WIDTH), lambda i: (0, 0))
    prompt = pl.BlockSpec((TB, Q_WIDTH), lambda i: (jnp.minimum(i, n_prompt_blocks - 1), 0))
    sample = pl.BlockSpec((TB, Q_WIDTH), lambda i: (0, 0))
    half = lambda rows, dt: jax.ShapeDtypeStruct((rows, Q_WIDTH), dt)
    return pl.pallas_call(
        functools.partial(_qknorm_body, n_prompt_blocks=n_prompt_blocks),
        grid=(m // TB,),
        in_specs=[col(COL_Q), col(COL_K), col(COL_V), gain, gain],
        out_specs=[col(0), col(0), col(0), prompt, sample, prompt, sample],
        out_shape=[half(m, bf16), half(m, bf16), half(m, bf16), half(mp, f32), half(TB, f32), half(mp, f32),
                   half(TB, f32)],
        scratch_shapes=[pltpu.VMEM((TB, Q_WIDTH), f32)],
        compiler_params=_cparams(("arbitrary",)),
        name="qk_norm",
    )(proj, proj, proj, qg, kg)


def _bucket(rel):
    half = N_BUCKETS // 2
    max_exact = half // 2
    ret = jnp.where(rel > 0, half, 0)
    n = jnp.abs(rel)
    nf = jnp.maximum(n, 1).astype(f32)
    large = max_exact + (jnp.log(nf / max_exact) / math.log(MAX_DISTANCE / max_exact)
                         * (half - max_exact)).astype(jnp.int32)
    large = jnp.minimum(large, half - 1)
    return ret + jnp.where(n < max_exact, n, large)


def _bias_from_bucket(bucket, rb_ref, h):
    out = jnp.zeros(bucket.shape, f32)
    for j in range(N_BUCKETS):
        out = jnp.where(bucket == j, rb_ref[j, h], out)
    return out


def _block_time(r):
    return (r % NSUB) * SUBLEN + r // NSUB


def _bias_body(rb_ref, pb_ref, sbc_ref, sbn_ref, *, past):
    h = pl.program_id(0)
    tq = _block_time(lax.broadcasted_iota(jnp.int32, (TB, TB), 0))
    tk = _block_time(lax.broadcasted_iota(jnp.int32, (TB, TB), 1))
    b0 = _bias_from_bucket(_bucket(tk - tq), rb_ref, h)
    pb_ref[0] = jnp.where(tk // CHUNK <= tq // CHUNK, b0, MASK_NEG)
    pb_ref[1] = _bias_from_bucket(_bucket(tk - tq - TB), rb_ref, h)
    pb_ref[2] = _bias_from_bucket(_bucket(tk - tq - 2 * TB), rb_ref, h)
    nq = sbc_ref.shape[0]
    qpos = past + lax.broadcasted_iota(jnp.int32, (nq, past), 0)
    kpos = lax.broadcasted_iota(jnp.int32, (nq, past), 1)
    sbc_ref[...] = _bias_from_bucket(_bucket(kpos - qpos), rb_ref, h)
    qn = lax.broadcasted_iota(jnp.int32, (nq, nq), 0)
    kn = lax.broadcasted_iota(jnp.int32, (nq, nq), 1)
    sbn_ref[...] = _bias_from_bucket(_bucket(kn - qn), rb_ref, h)


def bias_tiles(rel_bias, past):
    return pl.pallas_call(
        functools.partial(_bias_body, past=past),
        grid=(N_HEADS,),
        in_specs=[pl.BlockSpec(memory_space=pltpu.SMEM)],
        out_specs=[
            pl.BlockSpec((None, 3, TB, TB), lambda h: (h, 0, 0, 0)),
            pl.BlockSpec((None, SUBLEN, past), lambda h: (h, 0, 0)),
            pl.BlockSpec((None, SUBLEN, SUBLEN), lambda h: (h, 0, 0)),
        ],
        out_shape=[
            jax.ShapeDtypeStruct((N_HEADS, 3, TB, TB), f32),
            jax.ShapeDtypeStruct((N_HEADS, SUBLEN, past), f32),
            jax.ShapeDtypeStruct((N_HEADS, SUBLEN, SUBLEN), f32),
        ],
        compiler_params=_cparams(("parallel",)),
        name="bias_tiles",
    )(rel_bias)


def _stack_maps(q):
    lane = lax.broadcasted_iota(jnp.int32, q.shape, 1)
    zero = jnp.zeros_like(q)
    return jnp.concatenate([jnp.where(lane < QK_DIM, q, zero), jnp.where(lane >= QK_DIM, q, zero)], axis=0)


def _online_softmax_step(s, v, m_prev, l_prev, acc_prev):
    m_new = jnp.maximum(m_prev, jnp.max(s, axis=-1, keepdims=True))
    alpha = jnp.exp(m_prev - m_new)
    p = jnp.exp(s - jnp.tile(m_new, (1, s.shape[1] // LANES)))
    l_new = alpha * l_prev + jnp.sum(p, axis=-1, keepdims=True)
    acc_new = alpha * acc_prev + jnp.dot(p.astype(bf16), v, preferred_element_type=f32)
    return m_new, l_new, acc_new


def _diff_combine(l, acc, lam, g, nq, lam_init):
    o = acc[:nq] / l[:nq] - lam * (acc[nq:] / l[nq:])
    ms = jnp.mean(o * o, axis=-1, keepdims=True)
    return o * lax.rsqrt(ms + EPS) * g * (1.0 - lam_init)


def _prompt_attn_body(lam_ref, q_ref, k_ref, v_ref, pb_ref, g_ref, o_ref, m_ref, l_ref, acc_ref, *, lam_init):
    qi = pl.program_id(1)
    kj = pl.program_id(2)

    @pl.when(kj == 0)
    def _():
        m_ref[...] = jnp.full(m_ref.shape, -jnp.inf, f32)
        l_ref[...] = jnp.zeros(l_ref.shape, f32)
        acc_ref[...] = jnp.zeros(acc_ref.shape, f32)

    @pl.when(kj <= qi)
    def _():
        for h in range(N_HEADS):
            sl = slice(h * V_DIM, (h + 1) * V_DIM)
            qq = _stack_maps(q_ref[:, sl])
            s = _dot_nt(qq, k_ref[:, sl])
            s = (s.reshape(2, TB, TB) + pb_ref[h][None]).reshape(2 * TB, TB)
            m, l, acc = _online_softmax_step(s, v_ref[:, sl], m_ref[h], l_ref[h], acc_ref[h])
            m_ref[h] = m
            l_ref[h] = l
            acc_ref[h] = acc

    @pl.when(kj == pl.num_programs(2) - 1)
    def _():
        for h in range(N_HEADS):
            o = _diff_combine(l_ref[h], acc_ref[h], lam_ref[0], g_ref[...], TB, lam_init)
            o_ref[:, h * V_DIM:(h + 1) * V_DIM] = o.astype(o_ref.dtype)


def prompt_attention(qn, kb, vb, pbias, lam, subln_g, *, n_seq, blocks_per_seq, lam_init):
    nb = blocks_per_seq
    m = n_seq * nb * TB
    kv = pl.BlockSpec((TB, ATT_WIDTH), lambda b, i, j: (b * nb + jnp.minimum(i, j), 0))
    return pl.pallas_call(
        functools.partial(_prompt_attn_body, lam_init=lam_init),
        grid=(n_seq, nb, nb),
        in_specs=[
            pl.BlockSpec(memory_space=pltpu.SMEM),
            pl.BlockSpec((TB, Q_WIDTH), lambda b, i, j: (b * nb + i, 0)),
            kv, kv,
            pl.BlockSpec((N_HEADS, None, TB, TB), lambda b, i, j: (0, jnp.clip(i - j, 0, 2), 0, 0)),
            pl.BlockSpec((1, V_DIM), lambda b, i, j: (0, 0)),
        ],
        out_specs=pl.BlockSpec((TB, ATT_WIDTH), lambda b, i, j: (b * nb + i, 0)),
        out_shape=jax.ShapeDtypeStruct((m, ATT_WIDTH), bf16),
        scratch_shapes=[pltpu.VMEM((N_HEADS, 2 * TB, LANES), f32), pltpu.VMEM((N_HEADS, 2 * TB, LANES), f32),
                        pltpu.VMEM((N_HEADS, 2 * TB, V_DIM), f32)],
        compiler_params=_cparams(("parallel", "parallel", "arbitrary")),
        name="prompt_attention",
    )(lam, qn, kb, vb, pbias, subln_g.reshape(1, V_DIM))


def _sample_attn_body(lam_ref, q_ref, kn_ref, vn_ref, ck_ref, cv_ref, sbc_ref, sbn_ref, g_ref, o_ref,
                      m_ref, l_ref, acc_ref, *, lam_init):
    kj = pl.program_id(1)
    nq = q_ref.shape[0]
    for h in range(N_HEADS):
        sl = slice(h * V_DIM, (h + 1) * V_DIM)
        qq = _stack_maps(q_ref[:, sl])

        @pl.when(kj == 0)
        def _():
            s = _dot_nt(qq, kn_ref[:, sl].astype(bf16))
            s = (s.reshape(2, nq, nq) + sbn_ref[h][None]).reshape(2 * nq, nq)
            m = jnp.max(s, axis=-1, keepdims=True)
            p = jnp.exp(s - m)
            m_ref[h] = jnp.broadcast_to(m, m_ref.shape[1:])
            l_ref[h] = jnp.broadcast_to(jnp.sum(p, axis=-1, keepdims=True), l_ref.shape[1:])
            acc_ref[h] = jnp.dot(p.astype(bf16), vn_ref[:, sl].astype(bf16), preferred_element_type=f32)

        ck = ck_ref.shape[0] // N_HEADS
        head_rows = pl.ds(h, ck, stride=N_HEADS)
        s = _dot_nt(qq, ck_ref[head_rows, :].astype(bf16))
        s = (s.reshape(2, nq, ck) + sbc_ref[h][None]).reshape(2 * nq, ck)
        m, l, acc = _online_softmax_step(s, cv_ref[head_rows, :].astype(bf16), m_ref[h], l_ref[h], acc_ref[h])
        m_ref[h] = m
        l_ref[h] = l
        acc_ref[h] = acc

        @pl.when(kj == pl.num_programs(1) - 1)
        def _():
            o = _diff_combine(l_ref[h], acc_ref[h], lam_ref[0], g_ref[...], nq, lam_init)
            o_ref[:, sl] = o.astype(o_ref.dtype)


def sample_attention(q_s, kn_s, v_s, cache_k, cache_v, sbc, sbn, lam, subln_g, *, layer, lam_init):
    nb, nq, _ = q_s.shape
    past = cache_k.shape[2] // N_HEADS
    chunk = min(past, 1024)
    return pl.pallas_call(
        functools.partial(_sample_attn_body, lam_init=lam_init),
        grid=(nb, past // chunk),
        in_specs=[
            pl.BlockSpec(memory_space=pltpu.SMEM),
            pl.BlockSpec((None, nq, ATT_WIDTH), lambda b, j: (b, 0, 0)),
            pl.BlockSpec((None, nq, ATT_WIDTH), lambda b, j: (b, 0, 0)),
            pl.BlockSpec((None, nq, ATT_WIDTH), lambda b, j: (b, 0, 0)),
            pl.BlockSpec((None, None, chunk * N_HEADS, V_DIM), lambda b, j: (layer, b, j, 0)),
            pl.BlockSpec((None, None, chunk * N_HEADS, V_DIM), lambda b, j: (layer, b, j, 0)),
            pl.BlockSpec((N_HEADS, nq, chunk), lambda b, j: (0, 0, j)),
            pl.BlockSpec((N_HEADS, nq, nq), lambda b, j: (0, 0, 0)),
            pl.BlockSpec((1, V_DIM), lambda b, j: (0, 0)),
        ],
        out_specs=pl.BlockSpec((None, nq, ATT_WIDTH), lambda b, j: (b, 0, 0)),
        out_shape=jax.ShapeDtypeStruct((nb, nq, ATT_WIDTH), bf16),
        scratch_shapes=[pltpu.VMEM((N_HEADS, 2 * nq, LANES), f32), pltpu.VMEM((N_HEADS, 2 * nq, LANES), f32),
                        pltpu.VMEM((N_HEADS, 2 * nq, V_DIM), f32)],
        compiler_params=_cparams(("parallel", "arbitrary")),
        name="sample_attention",
    )(lam, q_s, kn_s, v_s, cache_k, cache_v, sbc, sbn, subln_g.reshape(1, V_DIM))


def _chain_carries(c0, mult, add):
    rows = [c0]
    c = c0
    for s in range(NSUB):
        c = mult[s:s + 1] * c + add[s:s + 1]
        if s + 1 < NSUB:
            rows.append(c)
    return jnp.concatenate(rows, axis=0), c


def _rglru_body(xrg_ref, xgate_ref, head_s_ref, h0_s_ref, cw_ref, cb_ref, wa_ref, wx_ref, ba_ref, bx_ref,
                lam_ref, hg_ref, htail_ref, ctail_ref, xbuf_ref, a_ref, b_ref, hcar_ref, ptail_ref,
                *, blocks_per_seq, n_prompt_blocks):
    g = pl.program_id(0)
    is_sample = g >= n_prompt_blocks
    hl = (CONV_W - 1) * NSUB

    @pl.when(g % blocks_per_seq == 0)
    def _():
        hcar_ref[...] = jnp.zeros(hcar_ref.shape, f32)
        ptail_ref[...] = jnp.zeros(ptail_ref.shape, f32)

    x = xrg_ref[...]
    tail = x[TB - hl:]
    sub = lax.broadcasted_iota(jnp.int32, (NSUB, RG_WIDTH), 0)
    for k in range(CONV_W - 1):
        rs = slice(k * NSUB, (k + 1) * NSUB)
        prompt_head = jnp.where(sub == 0, pltpu.roll(ptail_ref[rs, :], 1, 0), pltpu.roll(tail[rs], 1, 0))
        xbuf_ref[rs, :] = jnp.where(is_sample, head_s_ref[rs, :], prompt_head)
    xbuf_ref[hl:, :] = x
    ptail_ref[...] = tail
    ctail_ref[...] = tail

    xc = cb_ref[...] + cw_ref[CONV_W - 1:CONV_W, :] * x
    for j in range(1, CONV_W):
        xc = xc + cw_ref[CONV_W - 1 - j:CONV_W - j, :] * xbuf_ref[hl - j * NSUB:hl - j * NSUB + TB, :]

    xcb = xc.astype(bf16)
    r_parts, i_parts = [], []
    for blk in range(RG_BLOCKS):
        sl = slice(blk * RG_BLOCK, (blk + 1) * RG_BLOCK)
        r_parts.append(jnp.dot(xcb[:, sl], wa_ref[blk], preferred_element_type=f32))
        i_parts.append(jnp.dot(xcb[:, sl], wx_ref[blk], preferred_element_type=f32))
    r = jax.nn.sigmoid(jnp.concatenate(r_parts, axis=1) + ba_ref[...])
    ig = jax.nn.sigmoid(jnp.concatenate(i_parts, axis=1) + bx_ref[...])
    nl = -lam_ref[...]
    softplus = jnp.maximum(nl, 0.0) + jnp.log1p(jnp.exp(-jnp.abs(nl)))
    a = jnp.exp(-RG_C * r * softplus)
    a_ref[...] = a
    b_ref[...] = jnp.sqrt(1.0 - a * a) * (ig * xc)

    def rows(i):
        return pl.ds(pl.multiple_of(i * NSUB, NSUB), NSUB)

    def local(i, carry):
        p, e = carry
        ai = a_ref[rows(i), :]
        return p * ai, ai * e + b_ref[rows(i), :]

    p, e = lax.fori_loop(0, SUBLEN, local,
                         (jnp.ones((NSUB, RG_WIDTH), f32), jnp.zeros((NSUB, RG_WIDTH), f32)), unroll=4)
    entry, exit_state = _chain_carries(hcar_ref[...], p, e)
    hcar_ref[...] = exit_state
    h_init = jnp.where(is_sample, h0_s_ref[...], entry)

    def step(i, hprev):
        hcur = a_ref[rows(i), :] * hprev + b_ref[rows(i), :]
        b_ref[rows(i), :] = hcur
        return hcur

    h_last = lax.fori_loop(0, SUBLEN, step, h_init, unroll=4)
    htail_ref[...] = h_last
    hg_ref[...] = (b_ref[...] * _gelu(xgate_ref[...])).astype(hg_ref.dtype)


def rglru_mixer(proj, head_s, h0_s, conv_w, conv_b, wa, wx, ba, bx, lam, *, blocks_per_seq, n_prompt_blocks):
    m = proj.shape[0]
    nblk = m // TB
    hl = (CONV_W - 1) * NSUB
    row = lambda v: v.reshape(1, RG_WIDTH)
    full = lambda shape: pl.BlockSpec(shape, lambda g: (0,) * len(shape))
    return pl.pallas_call(
        functools.partial(_rglru_body, blocks_per_seq=blocks_per_seq, n_prompt_blocks=n_prompt_blocks),
        grid=(nblk,),
        in_specs=[
            pl.BlockSpec((TB, RG_WIDTH), lambda g: (g, COL_RG)),
            pl.BlockSpec((TB, RG_WIDTH), lambda g: (g, COL_GATE)),
            full((hl, RG_WIDTH)),
            full((NSUB, RG_WIDTH)),
            full((CONV_W, RG_WIDTH)),
            full((1, RG_WIDTH)),
            full((RG_BLOCKS, RG_BLOCK, RG_BLOCK)),
            full((RG_BLOCKS, RG_BLOCK, RG_BLOCK)),
            full((1, RG_WIDTH)),
            full((1, RG_WIDTH)),
            full((1, RG_WIDTH)),
        ],
        out_specs=[
            pl.BlockSpec((TB, RG_WIDTH), lambda g: (g, 0)),
            pl.BlockSpec((None, NSUB, RG_WIDTH), lambda g: (g, 0, 0)),
            pl.BlockSpec((None, hl, RG_WIDTH), lambda g: (g, 0, 0)),
        ],
        out_shape=[
            jax.ShapeDtypeStruct((m, RG_WIDTH), bf16),
            jax.ShapeDtypeStruct((nblk, NSUB, RG_WIDTH), f32),
            jax.ShapeDtypeStruct((nblk, hl, RG_WIDTH), f32),
        ],
        scratch_shapes=[
            pltpu.VMEM((hl + TB, RG_WIDTH), f32),
            pltpu.VMEM((TB, RG_WIDTH), f32),
            pltpu.VMEM((TB, RG_WIDTH), f32),
            pltpu.VMEM((1, RG_WIDTH), f32),
            pltpu.VMEM((hl, RG_WIDTH), f32),
        ],
        compiler_params=_cparams(("arbitrary",)),
        name="rglru_mixer",
    )(proj, proj, head_s, h0_s, conv_w, row(conv_b), wa.astype(bf16), wx.astype(bf16), row(ba), row(bx),
      row(lam))


S5_CHUNKS = 8
S5_CH_IN = S5_WIDTH // S5_CHUNKS
S5_CH_ST = S5_LANES // S5_CHUNKS
S5_SCAN_LANES = 512


def _cmul(ar, ai, br, bi):
    return ar * br - ai * bi, ar * bi + ai * br


def _s5_body(u_ref, s0r_ref, s0i_ref, abr_ref, abi_ref, bdr_ref, bdi_ref, cdr_ref, cdi_ref, d_ref, gw_ref,
             gb_ref, gs_ref, str_ref, sti_ref, xr_ref, xi_ref, car_ref, cai_ref,
             *, blocks_per_seq, n_prompt_blocks):
    g = pl.program_id(0)
    is_sample = g >= n_prompt_blocks

    @pl.when(g % blocks_per_seq == 0)
    def _():
        car_ref[...] = jnp.zeros(car_ref.shape, f32)
        cai_ref[...] = jnp.zeros(cai_ref.shape, f32)

    u = u_ref[...]
    ub = u.astype(bf16)
    for c in range(S5_CHUNKS):
        ci = slice(c * S5_CH_IN, (c + 1) * S5_CH_IN)
        cs = slice(c * S5_CH_ST, (c + 1) * S5_CH_ST)
        xr_ref[:, cs] = jnp.dot(ub[:, ci], bdr_ref[c], preferred_element_type=f32)
        xi_ref[:, cs] = jnp.dot(ub[:, ci], bdi_ref[c], preferred_element_type=f32)

    def rows(i):
        return pl.ds(pl.multiple_of(i * NSUB, NSUB), NSUB)

    for c in range(S5_LANES // S5_SCAN_LANES):
        ls = slice(c * S5_SCAN_LANES, (c + 1) * S5_SCAN_LANES)
        ar1 = abr_ref[:, ls]
        ai1 = abi_ref[:, ls]
        ar = jnp.broadcast_to(ar1, (NSUB, S5_SCAN_LANES))
        ai = jnp.broadcast_to(ai1, (NSUB, S5_SCAN_LANES))

        def local(i, carry, ls=ls, ar=ar, ai=ai):
            er, ei = carry
            nr, ni = _cmul(ar, ai, er, ei)
            return nr + xr_ref[rows(i), ls], ni + xi_ref[rows(i), ls]

        zero = jnp.zeros((NSUB, S5_SCAN_LANES), f32)
        er, ei = lax.fori_loop(0, SUBLEN, local, (zero, zero), unroll=4)
        pr, pi = ar1, ai1
        for _ in range(int(math.log2(SUBLEN))):
            pr, pi = _cmul(pr, pi, pr, pi)
        rows_r, rows_i = [car_ref[:, ls]], [cai_ref[:, ls]]
        cr, ci_ = rows_r[0], rows_i[0]
        for s in range(NSUB):
            mr, mi = _cmul(pr, pi, cr, ci_)
            cr, ci_ = mr + er[s:s + 1], mi + ei[s:s + 1]
            if s + 1 < NSUB:
                rows_r.append(cr)
                rows_i.append(ci_)
        car_ref[:, ls] = cr
        cai_ref[:, ls] = ci_
        init_r = jnp.where(is_sample, s0r_ref[:, ls], jnp.concatenate(rows_r, axis=0))
        init_i = jnp.where(is_sample, s0i_ref[:, ls], jnp.concatenate(rows_i, axis=0))

        def step(i, carry, ls=ls, ar=ar, ai=ai):
            sr, si = carry
            nr, ni = _cmul(ar, ai, sr, si)
            nr = nr + xr_ref[rows(i), ls]
            ni = ni + xi_ref[rows(i), ls]
            xr_ref[rows(i), ls] = nr
            xi_ref[rows(i), ls] = ni
            return nr, ni

        sr, si = lax.fori_loop(0, SUBLEN, step, (init_r, init_i), unroll=4)
        str_ref[:, ls] = sr
        sti_ref[:, ls] = si

    y_parts = []
    for c in range(S5_CHUNKS):
        cs = slice(c * S5_CH_ST, (c + 1) * S5_CH_ST)
        y_parts.append(jnp.dot(xr_ref[:, cs].astype(bf16), cdr_ref[c], preferred_element_type=f32)
                       + jnp.dot(xi_ref[:, cs].astype(bf16), cdi_ref[c], preferred_element_type=f32))
    y = jnp.concatenate(y_parts, axis=1) + d_ref[...] * u
    gy = _gelu(y)
    z = jnp.dot(gy.astype(bf16), gw_ref[...], preferred_element_type=f32) + gb_ref[...]
    gs_ref[...] = (gy * jax.nn.sigmoid(z)).astype(gs_ref.dtype)


def _s5_discretise(lam_re, lam_im, log_step, b_re, b_im, c_re, c_im):
    step = jnp.exp(log_step)[:, None]
    mag = jnp.exp(lam_re * step)
    ab_re, ab_im = mag * jnp.cos(lam_im * step), mag * jnp.sin(lam_im * step)
    den = lam_re * lam_re + lam_im * lam_im
    nr, ni = ab_re - 1.0, ab_im
    co_re = (nr * lam_re + ni * lam_im) / den
    co_im = (ni * lam_re - nr * lam_im) / den
    bb_re = co_re[..., None] * b_re - co_im[..., None] * b_im
    bb_im = co_re[..., None] * b_im + co_im[..., None] * b_re
    gpc = S5_GROUPS // S5_CHUNKS
    eye = jnp.eye(gpc, dtype=f32)

    def in_chunks(bb):
        t = bb.reshape(S5_CHUNKS, gpc, S5_STATE, S5_GROUP)
        return jnp.einsum('cgph,gk->cghkp', t, eye).reshape(S5_CHUNKS, S5_CH_IN, S5_CH_ST).astype(bf16)

    def out_chunks(cc):
        t = cc.reshape(S5_CHUNKS, gpc, S5_GROUP, S5_STATE)
        return jnp.einsum('cghp,gk->cgpkh', t, eye).reshape(S5_CHUNKS, S5_CH_ST, S5_CH_IN).astype(bf16)

    return (ab_re.reshape(1, S5_LANES), ab_im.reshape(1, S5_LANES), in_chunks(bb_re), in_chunks(bb_im),
            out_chunks(c_re), out_chunks(-c_im))


def s5_mixer(proj, s0_re, s0_im, disc, d, glu_w, glu_b, *, layer, blocks_per_seq, n_prompt_blocks):
    m = proj.shape[0]
    nblk = m // TB
    ab_re, ab_im, bd_re, bd_im, cd_re, cd_im = disc
    full = lambda shape: pl.BlockSpec(shape, lambda g: (0,) * len(shape))
    return pl.pallas_call(
        functools.partial(_s5_body, blocks_per_seq=blocks_per_seq, n_prompt_blocks=n_prompt_blocks),
        grid=(nblk,),
        in_specs=[
            pl.BlockSpec((TB, S5_WIDTH), lambda g: (g, COL_S5)),
            full((NSUB, S5_LANES)),
            full((NSUB, S5_LANES)),
            full((1, S5_LANES)),
            full((1, S5_LANES)),
            full((S5_CHUNKS, S5_CH_IN, S5_CH_ST)),
            full((S5_CHUNKS, S5_CH_IN, S5_CH_ST)),
            full((S5_CHUNKS, S5_CH_ST, S5_CH_IN)),
            full((S5_CHUNKS, S5_CH_ST, S5_CH_IN)),
            full((1, S5_WIDTH)),
            pl.BlockSpec((None, S5_WIDTH, S5_WIDTH), lambda g: (layer, 0, 0)),
            full((1, S5_WIDTH)),
        ],
        out_specs=[
            pl.BlockSpec((TB, S5_WIDTH), lambda g: (g, 0)),
            pl.BlockSpec((None, NSUB, S5_LANES), lambda g: (g, 0, 0)),
            pl.BlockSpec((None, NSUB, S5_LANES), lambda g: (g, 0, 0)),
        ],
        out_shape=[
            jax.ShapeDtypeStruct((m, S5_WIDTH), bf16),
            jax.ShapeDtypeStruct((nblk, NSUB, S5_LANES), f32),
            jax.ShapeDtypeStruct((nblk, NSUB, S5_LANES), f32),
        ],
        scratch_shapes=[
            pltpu.VMEM((TB, S5_LANES), f32),
            pltpu.VMEM((TB, S5_LANES), f32),
            pltpu.VMEM((1, S5_LANES), f32),
            pltpu.VMEM((1, S5_LANES), f32),
        ],
        compiler_params=_cparams(("arbitrary",)),
        name="s5_mixer",
    )(proj, s0_re, s0_im, ab_re, ab_im, bd_re, bd_im, cd_re, cd_im, d.reshape(1, S5_WIDTH),
      glu_w, glu_b.reshape(1, S5_WIDTH))


def _merge_body(att_ref, rg_ref, s5_ref, wa_ref, wr_ref, ws_ref, g0_ref, g1_ref, g2_ref, b0_ref, b1_ref,
                b2_ref, o_ref):
    acc = jax.nn.sigmoid(g0_ref[...] + b0_ref[...]) * jnp.dot(att_ref[...], wa_ref[...],
                                                               preferred_element_type=f32)
    acc = acc + jax.nn.sigmoid(g1_ref[...] + b1_ref[...]) * jnp.dot(rg_ref[...], wr_ref[...],
                                                                     preferred_element_type=f32)
    acc = acc + jax.nn.sigmoid(g2_ref[...] + b2_ref[...]) * jnp.dot(s5_ref[...], ws_ref[...],
                                                                     preferred_element_type=f32)
    o_ref[...] = acc.astype(o_ref.dtype)


def gated_merge(o_att, hg, gs, w_att_o, w_rg_o, w_s5_o, proj, b_gate, *, layer, tm, tn):
    m = o_att.shape[0]
    k = o_att.shape[1]
    gcol = COL_GATES // tn
    bcol = D_MODEL // tn
    act = pl.BlockSpec((tm, k), lambda i, j: (i, 0))
    wsp = pl.BlockSpec((None, k, tn), lambda i, j: (layer, 0, j))
    gate = lambda br: pl.BlockSpec((tm, tn), lambda i, j: (i, gcol + br * bcol + j))
    bias = lambda br: pl.BlockSpec((1, tn), lambda i, j: (0, br * bcol + j))
    bg = b_gate.reshape(1, N_BRANCH * D_MODEL)
    return pl.pallas_call(
        _merge_body,
        grid=(m // tm, D_MODEL // tn),
        in_specs=[act, act, act, wsp, wsp, wsp, gate(0), gate(1), gate(2), bias(0), bias(1), bias(2)],
        out_specs=pl.BlockSpec((tm, tn), lambda i, j: (i, j)),
        out_shape=jax.ShapeDtypeStruct((m, D_MODEL), bf16),
        compiler_params=_cparams(("parallel", "arbitrary")),
        name="gated_merge",
    )(o_att, hg, gs, w_att_o, w_rg_o, w_s5_o, proj, proj, proj, bg, bg, bg)


def _residual_matmul_body(x_ref, a_ref, w_ref, o_ref):
    o_ref[...] = x_ref[...] + jnp.dot(a_ref[...], w_ref[...], preferred_element_type=f32)


def residual_matmul(x, a, w, *, layer, tm, tn):
    m, n = x.shape
    k = a.shape[1]
    return pl.pallas_call(
        _residual_matmul_body,
        grid=(m // tm, n // tn),
        in_specs=[
            pl.BlockSpec((tm, tn), lambda i, j: (i, j)),
            pl.BlockSpec((tm, k), lambda i, j: (i, 0)),
            pl.BlockSpec((None, k, tn), lambda i, j: (layer, 0, j)),
        ],
        out_specs=pl.BlockSpec((tm, tn), lambda i, j: (i, j)),
        out_shape=jax.ShapeDtypeStruct((m, n), f32),
        compiler_params=_cparams(("parallel", "arbitrary")),
        name="residual_matmul",
    )(x, a, w)


PEER_TT = 256
PEER_ET = 1024
PEER_I1 = PEER_ET // N_KEYS
PEER_PIECE = 32
PEER_RG = 2


def _top_values(s, k):
    out = []
    for _ in range(k):
        mx = jnp.max(s, axis=0, keepdims=True)
        out.append(mx)
        s = jnp.where(s == mx, -jnp.inf, s)
    return out


def _peer_route_body(q_ref, sk_ref, s2_ref, e2_ref, thr_ref, c1_ref):
    for h in range(PEER_HEADS):
        s1 = _dot_nt(sk_ref[0], q_ref[:, (2 * h) * N_KEYS:(2 * h + 1) * N_KEYS])
        s2 = _dot_nt(sk_ref[1], q_ref[:, (2 * h + 1) * N_KEYS:(2 * h + 2) * N_KEYS])
        v1 = _top_values(s1, PEER_TOPK)
        v2 = _top_values(s2, PEER_TOPK)
        cand = [v1[k1] + v2[k2] for k1 in range(PEER_TOPK) for k2 in range(PEER_TOPK // (k1 + 1))]
        pad = (-len(cand)) % SUBLANES
        cand = jnp.concatenate(cand + [jnp.full_like(cand[0], -jnp.inf)] * pad, axis=0)
        top = _top_values(cand, PEER_TOPK)
        theta = top[-1]
        z = top[0] * 0.0
        for t in top:
            z = z + jnp.exp(t - top[0])
        s2_ref[h] = s2
        e2_ref[h] = jnp.exp(s2 - v2[0])
        thr_ref[h] = theta - s1
        c1_ref[h] = jnp.exp(s1 - v1[0]) / z


def peer_route(q, subkeys):
    m = q.shape[0]
    shp = jax.ShapeDtypeStruct((PEER_HEADS, N_KEYS, m), f32)
    spec = pl.BlockSpec((PEER_HEADS, N_KEYS, PEER_TT), lambda i: (0, 0, i))
    return pl.pallas_call(
        _peer_route_body,
        grid=(m // PEER_TT,),
        in_specs=[
            pl.BlockSpec((PEER_TT, q.shape[1]), lambda i: (i, 0)),
            pl.BlockSpec((2, N_KEYS, N_KEYS), lambda i: (0, 0, 0)),
        ],
        out_specs=[spec] * 4,
        out_shape=[shp] * 4,
        compiler_params=_cparams(("parallel",)),
        name="peer_route",
    )(q, subkeys)


def _peer_gate_weights(s2_ref, e2_ref, thr_ref, c1_ref, w_ref, token_tiles):
    n_pc = N_KEYS // PEER_PIECE
    for tc in token_tiles:
        ts = slice(tc * LANES, (tc + 1) * LANES)
        for r0 in range(0, PEER_I1, PEER_RG):
            w = [[jnp.zeros((PEER_PIECE, LANES), f32) for _ in range(n_pc)] for _ in range(PEER_RG)]
            for h in range(PEER_HEADS):
                thr = [thr_ref[h, r0 + r:r0 + r + 1, ts] for r in range(PEER_RG)]
                c1 = [c1_ref[h, r0 + r:r0 + r + 1, ts] for r in range(PEER_RG)]
                for pc in range(n_pc):
                    ks = slice(pc * PEER_PIECE, (pc + 1) * PEER_PIECE)
                    s2 = s2_ref[h, ks, ts]
                    e2 = e2_ref[h, ks, ts]
                    for r in range(PEER_RG):
                        w[r][pc] = w[r][pc] + jnp.where(s2 >= thr[r], e2 * c1[r], 0.0)
            for r in range(PEER_RG):
                for pc in range(n_pc):
                    base = (r0 + r) * N_KEYS + pc * PEER_PIECE
                    w_ref[base:base + PEER_PIECE, ts] = w[r][pc]


def _peer_expert_body(xnt_ref, u_ref, vt_ref, s2_ref, e2_ref, thr0_ref, c10_ref, thrn_ref, c1n_ref, o_ref,
                      act_ref, w_ref):
    j = pl.program_id(1)
    n_tiles = o_ref.shape[1] // LANES
    cur = w_ref.at[j % 2]
    nxt = w_ref.at[(j + 1) % 2]

    @pl.when(j == 0)
    def _():
        o_ref[...] = jnp.zeros(o_ref.shape, f32)
        _peer_gate_weights(s2_ref, e2_ref, thr0_ref, c10_ref, cur, range(n_tiles))

    zt = jnp.dot(u_ref[...], xnt_ref[...], preferred_element_type=f32)
    _peer_gate_weights(s2_ref, e2_ref, thrn_ref, c1n_ref, nxt, range(n_tiles // 2))
    act_ref[...] = (_gelu(zt) * cur[...]).astype(bf16)
    o_ref[...] += jnp.dot(vt_ref[...], act_ref[...], preferred_element_type=f32)
    _peer_gate_weights(s2_ref, e2_ref, thrn_ref, c1n_ref, nxt, range(n_tiles // 2, n_tiles))


def peer_experts(xnt, u_tab, vt_tab, s2, e2, thr, c1, *, layer, tt):
    d, m = xnt.shape
    thr = thr.reshape(PEER_HEADS, N_KEYS // PEER_I1, PEER_I1, m)
    c1 = c1.reshape(PEER_HEADS, N_KEYS // PEER_I1, PEER_I1, m)
    once = pl.Buffered(1)
    tok = pl.BlockSpec((PEER_HEADS, N_KEYS, tt), lambda i, j: (0, 0, i), pipeline_mode=once)
    n_steps = N_EXPERTS // PEER_ET
    row0 = pl.BlockSpec((PEER_HEADS, None, PEER_I1, tt), lambda i, j: (0, 0, 0, i))
    row_next = pl.BlockSpec((PEER_HEADS, None, PEER_I1, tt), lambda i, j: (0, jnp.minimum(j + 1, n_steps - 1), 0, i))
    return pl.pallas_call(
        _peer_expert_body,
        grid=(m // tt, n_steps),
        in_specs=[
            pl.BlockSpec((d, tt), lambda i, j: (0, i), pipeline_mode=once),
            pl.BlockSpec((None, PEER_ET, d), lambda i, j: (layer, j, 0)),
            pl.BlockSpec((None, d, PEER_ET), lambda i, j: (layer, 0, j)),
            tok, tok, row0, row0, row_next, row_next,
        ],
        out_specs=pl.BlockSpec((d, tt), lambda i, j: (0, i)),
        out_shape=jax.ShapeDtypeStruct((d, m), f32),
        scratch_shapes=[pltpu.VMEM((PEER_ET, tt), bf16), pltpu.VMEM((2, PEER_ET, tt), f32)],
        compiler_params=_cparams(("parallel", "arbitrary")),
        name="peer_experts",
    )(xnt, u_tab, vt_tab, s2, e2, thr, c1, thr, c1)


def _add_transposed_body(x_ref, dt_ref, o_ref):
    o_ref[...] = x_ref[...] + dt_ref[...].T


def _add_transposed_split_body(x_ref, dt_ref, op_ref, os_ref, *, n_prompt_blocks):
    y = x_ref[...] + dt_ref[...].T
    is_sample = pl.program_id(0) >= n_prompt_blocks

    @pl.when(jnp.logical_not(is_sample))
    def _():
        op_ref[...] = y

    @pl.when(is_sample)
    def _():
        os_ref[...] = y


def add_transposed_split(x, dt, *, n_prompt_blocks):
    m, d = x.shape
    mp = n_prompt_blocks * TB
    assert m == mp + TB
    return pl.pallas_call(
        functools.partial(_add_transposed_split_body, n_prompt_blocks=n_prompt_blocks),
        grid=(m // TB,),
        in_specs=[pl.BlockSpec((TB, d), lambda i: (i, 0)), pl.BlockSpec((d, TB), lambda i: (0, i))],
        out_specs=[pl.BlockSpec((TB, d), lambda i: (jnp.minimum(i, n_prompt_blocks - 1), 0)),
                   pl.BlockSpec((TB, d), lambda i: (0, 0))],
        out_shape=[jax.ShapeDtypeStruct((mp, d), f32), jax.ShapeDtypeStruct((TB, d), f32)],
        compiler_params=_cparams(("arbitrary",)),
        name="add_transposed_split",
    )(x, dt)


def add_transposed(x, dt, *, tm):
    m, d = x.shape
    return pl.pallas_call(
        _add_transposed_body,
        grid=(m // tm,),
        in_specs=[pl.BlockSpec((tm, d), lambda i: (i, 0)), pl.BlockSpec((d, tm), lambda i: (0, i))],
        out_specs=pl.BlockSpec((tm, d), lambda i: (i, 0)),
        out_shape=jax.ShapeDtypeStruct((m, d), f32),
        compiler_params=_cparams(("parallel",)),
        name="add_transposed",
    )(x, dt)


def _to_blocks(x):
    b, t, c = x.shape
    return x.reshape(b, t // TB, NSUB, SUBLEN, c).transpose(0, 1, 3, 2, 4).reshape(b * t, c)


def _from_blocks(y, b):
    m, c = y.shape
    t = m // b
    return y.reshape(b, t // TB, SUBLEN, NSUB, c).transpose(0, 1, 3, 2, 4).reshape(b, t, c)


def _layer(x, l, p, wb, tiles, dims, state, final):
    n_seq, blocks_per_seq, n_stream = dims
    n_prompt_blocks = n_seq * blocks_per_seq
    mp = n_prompt_blocks * TB
    pbias, sbc, sbn = tiles
    cache_k, cache_v, rg_h0, rg_conv0, s5_re0, s5_im0 = state
    tm = 3 * TB if x.shape[0] % (3 * TB) == 0 else TB
    tm_big = 1408 if x.shape[0] % 1408 == 0 else tm
    lam_init = 0.8 - 0.6 * math.exp(-0.3 * l)
    lam = (jnp.exp(jnp.sum(p['lam_q1'] * p['lam_k1'])) - jnp.exp(jnp.sum(p['lam_q2'] * p['lam_k2']))
           + lam_init).reshape(1)

    proj = norm_matmul(x, p['norm1_g'], wb['w_in'], layer=l, tm=tm_big, tn=1024, out_dtype=f32)
    qn, kb, vb, kn_p, kn_s, v_p, v_s = qk_norm(proj, p['qn_g'], p['kn_g'], n_prompt_blocks=n_prompt_blocks)

    o_att = prompt_attention(qn, kb, vb, pbias, lam, p['subln_g'], n_seq=n_seq,
                             blocks_per_seq=blocks_per_seq, lam_init=lam_init)
    stream = lambda a: a.reshape(SUBLEN, n_stream, -1).transpose(1, 0, 2)
    rows = lambda c: c.reshape(c.shape[0], c.shape[1], -1, V_DIM)
    o_s = sample_attention(stream(qn[mp:]), stream(kn_s), stream(v_s), rows(cache_k), rows(cache_v), sbc, sbn,
                           lam, p['subln_g'], layer=l, lam_init=lam_init)
    o_att = jnp.concatenate([o_att, o_s.transpose(1, 0, 2).reshape(TB, ATT_WIDTH)], axis=0)

    head_s = rg_conv0.transpose(1, 0, 2).reshape((CONV_W - 1) * NSUB, RG_WIDTH)
    hg, h_tail, c_tail = rglru_mixer(proj, head_s, rg_h0, p['rg_conv_w'], p['rg_conv_b'], p['rg_wa'],
                                     p['rg_wx'], p['rg_ba'], p['rg_bx'], p['rg_lambda'],
                                     blocks_per_seq=blocks_per_seq, n_prompt_blocks=n_prompt_blocks)
    disc = _s5_discretise(p['s5_lam_re'], p['s5_lam_im'], p['s5_log_step'], p['s5_b_re'], p['s5_b_im'],
                          p['s5_c_re'], p['s5_c_im'])
    gs, st_re, st_im = s5_mixer(proj, s5_re0.reshape(NSUB, S5_LANES), s5_im0.reshape(NSUB, S5_LANES), disc,
                                p['s5_d'], wb['s5_glu_w'], p['s5_glu_b'], layer=l,
                                blocks_per_seq=blocks_per_seq, n_prompt_blocks=n_prompt_blocks)

    merged = gated_merge(o_att, hg, gs, wb['w_att_o'], wb['w_rg_o'], wb['w_s5_o'], proj, p['b_gate'],
                         layer=l, tm=tm_big, tn=512)
    x = residual_matmul(x, merged, wb['w_out'], layer=l, tm=tm_big, tn=1024)

    q, xn2 = norm_matmul(x, p['norm2_g'], wb['peer_wq'], layer=l, tm=tm, tn=1024, out_dtype=bf16,
                         emit_xn=True)
    s2, e2, thr, c1 = peer_route(q, p['peer_subkeys'].astype(bf16))
    peer_t = peer_experts(xn2.T, wb['peer_u'], wb['peer_vt'], s2, e2, thr, c1, layer=l, tt=tm)
    if final:
        x = add_transposed_split(x, peer_t, n_prompt_blocks=n_prompt_blocks)
    else:
        x = add_transposed(x, peer_t, tm=TB)

    last = jnp.arange(n_seq) * blocks_per_seq + blocks_per_seq - 1
    hl = CONV_W - 1
    outs_p = (
        _from_blocks(kn_p, n_seq).reshape(n_seq, -1, N_HEADS, 2 * QK_DIM),
        _from_blocks(v_p, n_seq).reshape(n_seq, -1, N_HEADS, V_DIM),
        h_tail[last, NSUB - 1],
        c_tail[last].reshape(n_seq, hl, NSUB, RG_WIDTH)[:, :, NSUB - 1],
        st_re[last, NSUB - 1].reshape(n_seq, S5_GROUPS, S5_STATE),
        st_im[last, NSUB - 1].reshape(n_seq, S5_GROUPS, S5_STATE),
    )
    outs_s = (
        stream(kn_s).reshape(n_stream, SUBLEN, N_HEADS, 2 * QK_DIM),
        stream(v_s).reshape(n_stream, SUBLEN, N_HEADS, V_DIM),
        h_tail[n_prompt_blocks],
        c_tail[n_prompt_blocks].reshape(hl, NSUB, RG_WIDTH).transpose(1, 0, 2),
        st_re[n_prompt_blocks].reshape(n_stream, S5_GROUPS, S5_STATE),
        st_im[n_prompt_blocks].reshape(n_stream, S5_GROUPS, S5_STATE),
    )
    return x, outs_p, outs_s


def kernel(x_prompt, x_sample, cache_k, cache_v, state_rg_h, state_rg_conv, state_s5_re, state_s5_im, rel_bias, norm1_g, w_in, b_gate, qn_g, kn_g, lam_q1, lam_k1, lam_q2, lam_k2, subln_g, w_att_o, rg_conv_w, rg_conv_b, rg_wa, rg_ba, rg_wx, rg_bx, rg_lambda, w_rg_o, s5_lam_re, s5_lam_im, s5_log_step, s5_b_re, s5_b_im, s5_c_re, s5_c_im, s5_d, s5_glu_w, s5_glu_b, w_s5_o, w_out, norm2_g, peer_wq, peer_subkeys, peer_u, peer_v):
    n_seq, t_prompt, _ = x_prompt.shape
    n_stream, t_sample, _ = x_sample.shape
    depth = w_in.shape[0]
    assert n_stream == NSUB and t_sample == SUBLEN and t_prompt % TB == 0
    blocks_per_seq = t_prompt // TB
    dims = (n_seq, blocks_per_seq, n_stream)
    past = cache_k.shape[2]

    small = dict(norm1_g=norm1_g, b_gate=b_gate, qn_g=qn_g, kn_g=kn_g, lam_q1=lam_q1,
                 lam_k1=lam_k1, lam_q2=lam_q2, lam_k2=lam_k2, subln_g=subln_g,
                 rg_conv_w=rg_conv_w, rg_conv_b=rg_conv_b, rg_wa=rg_wa, rg_ba=rg_ba, rg_wx=rg_wx,
                 rg_bx=rg_bx, rg_lambda=rg_lambda, s5_lam_re=s5_lam_re,
                 s5_lam_im=s5_lam_im, s5_log_step=s5_log_step, s5_b_re=s5_b_re, s5_b_im=s5_b_im,
                 s5_c_re=s5_c_re, s5_c_im=s5_c_im, s5_d=s5_d, s5_glu_b=s5_glu_b,
                 norm2_g=norm2_g, peer_subkeys=peer_subkeys)
    wb = dict(w_in=w_in.astype(bf16), w_att_o=w_att_o.astype(bf16), w_rg_o=w_rg_o.astype(bf16),
              w_s5_o=w_s5_o.astype(bf16), w_out=w_out.astype(bf16), s5_glu_w=s5_glu_w.astype(bf16),
              peer_wq=peer_wq.astype(bf16), peer_u=peer_u.astype(bf16),
              peer_vt=jnp.swapaxes(peer_v, 1, 2).astype(bf16))

    tiles = bias_tiles(rel_bias, past)
    x = jnp.concatenate([_to_blocks(x_prompt), x_sample.transpose(1, 0, 2).reshape(TB, D_MODEL)], axis=0)
    outs_p, outs_s = [], []
    for l in range(depth):
        p = {k: v[l] for k, v in small.items()}
        state = (cache_k, cache_v, state_rg_h[l], state_rg_conv[l], state_s5_re[l], state_s5_im[l])
        x, op, os_ = _layer(x, l, p, wb, tiles, dims, state, final=(l == depth - 1))
        outs_p.append(op)
        outs_s.append(os_)

    x_prompt_rows, x_sample_rows = x
    y_prompt = _from_blocks(x_prompt_rows, n_seq)
    y_sample = x_sample_rows.reshape(SUBLEN, n_stream, D_MODEL).transpose(1, 0, 2)
    stack = lambda outs, i: jnp.stack([o[i] for o in outs])
    return (y_prompt, y_sample) + tuple(stack(outs_p, i) for i in range(6)) + tuple(stack(outs_s, i)
                                                                                    for i in range(6))
```

```python
import functools
import math

import jax
import jax.numpy as jnp
from jax import lax
from jax.experimental import pallas as pl
from jax.experimental.pallas import tpu as pltpu

f32 = jnp.float32
bf16 = jnp.bfloat16

D_MODEL = 2048
CHUNK = 64
EPS = 1e-6
N_HEADS = 8
QK_DIM = 64
V_DIM = 128
Q_WIDTH = 1024
ATT_WIDTH = 1024
N_BUCKETS = 32
MAX_DISTANCE = 128
RG_WIDTH = 1024
RG_BLOCKS = 8
RG_BLOCK = 128
CONV_W = 4
RG_C = 8.0
S5_WIDTH = 1024
S5_GROUP = 16
S5_GROUPS = 64
S5_STATE = 64
S5_LANES = S5_GROUPS * S5_STATE
PEER_HEADS = 8
N_KEYS = 128
N_EXPERTS = N_KEYS * N_KEYS
PEER_TOPK = 16
N_BRANCH = 3
IN_COLS = 2 * Q_WIDTH + ATT_WIDTH + 2 * RG_WIDTH + S5_WIDTH + N_BRANCH * D_MODEL

SUBLANES = 8
LANES = 128
VMEM_LIMIT = 56 * 1024 * 1024

NSUB = SUBLANES
SUBLEN = 32
TB = NSUB * SUBLEN
MASK_NEG = -1e30

COL_Q, COL_K, COL_V, COL_RG, COL_GATE, COL_S5 = 0, 1, 2, 3, 4, 5
COL_GATES = 6 * 1024


def _cparams(sem):
    return pltpu.CompilerParams(dimension_semantics=sem, vmem_limit_bytes=VMEM_LIMIT)


def _gelu(x):
    return 0.5 * x * (1.0 + lax.erf(x * (2.0 ** -0.5)))


def _dot_nt(a, b):
    return lax.dot_general(a, b, (((1,), (1,)), ((), ())), preferred_element_type=f32)


def _dot_tn(a, b):
    return lax.dot_general(a, b, (((0,), (0,)), ((), ())), preferred_element_type=f32)


def _norm_matmul_body(x_ref, g_ref, w_ref, o_ref, *rest, emit_xn):
    xn_ref = rest[-1]

    @pl.when(pl.program_id(1) == 0)
    def _():
        x = x_ref[...]
        ms = jnp.mean(x * x, axis=-1, keepdims=True)
        xn = (x * lax.rsqrt(ms + EPS) * g_ref[...]).astype(bf16)
        xn_ref[...] = xn
        if emit_xn:
            rest[0][...] = xn

    o_ref[...] = jnp.dot(xn_ref[...], w_ref[...], preferred_element_type=f32).astype(o_ref.dtype)


def norm_matmul(x, g, w, *, layer, tm, tn, out_dtype, emit_xn=False):
    m, d = x.shape
    n = w.shape[2]
    out_shape = [jax.ShapeDtypeStruct((m, n), out_dtype)]
    out_specs = [pl.BlockSpec((tm, tn), lambda i, j: (i, j))]
    if emit_xn:
        out_shape.append(jax.ShapeDtypeStruct((m, d), bf16))
        out_specs.append(pl.BlockSpec((tm, d), lambda i, j: (i, 0)))
    res = pl.pallas_call(
        functools.partial(_norm_matmul_body, emit_xn=emit_xn),
        grid=(m // tm, n // tn),
        in_specs=[
            pl.BlockSpec((tm, d), lambda i, j: (i, 0)),
            pl.BlockSpec((1, d), lambda i, j: (0, 0)),
            pl.BlockSpec((None, d, tn), lambda i, j: (layer, 0, j)),
        ],
        out_specs=out_specs,
        out_shape=out_shape,
        scratch_shapes=[pltpu.VMEM((tm, d), bf16)],
        compiler_params=_cparams(("parallel", "arbitrary")),
        name="norm_matmul",
    )(x, g.reshape(1, d), w)
    return res if emit_xn else res[0]


def _group_mean_sq(x, ones_bd):
    sq = x * x
    hi = sq.astype(bf16)
    lo = (sq - hi.astype(f32)).astype(bf16)
    s = jnp.dot(hi, ones_bd, preferred_element_type=f32) + jnp.dot(lo, ones_bd, preferred_element_type=f32)
    return s * (1.0 / QK_DIM)


def _qknorm_body(q_ref, k_ref, v_ref, qg_ref, kg_ref, qo_ref, kb_ref, vb_ref, kp_ref, ks_ref, vp_ref, vs_ref,
                 kn_ref, *, n_prompt_blocks):
    r = lax.broadcasted_iota(jnp.int32, (LANES, LANES), 0) // QK_DIM
    c = lax.broadcasted_iota(jnp.int32, (LANES, LANES), 1) // QK_DIM
    ones_bd = (r == c).astype(bf16)
    for t in range(Q_WIDTH // LANES):
        sl = slice(t * LANES, (t + 1) * LANES)
        q = q_ref[:, sl]
        k = k_ref[:, sl]
        qn = q * lax.rsqrt(_group_mean_sq(q, ones_bd) + EPS) * qg_ref[:, sl]
        kn = k * lax.rsqrt(_group_mean_sq(k, ones_bd) + EPS) * kg_ref[:, sl]
        qo_ref[:, sl] = (qn * (QK_DIM ** -0.5)).astype(bf16)
        kn_ref[:, sl] = kn
        kb_ref[:, sl] = kn.astype(bf16)
    vb_ref[...] = v_ref[...].astype(bf16)

    is_sample = pl.program_id(0) >= n_prompt_blocks

    @pl.when(jnp.logical_not(is_sample))
    def _():
        kp_ref[...] = kn_ref[...]
        vp_ref[...] = v_ref[...]

    @pl.when(is_sample)
    def _():
        ks_ref[...] = kn_ref[...]
        vs_ref[...] = v_ref[...]


def qk_norm(proj, qn_g, kn_g, *, n_prompt_blocks):
    m = proj.shape[0]
    mp = n_prompt_blocks * TB
    assert m == mp + TB
    qg = jnp.tile(qn_g, Q_WIDTH // QK_DIM).reshape(1, Q_WIDTH)
    kg = jnp.tile(kn_g, Q_WIDTH // QK_DIM).reshape(1, Q_WIDTH)
    col = lambda c: pl.BlockSpec((TB, Q_WIDTH), lambda i: (i, c))
    gain = pl.BlockSpec((1, Q_WIDTH), lambda i: (0, 0))
    prompt = pl.BlockSpec((TB, Q_WIDTH), lambda i: (jnp.minimum(i, n_prompt_blocks - 1), 0))
    sample = pl.BlockSpec((TB, Q_WIDTH), lambda i: (0, 0))
    half = lambda rows, dt: jax.ShapeDtypeStruct((rows, Q_WIDTH), dt)
    return pl.pallas_call(
        functools.partial(_qknorm_body, n_prompt_blocks=n_prompt_blocks),
        grid=(m // TB,),
        in_specs=[col(COL_Q), col(COL_K), col(COL_V), gain, gain],
        out_specs=[col(0), col(0), col(0), prompt, sample, prompt, sample],
        out_shape=[half(m, bf16), half(m, bf16), half(m, bf16), half(mp, f32), half(TB, f32), half(mp, f32),
                   half(TB, f32)],
        scratch_shapes=[pltpu.VMEM((TB, Q_WIDTH), f32)],
        compiler_params=_cparams(("arbitrary",)),
        name="qk_norm",
    )(proj, proj, proj, qg, kg)


def _bucket(rel):
    half = N_BUCKETS // 2
    max_exact = half // 2
    ret = jnp.where(rel > 0, half, 0)
    n = jnp.abs(rel)
    nf = jnp.maximum(n, 1).astype(f32)
    large = max_exact + (jnp.log(nf / max_exact) / math.log(MAX_DISTANCE / max_exact)
                         * (half - max_exact)).astype(jnp.int32)
    large = jnp.minimum(large, half - 1)
    return ret + jnp.where(n < max_exact, n, large)


def _bias_from_bucket(bucket, rb_ref, h):
    out = jnp.zeros(bucket.shape, f32)
    for j in range(N_BUCKETS):
        out = jnp.where(bucket == j, rb_ref[j, h], out)
    return out


def _block_time(r):
    return (r % NSUB) * SUBLEN + r // NSUB


def _bias_body(rb_ref, pb_ref, sbc_ref, sbn_ref, *, past):
    h = pl.program_id(0)
    tq = _block_time(lax.broadcasted_iota(jnp.int32, (TB, TB), 0))
    tk = _block_time(lax.broadcasted_iota(jnp.int32, (TB, TB), 1))
    b0 = _bias_from_bucket(_bucket(tk - tq), rb_ref, h)
    pb_ref[0] = jnp.where(tk // CHUNK <= tq // CHUNK, b0, MASK_NEG)
    pb_ref[1] = _bias_from_bucket(_bucket(tk - tq - TB), rb_ref, h)
    pb_ref[2] = _bias_from_bucket(_bucket(tk - tq - 2 * TB), rb_ref, h)
    nq = sbc_ref.shape[0]
    qpos = past + lax.broadcasted_iota(jnp.int32, (nq, past), 0)
    kpos = lax.broadcasted_iota(jnp.int32, (nq, past), 1)
    sbc_ref[...] = _bias_from_bucket(_bucket(kpos - qpos), rb_ref, h)
    qn = lax.broadcasted_iota(jnp.int32, (nq, nq), 0)
    kn = lax.broadcasted_iota(jnp.int32, (nq, nq), 1)
    sbn_ref[...] = _bias_from_bucket(_bucket(kn - qn), rb_ref, h)


def bias_tiles(rel_bias, past):
    return pl.pallas_call(
        functools.partial(_bias_body, past=past),
        grid=(N_HEADS,),
        in_specs=[pl.BlockSpec(memory_space=pltpu.SMEM)],
        out_specs=[
            pl.BlockSpec((None, 3, TB, TB), lambda h: (h, 0, 0, 0)),
            pl.BlockSpec((None, SUBLEN, past), lambda h: (h, 0, 0)),
            pl.BlockSpec((None, SUBLEN, SUBLEN), lambda h: (h, 0, 0)),
        ],
        out_shape=[
            jax.ShapeDtypeStruct((N_HEADS, 3, TB, TB), f32),
            jax.ShapeDtypeStruct((N_HEADS, SUBLEN, past), f32),
            jax.ShapeDtypeStruct((N_HEADS, SUBLEN, SUBLEN), f32),
        ],
        compiler_params=_cparams(("parallel",)),
        name="bias_tiles",
    )(rel_bias)


def _stack_maps(q):
    lane = lax.broadcasted_iota(jnp.int32, q.shape, 1)
    zero = jnp.zeros_like(q)
    return jnp.concatenate([jnp.where(lane < QK_DIM, q, zero), jnp.where(lane >= QK_DIM, q, zero)], axis=0)


def _online_softmax_step(s, v, m_prev, l_prev, acc_prev):
    m_new = jnp.maximum(m_prev, jnp.max(s, axis=-1, keepdims=True))
    alpha = jnp.exp(m_prev - m_new)
    p = jnp.exp(s - jnp.tile(m_new, (1, s.shape[1] // LANES)))
    l_new = alpha * l_prev + jnp.sum(p, axis=-1, keepdims=True)
    acc_new = alpha * acc_prev + jnp.dot(p.astype(bf16), v, preferred_element_type=f32)
    return m_new, l_new, acc_new


def _diff_combine(l, acc, lam, g, nq, lam_init):
    o = acc[:nq] / l[:nq] - lam * (acc[nq:] / l[nq:])
    ms = jnp.mean(o * o, axis=-1, keepdims=True)
    return o * lax.rsqrt(ms + EPS) * g * (1.0 - lam_init)


def _prompt_attn_body(lam_ref, q_ref, k_ref, v_ref, pb_ref, g_ref, o_ref, m_ref, l_ref, acc_ref, *, lam_init):
    qi = pl.program_id(1)
    kj = pl.program_id(2)

    @pl.when(kj == 0)
    def _():
        m_ref[...] = jnp.full(m_ref.shape, -jnp.inf, f32)
        l_ref[...] = jnp.zeros(l_ref.shape, f32)
        acc_ref[...] = jnp.zeros(acc_ref.shape, f32)

    @pl.when(kj <= qi)
    def _():
        for h in range(N_HEADS):
            sl = slice(h * V_DIM, (h + 1) * V_DIM)
            qq = _stack_maps(q_ref[:, sl])
            s = _dot_nt(qq, k_ref[:, sl])
            s = (s.reshape(2, TB, TB) + pb_ref[h][None]).reshape(2 * TB, TB)
            m, l, acc = _online_softmax_step(s, v_ref[:, sl], m_ref[h], l_ref[h], acc_ref[h])
            m_ref[h] = m
            l_ref[h] = l
            acc_ref[h] = acc

    @pl.when(kj == pl.num_programs(2) - 1)
    def _():
        for h in range(N_HEADS):
            o = _diff_combine(l_ref[h], acc_ref[h], lam_ref[0], g_ref[...], TB, lam_init)
            o_ref[:, h * V_DIM:(h + 1) * V_DIM] = o.astype(o_ref.dtype)


def prompt_attention(qn, kb, vb, pbias, lam, subln_g, *, n_seq, blocks_per_seq, lam_init):
    nb = blocks_per_seq
    m = n_seq * nb * TB
    kv = pl.BlockSpec((TB, ATT_WIDTH), lambda b, i, j: (b * nb + jnp.minimum(i, j), 0))
    return pl.pallas_call(
        functools.partial(_prompt_attn_body, lam_init=lam_init),
        grid=(n_seq, nb, nb),
        in_specs=[
            pl.BlockSpec(memory_space=pltpu.SMEM),
            pl.BlockSpec((TB, Q_WIDTH), lambda b, i, j: (b * nb + i, 0)),
            kv, kv,
            pl.BlockSpec((N_HEADS, None, TB, TB), lambda b, i, j: (0, jnp.clip(i - j, 0, 2), 0, 0)),
            pl.BlockSpec((1, V_DIM), lambda b, i, j: (0, 0)),
        ],
        out_specs=pl.BlockSpec((TB, ATT_WIDTH), lambda b, i, j: (b * nb + i, 0)),
        out_shape=jax.ShapeDtypeStruct((m, ATT_WIDTH), bf16),
        scratch_shapes=[pltpu.VMEM((N_HEADS, 2 * TB, LANES), f32), pltpu.VMEM((N_HEADS, 2 * TB, LANES), f32),
                        pltpu.VMEM((N_HEADS, 2 * TB, V_DIM), f32)],
        compiler_params=_cparams(("parallel", "parallel", "arbitrary")),
        name="prompt_attention",
    )(lam, qn, kb, vb, pbias, subln_g.reshape(1, V_DIM))


def _sample_attn_body(lam_ref, q_ref, kn_ref, vn_ref, ck_ref, cv_ref, sbc_ref, sbn_ref, g_ref, o_ref,
                      m_ref, l_ref, acc_ref, *, lam_init):
    kj = pl.program_id(1)
    nq = q_ref.shape[0]
    for h in range(N_HEADS):
        sl = slice(h * V_DIM, (h + 1) * V_DIM)
        qq = _stack_maps(q_ref[:, sl])

        @pl.when(kj == 0)
        def _():
            s = _dot_nt(qq, kn_ref[:, sl].astype(bf16))
            s = (s.reshape(2, nq, nq) + sbn_ref[h][None]).reshape(2 * nq, nq)
            m = jnp.max(s, axis=-1, keepdims=True)
            p = jnp.exp(s - m)
            m_ref[h] = jnp.broadcast_to(m, m_ref.shape[1:])
            l_ref[h] = jnp.broadcast_to(jnp.sum(p, axis=-1, keepdims=True), l_ref.shape[1:])
            acc_ref[h] = jnp.dot(p.astype(bf16), vn_ref[:, sl].astype(bf16), preferred_element_type=f32)

        ck = ck_ref.shape[0] // N_HEADS
        head_rows = pl.ds(h, ck, stride=N_HEADS)
        s = _dot_nt(qq, ck_ref[head_rows, :].astype(bf16))
        s = (s.reshape(2, nq, ck) + sbc_ref[h][None]).reshape(2 * nq, ck)
        m, l, acc = _online_softmax_step(s, cv_ref[head_rows, :].astype(bf16), m_ref[h], l_ref[h], acc_ref[h])
        m_ref[h] = m
        l_ref[h] = l
        acc_ref[h] = acc

        @pl.when(kj == pl.num_programs(1) - 1)
        def _():
            o = _diff_combine(l_ref[h], acc_ref[h], lam_ref[0], g_ref[...], nq, lam_init)
            o_ref[:, sl] = o.astype(o_ref.dtype)


def sample_attention(q_s, kn_s, v_s, cache_k, cache_v, sbc, sbn, lam, subln_g, *, layer, lam_init):
    nb, nq, _ = q_s.shape
    past = cache_k.shape[2] // N_HEADS
    chunk = min(past, 1024)
    return pl.pallas_call(
        functools.partial(_sample_attn_body, lam_init=lam_init),
        grid=(nb, past // chunk),
        in_specs=[
            pl.BlockSpec(memory_space=pltpu.SMEM),
            pl.BlockSpec((None, nq, ATT_WIDTH), lambda b, j: (b, 0, 0)),
            pl.BlockSpec((None, nq, ATT_WIDTH), lambda b, j: (b, 0, 0)),
            pl.BlockSpec((None, nq, ATT_WIDTH), lambda b, j: (b, 0, 0)),
            pl.BlockSpec((None, None, chunk * N_HEADS, V_DIM), lambda b, j: (layer, b, j, 0)),
            pl.BlockSpec((None, None, chunk * N_HEADS, V_DIM), lambda b, j: (layer, b, j, 0)),
            pl.BlockSpec((N_HEADS, nq, chunk), lambda b, j: (0, 0, j)),
            pl.BlockSpec((N_HEADS, nq, nq), lambda b, j: (0, 0, 0)),
            pl.BlockSpec((1, V_DIM), lambda b, j: (0, 0)),
        ],
        out_specs=pl.BlockSpec((None, nq, ATT_WIDTH), lambda b, j: (b, 0, 0)),
        out_shape=jax.ShapeDtypeStruct((nb, nq, ATT_WIDTH), bf16),
        scratch_shapes=[pltpu.VMEM((N_HEADS, 2 * nq, LANES), f32), pltpu.VMEM((N_HEADS, 2 * nq, LANES), f32),
                        pltpu.VMEM((N_HEADS, 2 * nq, V_DIM), f32)],
        compiler_params=_cparams(("parallel", "arbitrary")),
        name="sample_attention",
    )(lam, q_s, kn_s, v_s, cache_k, cache_v, sbc, sbn, subln_g.reshape(1, V_DIM))


def _chain_carries(c0, mult, add):
    rows = [c0]
    c = c0
    for s in range(NSUB):
        c = mult[s:s + 1] * c + add[s:s + 1]
        if s + 1 < NSUB:
            rows.append(c)
    return jnp.concatenate(rows, axis=0), c


def _rglru_body(xrg_ref, xgate_ref, head_s_ref, h0_s_ref, cw_ref, cb_ref, wa_ref, wx_ref, ba_ref, bx_ref,
                lam_ref, hg_ref, htail_ref, ctail_ref, xbuf_ref, a_ref, b_ref, hcar_ref, ptail_ref,
                *, blocks_per_seq, n_prompt_blocks):
    g = pl.program_id(0)
    is_sample = g >= n_prompt_blocks
    hl = (CONV_W - 1) * NSUB

    @pl.when(g % blocks_per_seq == 0)
    def _():
        hcar_ref[...] = jnp.zeros(hcar_ref.shape, f32)
        ptail_ref[...] = jnp.zeros(ptail_ref.shape, f32)

    x = xrg_ref[...]
    tail = x[TB - hl:]
    sub = lax.broadcasted_iota(jnp.int32, (NSUB, RG_WIDTH), 0)
    for k in range(CONV_W - 1):
        rs = slice(k * NSUB, (k + 1) * NSUB)
        prompt_head = jnp.where(sub == 0, pltpu.roll(ptail_ref[rs, :], 1, 0), pltpu.roll(tail[rs], 1, 0))
        xbuf_ref[rs, :] = jnp.where(is_sample, head_s_ref[rs, :], prompt_head)
    xbuf_ref[hl:, :] = x
    ptail_ref[...] = tail
    ctail_ref[...] = tail

    xc = cb_ref[...] + cw_ref[CONV_W - 1:CONV_W, :] * x
    for j in range(1, CONV_W):
        xc = xc + cw_ref[CONV_W - 1 - j:CONV_W - j, :] * xbuf_ref[hl - j * NSUB:hl - j * NSUB + TB, :]

    xcb = xc.astype(bf16)
    r_parts, i_parts = [], []
    for blk in range(RG_BLOCKS):
        sl = slice(blk * RG_BLOCK, (blk + 1) * RG_BLOCK)
        r_parts.append(jnp.dot(xcb[:, sl], wa_ref[blk], preferred_element_type=f32))
        i_parts.append(jnp.dot(xcb[:, sl], wx_ref[blk], preferred_element_type=f32))
    r = jax.nn.sigmoid(jnp.concatenate(r_parts, axis=1) + ba_ref[...])
    ig = jax.nn.sigmoid(jnp.concatenate(i_parts, axis=1) + bx_ref[...])
    nl = -lam_ref[...]
    softplus = jnp.maximum(nl, 0.0) + jnp.log1p(jnp.exp(-jnp.abs(nl)))
    a = jnp.exp(-RG_C * r * softplus)
    a_ref[...] = a
    b_ref[...] = jnp.sqrt(1.0 - a * a) * (ig * xc)

    def rows(i):
        return pl.ds(pl.multiple_of(i * NSUB, NSUB), NSUB)

    def local(i, carry):
        p, e = carry
        ai = a_ref[rows(i), :]
        return p * ai, ai * e + b_ref[rows(i), :]

    p, e = lax.fori_loop(0, SUBLEN, local,
                         (jnp.ones((NSUB, RG_WIDTH), f32), jnp.zeros((NSUB, RG_WIDTH), f32)), unroll=4)
    entry, exit_state = _chain_carries(hcar_ref[...], p, e)
    hcar_ref[...] = exit_state
    h_init = jnp.where(is_sample, h0_s_ref[...], entry)

    def step(i, hprev):
        hcur = a_ref[rows(i), :] * hprev + b_ref[rows(i), :]
        b_ref[rows(i), :] = hcur
        return hcur

    h_last = lax.fori_loop(0, SUBLEN, step, h_init, unroll=4)
    htail_ref[...] = h_last
    hg_ref[...] = (b_ref[...] * _gelu(xgate_ref[...])).astype(hg_ref.dtype)


def rglru_mixer(proj, head_s, h0_s, conv_w, conv_b, wa, wx, ba, bx, lam, *, blocks_per_seq, n_prompt_blocks):
    m = proj.shape[0]
    nblk = m // TB
    hl = (CONV_W - 1) * NSUB
    row = lambda v: v.reshape(1, RG_WIDTH)
    full = lambda shape: pl.BlockSpec(shape, lambda g: (0,) * len(shape))
    return pl.pallas_call(
        functools.partial(_rglru_body, blocks_per_seq=blocks_per_seq, n_prompt_blocks=n_prompt_blocks),
        grid=(nblk,),
        in_specs=[
            pl.BlockSpec((TB, RG_WIDTH), lambda g: (g, COL_RG)),
            pl.BlockSpec((TB, RG_WIDTH), lambda g: (g, COL_GATE)),
            full((hl, RG_WIDTH)),
            full((NSUB, RG_WIDTH)),
            full((CONV_W, RG_WIDTH)),
            full((1, RG_WIDTH)),
            full((RG_BLOCKS, RG_BLOCK, RG_BLOCK)),
            full((RG_BLOCKS, RG_BLOCK, RG_BLOCK)),
            full((1, RG_WIDTH)),
            full((1, RG_WIDTH)),
            full((1, RG_WIDTH)),
        ],
        out_specs=[
            pl.BlockSpec((TB, RG_WIDTH), lambda g: (g, 0)),
            pl.BlockSpec((None, NSUB, RG_WIDTH), lambda g: (g, 0, 0)),
            pl.BlockSpec((None, hl, RG_WIDTH), lambda g: (g, 0, 0)),
        ],
        out_shape=[
            jax.ShapeDtypeStruct((m, RG_WIDTH), bf16),
            jax.ShapeDtypeStruct((nblk, NSUB, RG_WIDTH), f32),
            jax.ShapeDtypeStruct((nblk, hl, RG_WIDTH), f32),
        ],
        scratch_shapes=[
            pltpu.VMEM((hl + TB, RG_WIDTH), f32),
            pltpu.VMEM((TB, RG_WIDTH), f32),
            pltpu.VMEM((TB, RG_WIDTH), f32),
            pltpu.VMEM((1, RG_WIDTH), f32),
            pltpu.VMEM((hl, RG_WIDTH), f32),
        ],
        compiler_params=_cparams(("arbitrary",)),
        name="rglru_mixer",
    )(proj, proj, head_s, h0_s, conv_w, row(conv_b), wa.astype(bf16), wx.astype(bf16), row(ba), row(bx),
      row(lam))


S5_CHUNKS = 8
S5_CH_IN = S5_WIDTH // S5_CHUNKS
S5_CH_ST = S5_LANES // S5_CHUNKS
S5_SCAN_LANES = 512


def _cmul(ar, ai, br, bi):
    return ar * br - ai * bi, ar * bi + ai * br


def _s5_body(u_ref, s0r_ref, s0i_ref, abr_ref, abi_ref, bdr_ref, bdi_ref, cdr_ref, cdi_ref, d_ref, gw_ref,
             gb_ref, gs_ref, str_ref, sti_ref, xr_ref, xi_ref, car_ref, cai_ref,
             *, blocks_per_seq, n_prompt_blocks):
    g = pl.program_id(0)
    is_sample = g >= n_prompt_blocks

    @pl.when(g % blocks_per_seq == 0)
    def _():
        car_ref[...] = jnp.zeros(car_ref.shape, f32)
        cai_ref[...] = jnp.zeros(cai_ref.shape, f32)

    u = u_ref[...]
    ub = u.astype(bf16)
    for c in range(S5_CHUNKS):
        ci = slice(c * S5_CH_IN, (c + 1) * S5_CH_IN)
        cs = slice(c * S5_CH_ST, (c + 1) * S5_CH_ST)
        xr_ref[:, cs] = jnp.dot(ub[:, ci], bdr_ref[c], preferred_element_type=f32)
        xi_ref[:, cs] = jnp.dot(ub[:, ci], bdi_ref[c], preferred_element_type=f32)

    def rows(i):
        return pl.ds(pl.multiple_of(i * NSUB, NSUB), NSUB)

    for c in range(S5_LANES // S5_SCAN_LANES):
        ls = slice(c * S5_SCAN_LANES, (c + 1) * S5_SCAN_LANES)
        ar1 = abr_ref[:, ls]
        ai1 = abi_ref[:, ls]
        ar = jnp.broadcast_to(ar1, (NSUB, S5_SCAN_LANES))
        ai = jnp.broadcast_to(ai1, (NSUB, S5_SCAN_LANES))

        def local(i, carry, ls=ls, ar=ar, ai=ai):
            er, ei = carry
            nr, ni = _cmul(ar, ai, er, ei)
            return nr + xr_ref[rows(i), ls], ni + xi_ref[rows(i), ls]

        zero = jnp.zeros((NSUB, S5_SCAN_LANES), f32)
        er, ei = lax.fori_loop(0, SUBLEN, local, (zero, zero), unroll=4)
        pr, pi = ar1, ai1
        for _ in range(int(math.log2(SUBLEN))):
            pr, pi = _cmul(pr, pi, pr, pi)
        rows_r, rows_i = [car_ref[:, ls]], [cai_ref[:, ls]]
        cr, ci_ = rows_r[0], rows_i[0]
        for s in range(NSUB):
            mr, mi = _cmul(pr, pi, cr, ci_)
            cr, ci_ = mr + er[s:s + 1], mi + ei[s:s + 1]
            if s + 1 < NSUB:
                rows_r.append(cr)
                rows_i.append(ci_)
        car_ref[:, ls] = cr
        cai_ref[:, ls] = ci_
        init_r = jnp.where(is_sample, s0r_ref[:, ls], jnp.concatenate(rows_r, axis=0))
        init_i = jnp.where(is_sample, s0i_ref[:, ls], jnp.concatenate(rows_i, axis=0))

        def step(i, carry, ls=ls, ar=ar, ai=ai):
            sr, si = carry
            nr, ni = _cmul(ar, ai, sr, si)
            nr = nr + xr_ref[rows(i), ls]
            ni = ni + xi_ref[rows(i), ls]
            xr_ref[rows(i), ls] = nr
            xi_ref[rows(i), ls] = ni
            return nr, ni

        sr, si = lax.fori_loop(0, SUBLEN, step, (init_r, init_i), unroll=4)
        str_ref[:, ls] = sr
        sti_ref[:, ls] = si

    y_parts = []
    for c in range(S5_CHUNKS):
        cs = slice(c * S5_CH_ST, (c + 1) * S5_CH_ST)
        y_parts.append(jnp.dot(xr_ref[:, cs].astype(bf16), cdr_ref[c], preferred_element_type=f32)
                       + jnp.dot(xi_ref[:, cs].astype(bf16), cdi_ref[c], preferred_element_type=f32))
    y = jnp.concatenate(y_parts, axis=1) + d_ref[...] * u
    gy = _gelu(y)
    z = jnp.dot(gy.astype(bf16), gw_ref[...], preferred_element_type=f32) + gb_ref[...]
    gs_ref[...] = (gy * jax.nn.sigmoid(z)).astype(gs_ref.dtype)


def _s5_discretise(lam_re, lam_im, log_step, b_re, b_im, c_re, c_im):
    step = jnp.exp(log_step)[:, None]
    mag = jnp.exp(lam_re * step)
    ab_re, ab_im = mag * jnp.cos(lam_im * step), mag * jnp.sin(lam_im * step)
    den = lam_re * lam_re + lam_im * lam_im
    nr, ni = ab_re - 1.0, ab_im
    co_re = (nr * lam_re + ni * lam_im) / den
    co_im = (ni * lam_re - nr * lam_im) / den
    bb_re = co_re[..., None] * b_re - co_im[..., None] * b_im
    bb_im = co_re[..., None] * b_im + co_im[..., None] * b_re
    gpc = S5_GROUPS // S5_CHUNKS
    eye = jnp.eye(gpc, dtype=f32)

    def in_chunks(bb):
        t = bb.reshape(S5_CHUNKS, gpc, S5_STATE, S5_GROUP)
        return jnp.einsum('cgph,gk->cghkp', t, eye).reshape(S5_CHUNKS, S5_CH_IN, S5_CH_ST).astype(bf16)

    def out_chunks(cc):
        t = cc.reshape(S5_CHUNKS, gpc, S5_GROUP, S5_STATE)
        return jnp.einsum('cghp,gk->cgpkh', t, eye).reshape(S5_CHUNKS, S5_CH_ST, S5_CH_IN).astype(bf16)

    return (ab_re.reshape(1, S5_LANES), ab_im.reshape(1, S5_LANES), in_chunks(bb_re), in_chunks(bb_im),
            out_chunks(c_re), out_chunks(-c_im))


def s5_mixer(proj, s0_re, s0_im, disc, d, glu_w, glu_b, *, layer, blocks_per_seq, n_prompt_blocks):
    m = proj.shape[0]
    nblk = m // TB
    ab_re, ab_im, bd_re, bd_im, cd_re, cd_im = disc
    full = lambda shape: pl.BlockSpec(shape, lambda g: (0,) * len(shape))
    return pl.pallas_call(
        functools.partial(_s5_body, blocks_per_seq=blocks_per_seq, n_prompt_blocks=n_prompt_blocks),
        grid=(nblk,),
        in_specs=[
            pl.BlockSpec((TB, S5_WIDTH), lambda g: (g, COL_S5)),
            full((NSUB, S5_LANES)),
            full((NSUB, S5_LANES)),
            full((1, S5_LANES)),
            full((1, S5_LANES)),
            full((S5_CHUNKS, S5_CH_IN, S5_CH_ST)),
            full((S5_CHUNKS, S5_CH_IN, S5_CH_ST)),
            full((S5_CHUNKS, S5_CH_ST, S5_CH_IN)),
            full((S5_CHUNKS, S5_CH_ST, S5_CH_IN)),
            full((1, S5_WIDTH)),
            pl.BlockSpec((None, S5_WIDTH, S5_WIDTH), lambda g: (layer, 0, 0)),
            full((1, S5_WIDTH)),
        ],
        out_specs=[
            pl.BlockSpec((TB, S5_WIDTH), lambda g: (g, 0)),
            pl.BlockSpec((None, NSUB, S5_LANES), lambda g: (g, 0, 0)),
            pl.BlockSpec((None, NSUB, S5_LANES), lambda g: (g, 0, 0)),
        ],
        out_shape=[
            jax.ShapeDtypeStruct((m, S5_WIDTH), bf16),
            jax.ShapeDtypeStruct((nblk, NSUB, S5_LANES), f32),
            jax.ShapeDtypeStruct((nblk, NSUB, S5_LANES), f32),
        ],
        scratch_shapes=[
            pltpu.VMEM((TB, S5_LANES), f32),
            pltpu.VMEM((TB, S5_LANES), f32),
            pltpu.VMEM((1, S5_LANES), f32),
            pltpu.VMEM((1, S5_LANES), f32),
        ],
        compiler_params=_cparams(("arbitrary",)),
        name="s5_mixer",
    )(proj, s0_re, s0_im, ab_re, ab_im, bd_re, bd_im, cd_re, cd_im, d.reshape(1, S5_WIDTH),
      glu_w, glu_b.reshape(1, S5_WIDTH))


def _merge_body(att_ref, rg_ref, s5_ref, wa_ref, wr_ref, ws_ref, g0_ref, g1_ref, g2_ref, b0_ref, b1_ref,
                b2_ref, o_ref):
    acc = jax.nn.sigmoid(g0_ref[...] + b0_ref[...]) * jnp.dot(att_ref[...], wa_ref[...],
                                                               preferred_element_type=f32)
    acc = acc + jax.nn.sigmoid(g1_ref[...] + b1_ref[...]) * jnp.dot(rg_ref[...], wr_ref[...],
                                                                     preferred_element_type=f32)
    acc = acc + jax.nn.sigmoid(g2_ref[...] + b2_ref[...]) * jnp.dot(s5_ref[...], ws_ref[...],
                                                                     preferred_element_type=f32)
    o_ref[...] = acc.astype(o_ref.dtype)


def gated_merge(o_att, hg, gs, w_att_o, w_rg_o, w_s5_o, proj, b_gate, *, layer, tm, tn):
    m = o_att.shape[0]
    k = o_att.shape[1]
    gcol = COL_GATES // tn
    bcol = D_MODEL // tn
    act = pl.BlockSpec((tm, k), lambda i, j: (i, 0))
    wsp = pl.BlockSpec((None, k, tn), lambda i, j: (layer, 0, j))
    gate = lambda br: pl.BlockSpec((tm, tn), lambda i, j: (i, gcol + br * bcol + j))
    bias = lambda br: pl.BlockSpec((1, tn), lambda i, j: (0, br * bcol + j))
    bg = b_gate.reshape(1, N_BRANCH * D_MODEL)
    return pl.pallas_call(
        _merge_body,
        grid=(m // tm, D_MODEL // tn),
        in_specs=[act, act, act, wsp, wsp, wsp, gate(0), gate(1), gate(2), bias(0), bias(1), bias(2)],
        out_specs=pl.BlockSpec((tm, tn), lambda i, j: (i, j)),
        out_shape=jax.ShapeDtypeStruct((m, D_MODEL), bf16),
        compiler_params=_cparams(("parallel", "arbitrary")),
        name="gated_merge",
    )(o_att, hg, gs, w_att_o, w_rg_o, w_s5_o, proj, proj, proj, bg, bg, bg)


def _residual_matmul_body(x_ref, a_ref, w_ref, o_ref):
    o_ref[...] = x_ref[...] + jnp.dot(a_ref[...], w_ref[...], preferred_element_type=f32)


def residual_matmul(x, a, w, *, layer, tm, tn):
    m, n = x.shape
    k = a.shape[1]
    return pl.pallas_call(
        _residual_matmul_body,
        grid=(m // tm, n // tn),
        in_specs=[
            pl.BlockSpec((tm, tn), lambda i, j: (i, j)),
            pl.BlockSpec((tm, k), lambda i, j: (i, 0)),
            pl.BlockSpec((None, k, tn), lambda i, j: (layer, 0, j)),
        ],
        out_specs=pl.BlockSpec((tm, tn), lambda i, j: (i, j)),
        out_shape=jax.ShapeDtypeStruct((m, n), f32),
        compiler_params=_cparams(("parallel", "arbitrary")),
        name="residual_matmul",
    )(x, a, w)


PEER_TT = 256
PEER_ET = 1024
PEER_I1 = PEER_ET // N_KEYS
PEER_PIECE = 32
PEER_RG = 2


def _top_values(s, k):
    out = []
    for _ in range(k):
        mx = jnp.max(s, axis=0, keepdims=True)
        out.append(mx)
        s = jnp.where(s == mx, -jnp.inf, s)
    return out


def _next_float(x, up):
    bits = pltpu.bitcast(x, jnp.int32)
    step = jnp.where((x > 0) == up, 1, -1)
    return pltpu.bitcast(bits + step, f32)


def _two_sum(a, b):
    s = a + b
    bb = s - a
    return s, (a - (s - bb)) + (b - bb)


def _sum_threshold(theta, a):
    x, e = _two_sum(jnp.broadcast_to(theta, a.shape), -a)
    half_gap = (theta - _next_float(theta, False)) * 0.5
    y, err = _two_sum(x, e - half_gap)
    y = jnp.where(err > 0, _next_float(y, True), y)
    return jnp.where(a + y >= theta, y, _next_float(y, True))


def _peer_route_body(q_ref, sk_ref, s2_ref, e2_ref, thr_ref, c1_ref):
    for h in range(PEER_HEADS):
        s1 = _dot_nt(sk_ref[0], q_ref[:, (2 * h) * N_KEYS:(2 * h + 1) * N_KEYS])
        s2 = _dot_nt(sk_ref[1], q_ref[:, (2 * h + 1) * N_KEYS:(2 * h + 2) * N_KEYS])
        v1 = _top_values(s1, PEER_TOPK)
        v2 = _top_values(s2, PEER_TOPK)
        cand = [v1[k1] + v2[k2] for k1 in range(PEER_TOPK) for k2 in range(PEER_TOPK // (k1 + 1))]
        pad = (-len(cand)) % SUBLANES
        cand = jnp.concatenate(cand + [jnp.full_like(cand[0], -jnp.inf)] * pad, axis=0)
        top = _top_values(cand, PEER_TOPK)
        theta = top[-1]
        z = top[0] * 0.0
        for t in top:
            z = z + jnp.exp(t - top[0])
        s2_ref[h] = s2
        e2_ref[h] = jnp.exp(s2 - v2[0])
        thr_ref[h] = _sum_threshold(theta, s1)
        c1_ref[h] = jnp.exp(s1 - v1[0]) / z


def peer_route(q, subkeys):
    m = q.shape[0]
    shp = jax.ShapeDtypeStruct((PEER_HEADS, N_KEYS, m), f32)
    spec = pl.BlockSpec((PEER_HEADS, N_KEYS, PEER_TT), lambda i: (0, 0, i))
    return pl.pallas_call(
        _peer_route_body,
        grid=(m // PEER_TT,),
        in_specs=[
            pl.BlockSpec((PEER_TT, q.shape[1]), lambda i: (i, 0)),
            pl.BlockSpec((2, N_KEYS, N_KEYS), lambda i: (0, 0, 0)),
        ],
        out_specs=[spec] * 4,
        out_shape=[shp] * 4,
        compiler_params=_cparams(("parallel",)),
        name="peer_route",
    )(q, subkeys)


def _peer_gate_weights(s2_ref, e2_ref, thr_ref, c1_ref, w_ref, token_tiles):
    n_pc = N_KEYS // PEER_PIECE
    for tc in token_tiles:
        ts = slice(tc * LANES, (tc + 1) * LANES)
        for r0 in range(0, PEER_I1, PEER_RG):
            w = [[jnp.zeros((PEER_PIECE, LANES), f32) for _ in range(n_pc)] for _ in range(PEER_RG)]
            for h in range(PEER_HEADS):
                thr = [thr_ref[h, r0 + r:r0 + r + 1, ts] for r in range(PEER_RG)]
                c1 = [c1_ref[h, r0 + r:r0 + r + 1, ts] for r in range(PEER_RG)]
                for pc in range(n_pc):
                    ks = slice(pc * PEER_PIECE, (pc + 1) * PEER_PIECE)
                    s2 = s2_ref[h, ks, ts]
                    e2 = e2_ref[h, ks, ts]
                    for r in range(PEER_RG):
                        w[r][pc] = w[r][pc] + jnp.where(s2 >= thr[r], e2 * c1[r], 0.0)
            for r in range(PEER_RG):
                for pc in range(n_pc):
                    base = (r0 + r) * N_KEYS + pc * PEER_PIECE
                    w_ref[base:base + PEER_PIECE, ts] = w[r][pc]


def _peer_expert_body(xnt_ref, u_ref, vt_ref, s2_ref, e2_ref, thr0_ref, c10_ref, thrn_ref, c1n_ref, o_ref,
                      act_ref, w_ref):
    j = pl.program_id(1)
    n_tiles = o_ref.shape[1] // LANES
    cur = w_ref.at[j % 2]
    nxt = w_ref.at[(j + 1) % 2]

    @pl.when(j == 0)
    def _():
        o_ref[...] = jnp.zeros(o_ref.shape, f32)
        _peer_gate_weights(s2_ref, e2_ref, thr0_ref, c10_ref, cur, range(n_tiles))

    zt = jnp.dot(u_ref[...], xnt_ref[...], preferred_element_type=f32)
    _peer_gate_weights(s2_ref, e2_ref, thrn_ref, c1n_ref, nxt, range(n_tiles // 2))
    act_ref[...] = (_gelu(zt) * cur[...]).astype(bf16)
    o_ref[...] += jnp.dot(vt_ref[...], act_ref[...], preferred_element_type=f32)
    _peer_gate_weights(s2_ref, e2_ref, thrn_ref, c1n_ref, nxt, range(n_tiles // 2, n_tiles))


def peer_experts(xnt, u_tab, vt_tab, s2, e2, thr, c1, *, layer, tt):
    d, m = xnt.shape
    thr = thr.reshape(PEER_HEADS, N_KEYS // PEER_I1, PEER_I1, m)
    c1 = c1.reshape(PEER_HEADS, N_KEYS // PEER_I1, PEER_I1, m)
    once = pl.Buffered(1)
    tok = pl.BlockSpec((PEER_HEADS, N_KEYS, tt), lambda i, j: (0, 0, i), pipeline_mode=once)
    n_steps = N_EXPERTS // PEER_ET
    row0 = pl.BlockSpec((PEER_HEADS, None, PEER_I1, tt), lambda i, j: (0, 0, 0, i))
    row_next = pl.BlockSpec((PEER_HEADS, None, PEER_I1, tt), lambda i, j: (0, jnp.minimum(j + 1, n_steps - 1), 0, i))
    return pl.pallas_call(
        _peer_expert_body,
        grid=(m // tt, n_steps),
        in_specs=[
            pl.BlockSpec((d, tt), lambda i, j: (0, i), pipeline_mode=once),
            pl.BlockSpec((None, PEER_ET, d), lambda i, j: (layer, j, 0)),
            pl.BlockSpec((None, d, PEER_ET), lambda i, j: (layer, 0, j)),
            tok, tok, row0, row0, row_next, row_next,
        ],
        out_specs=pl.BlockSpec((d, tt), lambda i, j: (0, i)),
        out_shape=jax.ShapeDtypeStruct((d, m), f32),
        scratch_shapes=[pltpu.VMEM((PEER_ET, tt), bf16), pltpu.VMEM((2, PEER_ET, tt), f32)],
        compiler_params=_cparams(("parallel", "arbitrary")),
        name="peer_experts",
    )(xnt, u_tab, vt_tab, s2, e2, thr, c1, thr, c1)


def _add_transposed_body(x_ref, dt_ref, o_ref):
    o_ref[...] = x_ref[...] + dt_ref[...].T


def _add_transposed_split_body(x_ref, dt_ref, op_ref, os_ref, *, n_prompt_blocks):
    y = x_ref[...] + dt_ref[...].T
    is_sample = pl.program_id(0) >= n_prompt_blocks

    @pl.when(jnp.logical_not(is_sample))
    def _():
        op_ref[...] = y

    @pl.when(is_sample)
    def _():
        os_ref[...] = y


def add_transposed_split(x, dt, *, n_prompt_blocks):
    m, d = x.shape
    mp = n_prompt_blocks * TB
    assert m == mp + TB
    return pl.pallas_call(
        functools.partial(_add_transposed_split_body, n_prompt_blocks=n_prompt_blocks),
        grid=(m // TB,),
        in_specs=[pl.BlockSpec((TB, d), lambda i: (i, 0)), pl.BlockSpec((d, TB), lambda i: (0, i))],
        out_specs=[pl.BlockSpec((TB, d), lambda i: (jnp.minimum(i, n_prompt_blocks - 1), 0)),
                   pl.BlockSpec((TB, d), lambda i: (0, 0))],
        out_shape=[jax.ShapeDtypeStruct((mp, d), f32), jax.ShapeDtypeStruct((TB, d), f32)],
        compiler_params=_cparams(("arbitrary",)),
        name="add_transposed_split",
    )(x, dt)


def add_transposed(x, dt, *, tm):
    m, d = x.shape
    return pl.pallas_call(
        _add_transposed_body,
        grid=(m // tm,),
        in_specs=[pl.BlockSpec((tm, d), lambda i: (i, 0)), pl.BlockSpec((d, tm), lambda i: (0, i))],
        out_specs=pl.BlockSpec((tm, d), lambda i: (i, 0)),
        out_shape=jax.ShapeDtypeStruct((m, d), f32),
        compiler_params=_cparams(("parallel",)),
        name="add_transposed",
    )(x, dt)


def _to_blocks(x):
    b, t, c = x.shape
    return x.reshape(b, t // TB, NSUB, SUBLEN, c).transpose(0, 1, 3, 2, 4).reshape(b * t, c)


def _from_blocks(y, b):
    m, c = y.shape
    t = m // b
    return y.reshape(b, t // TB, SUBLEN, NSUB, c).transpose(0, 1, 3, 2, 4).reshape(b, t, c)


def _layer(x, l, p, wb, tiles, dims, state, final):
    n_seq, blocks_per_seq, n_stream = dims
    n_prompt_blocks = n_seq * blocks_per_seq
    mp = n_prompt_blocks * TB
    pbias, sbc, sbn = tiles
    cache_k, cache_v, rg_h0, rg_conv0, s5_re0, s5_im0 = state
    tm = 3 * TB if x.shape[0] % (3 * TB) == 0 else TB
    tm_big = 1408 if x.shape[0] % 1408 == 0 else tm
    lam_init = 0.8 - 0.6 * math.exp(-0.3 * l)
    lam = (jnp.exp(jnp.sum(p['lam_q1'] * p['lam_k1'])) - jnp.exp(jnp.sum(p['lam_q2'] * p['lam_k2']))
           + lam_init).reshape(1)

    proj = norm_matmul(x, p['norm1_g'], wb['w_in'], layer=l, tm=tm_big, tn=1024, out_dtype=f32)
    qn, kb, vb, kn_p, kn_s, v_p, v_s = qk_norm(proj, p['qn_g'], p['kn_g'], n_prompt_blocks=n_prompt_blocks)

    o_att = prompt_attention(qn, kb, vb, pbias, lam, p['subln_g'], n_seq=n_seq,
                             blocks_per_seq=blocks_per_seq, lam_init=lam_init)
    stream = lambda a: a.reshape(SUBLEN, n_stream, -1).transpose(1, 0, 2)
    rows = lambda c: c.reshape(c.shape[0], c.shape[1], -1, V_DIM)
    o_s = sample_attention(stream(qn[mp:]), stream(kn_s), stream(v_s), rows(cache_k), rows(cache_v), sbc, sbn,
                           lam, p['subln_g'], layer=l, lam_init=lam_init)
    o_att = jnp.concatenate([o_att, o_s.transpose(1, 0, 2).reshape(TB, ATT_WIDTH)], axis=0)

    head_s = rg_conv0.transpose(1, 0, 2).reshape((CONV_W - 1) * NSUB, RG_WIDTH)
    hg, h_tail, c_tail = rglru_mixer(proj, head_s, rg_h0, p['rg_conv_w'], p['rg_conv_b'], p['rg_wa'],
                                     p['rg_wx'], p['rg_ba'], p['rg_bx'], p['rg_lambda'],
                                     blocks_per_seq=blocks_per_seq, n_prompt_blocks=n_prompt_blocks)
    disc = _s5_discretise(p['s5_lam_re'], p['s5_lam_im'], p['s5_log_step'], p['s5_b_re'], p['s5_b_im'],
                          p['s5_c_re'], p['s5_c_im'])
    gs, st_re, st_im = s5_mixer(proj, s5_re0.reshape(NSUB, S5_LANES), s5_im0.reshape(NSUB, S5_LANES), disc,
                                p['s5_d'], wb['s5_glu_w'], p['s5_glu_b'], layer=l,
                                blocks_per_seq=blocks_per_seq, n_prompt_blocks=n_prompt_blocks)

    merged = gated_merge(o_att, hg, gs, wb['w_att_o'], wb['w_rg_o'], wb['w_s5_o'], proj, p['b_gate'],
                         layer=l, tm=tm_big, tn=512)
    x = residual_matmul(x, merged, wb['w_out'], layer=l, tm=tm_big, tn=1024)

    q, xn2 = norm_matmul(x, p['norm2_g'], wb['peer_wq'], layer=l, tm=tm, tn=1024, out_dtype=bf16,
                         emit_xn=True)
    s2, e2, thr, c1 = peer_route(q, p['peer_subkeys'].astype(bf16))
    peer_t = peer_experts(xn2.T, wb['peer_u'], wb['peer_vt'], s2, e2, thr, c1, layer=l, tt=tm)
    if final:
        x = add_transposed_split(x, peer_t, n_prompt_blocks=n_prompt_blocks)
    else:
        x = add_transposed(x, peer_t, tm=TB)

    last = jnp.arange(n_seq) * blocks_per_seq + blocks_per_seq - 1
    hl = CONV_W - 1
    outs_p = (
        _from_blocks(kn_p, n_seq).reshape(n_seq, -1, N_HEADS, 2 * QK_DIM),
        _from_blocks(v_p, n_seq).reshape(n_seq, -1, N_HEADS, V_DIM),
        h_tail[last, NSUB - 1],
        c_tail[last].reshape(n_seq, hl, NSUB, RG_WIDTH)[:, :, NSUB - 1],
        st_re[last, NSUB - 1].reshape(n_seq, S5_GROUPS, S5_STATE),
        st_im[last, NSUB - 1].reshape(n_seq, S5_GROUPS, S5_STATE),
    )
    outs_s = (
        stream(kn_s).reshape(n_stream, SUBLEN, N_HEADS, 2 * QK_DIM),
        stream(v_s).reshape(n_stream, SUBLEN, N_HEADS, V_DIM),
        h_tail[n_prompt_blocks],
        c_tail[n_prompt_blocks].reshape(hl, NSUB, RG_WIDTH).transpose(1, 0, 2),
        st_re[n_prompt_blocks].reshape(n_stream, S5_GROUPS, S5_STATE),
        st_im[n_prompt_blocks].reshape(n_stream, S5_GROUPS, S5_STATE),
    )
    return x, outs_p, outs_s


def kernel(x_prompt, x_sample, cache_k, cache_v, state_rg_h, state_rg_conv, state_s5_re, state_s5_im, rel_bias, norm1_g, w_in, b_gate, qn_g, kn_g, lam_q1, lam_k1, lam_q2, lam_k2, subln_g, w_att_o, rg_conv_w, rg_conv_b, rg_wa, rg_ba, rg_wx, rg_bx, rg_lambda, w_rg_o, s5_lam_re, s5_lam_im, s5_log_step, s5_b_re, s5_b_im, s5_c_re, s5_c_im, s5_d, s5_glu_w, s5_glu_b, w_s5_o, w_out, norm2_g, peer_wq, peer_subkeys, peer_u, peer_v):
    n_seq, t_prompt, _ = x_prompt.shape
    n_stream, t_sample, _ = x_sample.shape
    depth = w_in.shape[0]
    assert n_stream == NSUB and t_sample == SUBLEN and t_prompt % TB == 0
    blocks_per_seq = t_prompt // TB
    dims = (n_seq, blocks_per_seq, n_stream)
    past = cache_k.shape[2]

    small = dict(norm1_g=norm1_g, b_gate=b_gate, qn_g=qn_g, kn_g=kn_g, lam_q1=lam_q1,
                 lam_k1=lam_k1, lam_q2=lam_q2, lam_k2=lam_k2, subln_g=subln_g,
                 rg_conv_w=rg_conv_w, rg_conv_b=rg_conv_b, rg_wa=rg_wa, rg_ba=rg_ba, rg_wx=rg_wx,
                 rg_bx=rg_bx, rg_lambda=rg_lambda, s5_lam_re=s5_lam_re,
                 s5_lam_im=s5_lam_im, s5_log_step=s5_log_step, s5_b_re=s5_b_re, s5_b_im=s5_b_im,
                 s5_c_re=s5_c_re, s5_c_im=s5_c_im, s5_d=s5_d, s5_glu_b=s5_glu_b,
                 norm2_g=norm2_g, peer_subkeys=peer_subkeys)
    wb = dict(w_in=w_in.astype(bf16), w_att_o=w_att_o.astype(bf16), w_rg_o=w_rg_o.astype(bf16),
              w_s5_o=w_s5_o.astype(bf16), w_out=w_out.astype(bf16), s5_glu_w=s5_glu_w.astype(bf16),
              peer_wq=peer_wq.astype(bf16), peer_u=peer_u.astype(bf16),
              peer_vt=jnp.swapaxes(peer_v, 1, 2).astype(bf16))

    tiles = bias_tiles(rel_bias, past)
    x = jnp.concatenate([_to_blocks(x_prompt), x_sample.transpose(1, 0, 2).reshape(TB, D_MODEL)], axis=0)
    outs_p, outs_s = [], []
    for l in range(depth):
        p = {k: v[l] for k, v in small.items()}
        state = (cache_k, cache_v, state_rg_h[l], state_rg_conv[l], state_s5_re[l], state_s5_im[l])
        x, op, os_ = _layer(x, l, p, wb, tiles, dims, state, final=(l == depth - 1))
        outs_p.append(op)
        outs_s.append(os_)

    x_prompt_rows, x_sample_rows = x
    y_prompt = _from_blocks(x_prompt_rows, n_seq)
    y_sample = x_sample_rows.reshape(SUBLEN, n_stream, D_MODEL).transpose(1, 0, 2)
    stack = lambda outs, i: jnp.stack([o[i] for o in outs])
    return (y_prompt, y_sample) + tuple(stack(outs_p, i) for i in range(6)) + tuple(stack(outs_s, i)
                                                                                    for i in range(6))
```

```python
import functools
import math

import jax
import jax.numpy as jnp
from jax import lax
from jax.experimental import pallas as pl
from jax.experimental.pallas import tpu as pltpu

f32 = jnp.float32
bf16 = jnp.bfloat16

D_MODEL = 2048
CHUNK = 64
EPS = 1e-6
N_HEADS = 8
QK_DIM = 64
V_DIM = 128
Q_WIDTH = 1024
ATT_WIDTH = 1024
N_BUCKETS = 32
MAX_DISTANCE = 128
RG_WIDTH = 1024
RG_BLOCKS = 8
RG_BLOCK = 128
CONV_W = 4
RG_C = 8.0
S5_WIDTH = 1024
S5_GROUP = 16
S5_GROUPS = 64
S5_STATE = 64
S5_LANES = S5_GROUPS * S5_STATE
PEER_HEADS = 8
N_KEYS = 128
N_EXPERTS = N_KEYS * N_KEYS
PEER_TOPK = 16
N_BRANCH = 3
IN_COLS = 2 * Q_WIDTH + ATT_WIDTH + 2 * RG_WIDTH + S5_WIDTH + N_BRANCH * D_MODEL

SUBLANES = 8
LANES = 128
VMEM_LIMIT = 56 * 1024 * 1024

NSUB = SUBLANES
SUBLEN = 32
TB = NSUB * SUBLEN
MASK_NEG = -1e30

COL_Q, COL_K, COL_V, COL_RG, COL_GATE, COL_S5 = 0, 1, 2, 3, 4, 5
COL_GATES = 6 * 1024


def _cparams(sem):
    return pltpu.CompilerParams(dimension_semantics=sem, vmem_limit_bytes=VMEM_LIMIT)


def _gelu(x):
    return 0.5 * x * (1.0 + lax.erf(x * (2.0 ** -0.5)))


def _dot_nt(a, b):
    return lax.dot_general(a, b, (((1,), (1,)), ((), ())), preferred_element_type=f32)


def _dot_tn(a, b):
    return lax.dot_general(a, b, (((0,), (0,)), ((), ())), preferred_element_type=f32)


def _norm_matmul_body(x_ref, g_ref, w_ref, o_ref, *rest, emit_xn):
    xn_ref = rest[-1]

    @pl.when(pl.program_id(1) == 0)
    def _():
        x = x_ref[...]
        ms = jnp.mean(x * x, axis=-1, keepdims=True)
        xn = (x * lax.rsqrt(ms + EPS) * g_ref[...]).astype(bf16)
        xn_ref[...] = xn
        if emit_xn:
            rest[0][...] = xn

    o_ref[...] = jnp.dot(xn_ref[...], w_ref[...], preferred_element_type=f32).astype(o_ref.dtype)


def norm_matmul(x, g, w, *, layer, tm, tn, out_dtype, emit_xn=False):
    m, d = x.shape
    n = w.shape[2]
    out_shape = [jax.ShapeDtypeStruct((m, n), out_dtype)]
    out_specs = [pl.BlockSpec((tm, tn), lambda i, j: (i, j))]
    if emit_xn:
        out_shape.append(jax.ShapeDtypeStruct((m, d), bf16))
        out_specs.append(pl.BlockSpec((tm, d), lambda i, j: (i, 0)))
    res = pl.pallas_call(
        functools.partial(_norm_matmul_body, emit_xn=emit_xn),
        grid=(m // tm, n // tn),
        in_specs=[
            pl.BlockSpec((tm, d), lambda i, j: (i, 0)),
            pl.BlockSpec((1, d), lambda i, j: (0, 0)),
            pl.BlockSpec((None, d, tn), lambda i, j: (layer, 0, j)),
        ],
        out_specs=out_specs,
        out_shape=out_shape,
        scratch_shapes=[pltpu.VMEM((tm, d), bf16)],
        compiler_params=_cparams(("parallel", "arbitrary")),
        name="norm_matmul",
    )(x, g.reshape(1, d), w)
    return res if emit_xn else res[0]


def _group_mean_sq(x, ones_bd):
    sq = x * x
    hi = sq.astype(bf16)
    lo = (sq - hi.astype(f32)).astype(bf16)
    s = jnp.dot(hi, ones_bd, preferred_element_type=f32) + jnp.dot(lo, ones_bd, preferred_element_type=f32)
    return s * (1.0 / QK_DIM)


def _qknorm_body(q_ref, k_ref, v_ref, qg_ref, kg_ref, qo_ref, kb_ref, vb_ref, kp_ref, ks_ref, vp_ref, vs_ref,
                 kn_ref, *, n_prompt_blocks):
    r = lax.broadcasted_iota(jnp.int32, (LANES, LANES), 0) // QK_DIM
    c = lax.broadcasted_iota(jnp.int32, (LANES, LANES), 1) // QK_DIM
    ones_bd = (r == c).astype(bf16)
    for t in range(Q_WIDTH // LANES):
        sl = slice(t * LANES, (t + 1) * LANES)
        q = q_ref[:, sl]
        k = k_ref[:, sl]
        qn = q * lax.rsqrt(_group_mean_sq(q, ones_bd) + EPS) * qg_ref[:, sl]
        kn = k * lax.rsqrt(_group_mean_sq(k, ones_bd) + EPS) * kg_ref[:, sl]
        qo_ref[:, sl] = (qn * (QK_DIM ** -0.5)).astype(bf16)
        kn_ref[:, sl] = kn
        kb_ref[:, sl] = kn.astype(bf16)
    vb_ref[...] = v_ref[...].astype(bf16)

    is_sample = pl.program_id(0) >= n_prompt_blocks

    @pl.when(jnp.logical_not(is_sample))
    def _():
        kp_ref[...] = kn_ref[...]
        vp_ref[...] = v_ref[...]

    @pl.when(is_sample)
    def _():
        ks_ref[...] = kn_ref[...]
        vs_ref[...] = v_ref[...]


def qk_norm(proj, qn_g, kn_g, *, n_prompt_blocks):
    m = proj.shape[0]
    mp = n_prompt_blocks * TB
    assert m == mp + TB
    qg = jnp.tile(qn_g, Q_WIDTH // QK_DIM).reshape(1, Q_WIDTH)
    kg = jnp.tile(kn_g, Q_WIDTH // QK_DIM).reshape(1, Q_WIDTH)
    col = lambda c: pl.BlockSpec((TB, Q_WIDTH), lambda i: (i, c))
    gain = pl.BlockSpec((1, Q_WIDTH), lambda i: (0, 0))
    prompt = pl.BlockSpec((TB, Q_WIDTH), lambda i: (jnp.minimum(i, n_prompt_blocks - 1), 0))
    sample = pl.BlockSpec((TB, Q_WIDTH), lambda i: (0, 0))
    half = lambda rows, dt: jax.ShapeDtypeStruct((rows, Q_WIDTH), dt)
    return pl.pallas_call(
        functools.partial(_qknorm_body, n_prompt_blocks=n_prompt_blocks),
        grid=(m // TB,),
        in_specs=[col(COL_Q), col(COL_K), col(COL_V), gain, gain],
        out_specs=[col(0), col(0), col(0), prompt, sample, prompt, sample],
        out_shape=[half(m, bf16), half(m, bf16), half(m, bf16), half(mp, f32), half(TB, f32), half(mp, f32),
                   half(TB, f32)],
        scratch_shapes=[pltpu.VMEM((TB, Q_WIDTH), f32)],
        compiler_params=_cparams(("arbitrary",)),
        name="qk_norm",
    )(proj, proj, proj, qg, kg)


def _bucket(rel):
    half = N_BUCKETS // 2
    max_exact = half // 2
    ret = jnp.where(rel > 0, half, 0)
    n = jnp.abs(rel)
    nf = jnp.maximum(n, 1).astype(f32)
    large = max_exact + (jnp.log(nf / max_exact) / math.log(MAX_DISTANCE / max_exact)
                         * (half - max_exact)).astype(jnp.int32)
    large = jnp.minimum(large, half - 1)
    return ret + jnp.where(n < max_exact, n, large)


def _bias_from_bucket(bucket, rb_ref, h):
    out = jnp.zeros(bucket.shape, f32)
    for j in range(N_BUCKETS):
        out = jnp.where(bucket == j, rb_ref[j, h], out)
    return out


def _block_time(r):
    return (r % NSUB) * SUBLEN + r // NSUB


def _bias_body(rb_ref, pb_ref, sbc_ref, sbn_ref, *, past):
    h = pl.program_id(0)
    tq = _block_time(lax.broadcasted_iota(jnp.int32, (TB, TB), 0))
    tk = _block_time(lax.broadcasted_iota(jnp.int32, (TB, TB), 1))
    b0 = _bias_from_bucket(_bucket(tk - tq), rb_ref, h)
    pb_ref[0] = jnp.where(tk // CHUNK <= tq // CHUNK, b0, MASK_NEG)
    pb_ref[1] = _bias_from_bucket(_bucket(tk - tq - TB), rb_ref, h)
    pb_ref[2] = _bias_from_bucket(_bucket(tk - tq - 2 * TB), rb_ref, h)
    nq = sbc_ref.shape[0]
    qpos = past + lax.broadcasted_iota(jnp.int32, (nq, past), 0)
    kpos = lax.broadcasted_iota(jnp.int32, (nq, past), 1)
    sbc_ref[...] = _bias_from_bucket(_bucket(kpos - qpos), rb_ref, h)
    qn = lax.broadcasted_iota(jnp.int32, (nq, nq), 0)
    kn = lax.broadcasted_iota(jnp.int32, (nq, nq), 1)
    sbn_ref[...] = _bias_from_bucket(_bucket(kn - qn), rb_ref, h)


def bias_tiles(rel_bias, past):
    return pl.pallas_call(
        functools.partial(_bias_body, past=past),
        grid=(N_HEADS,),
        in_specs=[pl.BlockSpec(memory_space=pltpu.SMEM)],
        out_specs=[
            pl.BlockSpec((None, 3, TB, TB), lambda h: (h, 0, 0, 0)),
            pl.BlockSpec((None, SUBLEN, past), lambda h: (h, 0, 0)),
            pl.BlockSpec((None, SUBLEN, SUBLEN), lambda h: (h, 0, 0)),
        ],
        out_shape=[
            jax.ShapeDtypeStruct((N_HEADS, 3, TB, TB), f32),
            jax.ShapeDtypeStruct((N_HEADS, SUBLEN, past), f32),
            jax.ShapeDtypeStruct((N_HEADS, SUBLEN, SUBLEN), f32),
        ],
        compiler_params=_cparams(("parallel",)),
        name="bias_tiles",
    )(rel_bias)


def _stack_maps(q):
    lane = lax.broadcasted_iota(jnp.int32, q.shape, 1)
    zero = jnp.zeros_like(q)
    return jnp.concatenate([jnp.where(lane < QK_DIM, q, zero), jnp.where(lane >= QK_DIM, q, zero)], axis=0)


def _online_softmax_step(s, v, m_prev, l_prev, acc_prev):
    m_new = jnp.maximum(m_prev, jnp.max(s, axis=-1, keepdims=True))
    alpha = jnp.exp(m_prev - m_new)
    p = jnp.exp(s - jnp.tile(m_new, (1, s.shape[1] // LANES)))
    l_new = alpha * l_prev + jnp.sum(p, axis=-1, keepdims=True)
    acc_new = alpha * acc_prev + jnp.dot(p.astype(bf16), v, preferred_element_type=f32)
    return m_new, l_new, acc_new


def _diff_combine(l, acc, lam, g, nq, lam_init):
    o = acc[:nq] / l[:nq] - lam * (acc[nq:] / l[nq:])
    ms = jnp.mean(o * o, axis=-1, keepdims=True)
    return o * lax.rsqrt(ms + EPS) * g * (1.0 - lam_init)


def _prompt_attn_body(lam_ref, q_ref, k_ref, v_ref, pb_ref, g_ref, o_ref, m_ref, l_ref, acc_ref, *, lam_init):
    qi = pl.program_id(1)
    kj = pl.program_id(2)

    @pl.when(kj == 0)
    def _():
        m_ref[...] = jnp.full(m_ref.shape, -jnp.inf, f32)
        l_ref[...] = jnp.zeros(l_ref.shape, f32)
        acc_ref[...] = jnp.zeros(acc_ref.shape, f32)

    @pl.when(kj <= qi)
    def _():
        for h in range(N_HEADS):
            sl = slice(h * V_DIM, (h + 1) * V_DIM)
            qq = _stack_maps(q_ref[:, sl])
            s = _dot_nt(qq, k_ref[:, sl])
            s = (s.reshape(2, TB, TB) + pb_ref[h][None]).reshape(2 * TB, TB)
            m, l, acc = _online_softmax_step(s, v_ref[:, sl], m_ref[h], l_ref[h], acc_ref[h])
            m_ref[h] = m
            l_ref[h] = l
            acc_ref[h] = acc

    @pl.when(kj == pl.num_programs(2) - 1)
    def _():
        for h in range(N_HEADS):
            o = _diff_combine(l_ref[h], acc_ref[h], lam_ref[0], g_ref[...], TB, lam_init)
            o_ref[:, h * V_DIM:(h + 1) * V_DIM] = o.astype(o_ref.dtype)


def prompt_attention(qn, kb, vb, pbias, lam, subln_g, *, n_seq, blocks_per_seq, lam_init):
    nb = blocks_per_seq
    m = n_seq * nb * TB
    kv = pl.BlockSpec((TB, ATT_WIDTH), lambda b, i, j: (b * nb + jnp.minimum(i, j), 0))
    return pl.pallas_call(
        functools.partial(_prompt_attn_body, lam_init=lam_init),
        grid=(n_seq, nb, nb),
        in_specs=[
            pl.BlockSpec(memory_space=pltpu.SMEM),
            pl.BlockSpec((TB, Q_WIDTH), lambda b, i, j: (b * nb + i, 0)),
            kv, kv,
            pl.BlockSpec((N_HEADS, None, TB, TB), lambda b, i, j: (0, jnp.clip(i - j, 0, 2), 0, 0)),
            pl.BlockSpec((1, V_DIM), lambda b, i, j: (0, 0)),
        ],
        out_specs=pl.BlockSpec((TB, ATT_WIDTH), lambda b, i, j: (b * nb + i, 0)),
        out_shape=jax.ShapeDtypeStruct((m, ATT_WIDTH), bf16),
        scratch_shapes=[pltpu.VMEM((N_HEADS, 2 * TB, LANES), f32), pltpu.VMEM((N_HEADS, 2 * TB, LANES), f32),
                        pltpu.VMEM((N_HEADS, 2 * TB, V_DIM), f32)],
        compiler_params=_cparams(("parallel", "parallel", "arbitrary")),
        name="prompt_attention",
    )(lam, qn, kb, vb, pbias, subln_g.reshape(1, V_DIM))


def _sample_attn_body(lam_ref, q_ref, kn_ref, vn_ref, ck_ref, cv_ref, sbc_ref, sbn_ref, g_ref, o_ref,
                      m_ref, l_ref, acc_ref, *, lam_init):
    kj = pl.program_id(1)
    nq = q_ref.shape[0]
    for h in range(N_HEADS):
        sl = slice(h * V_DIM, (h + 1) * V_DIM)
        qq = _stack_maps(q_ref[:, sl])

        @pl.when(kj == 0)
        def _():
            s = _dot_nt(qq, kn_ref[:, sl].astype(bf16))
            s = (s.reshape(2, nq, nq) + sbn_ref[h][None]).reshape(2 * nq, nq)
            m = jnp.max(s, axis=-1, keepdims=True)
            p = jnp.exp(s - m)
            m_ref[h] = jnp.broadcast_to(m, m_ref.shape[1:])
            l_ref[h] = jnp.broadcast_to(jnp.sum(p, axis=-1, keepdims=True), l_ref.shape[1:])
            acc_ref[h] = jnp.dot(p.astype(bf16), vn_ref[:, sl].astype(bf16), preferred_element_type=f32)

        ck = ck_ref.shape[0] // N_HEADS
        head_rows = pl.ds(h, ck, stride=N_HEADS)
        s = _dot_nt(qq, ck_ref[head_rows, :].astype(bf16))
        s = (s.reshape(2, nq, ck) + sbc_ref[h][None]).reshape(2 * nq, ck)
        m, l, acc = _online_softmax_step(s, cv_ref[head_rows, :].astype(bf16), m_ref[h], l_ref[h], acc_ref[h])
        m_ref[h] = m
        l_ref[h] = l
        acc_ref[h] = acc

        @pl.when(kj == pl.num_programs(1) - 1)
        def _():
            o = _diff_combine(l_ref[h], acc_ref[h], lam_ref[0], g_ref[...], nq, lam_init)
            o_ref[:, sl] = o.astype(o_ref.dtype)


def sample_attention(q_s, kn_s, v_s, cache_k, cache_v, sbc, sbn, lam, subln_g, *, layer, lam_init):
    nb, nq, _ = q_s.shape
    past = cache_k.shape[2] // N_HEADS
    chunk = min(past, 1024)
    return pl.pallas_call(
        functools.partial(_sample_attn_body, lam_init=lam_init),
        grid=(nb, past // chunk),
        in_specs=[
            pl.BlockSpec(memory_space=pltpu.SMEM),
            pl.BlockSpec((None, nq, ATT_WIDTH), lambda b, j: (b, 0, 0)),
            pl.BlockSpec((None, nq, ATT_WIDTH), lambda b, j: (b, 0, 0)),
            pl.BlockSpec((None, nq, ATT_WIDTH), lambda b, j: (b, 0, 0)),
            pl.BlockSpec((None, None, chunk * N_HEADS, V_DIM), lambda b, j: (layer, b, j, 0)),
            pl.BlockSpec((None, None, chunk * N_HEADS, V_DIM), lambda b, j: (layer, b, j, 0)),
            pl.BlockSpec((N_HEADS, nq, chunk), lambda b, j: (0, 0, j)),
            pl.BlockSpec((N_HEADS, nq, nq), lambda b, j: (0, 0, 0)),
            pl.BlockSpec((1, V_DIM), lambda b, j: (0, 0)),
        ],
        out_specs=pl.BlockSpec((None, nq, ATT_WIDTH), lambda b, j: (b, 0, 0)),
        out_shape=jax.ShapeDtypeStruct((nb, nq, ATT_WIDTH), bf16),
        scratch_shapes=[pltpu.VMEM((N_HEADS, 2 * nq, LANES), f32), pltpu.VMEM((N_HEADS, 2 * nq, LANES), f32),
                        pltpu.VMEM((N_HEADS, 2 * nq, V_DIM), f32)],
        compiler_params=_cparams(("parallel", "arbitrary")),
        name="sample_attention",
    )(lam, q_s, kn_s, v_s, cache_k, cache_v, sbc, sbn, subln_g.reshape(1, V_DIM))


def _chain_carries(c0, mult, add):
    rows = [c0]
    c = c0
    for s in range(NSUB):
        c = mult[s:s + 1] * c + add[s:s + 1]
        if s + 1 < NSUB:
            rows.append(c)
    return jnp.concatenate(rows, axis=0), c


def _rglru_body(xrg_ref, xgate_ref, head_s_ref, h0_s_ref, cw_ref, cb_ref, wa_ref, wx_ref, ba_ref, bx_ref,
                lam_ref, hg_ref, htail_ref, ctail_ref, xbuf_ref, a_ref, b_ref, hcar_ref, ptail_ref,
                *, blocks_per_seq, n_prompt_blocks):
    g = pl.program_id(0)
    is_sample = g >= n_prompt_blocks
    hl = (CONV_W - 1) * NSUB

    @pl.when(g % blocks_per_seq == 0)
    def _():
        hcar_ref[...] = jnp.zeros(hcar_ref.shape, f32)
        ptail_ref[...] = jnp.zeros(ptail_ref.shape, f32)

    x = xrg_ref[...]
    tail = x[TB - hl:]
    sub = lax.broadcasted_iota(jnp.int32, (NSUB, RG_WIDTH), 0)
    for k in range(CONV_W - 1):
        rs = slice(k * NSUB, (k + 1) * NSUB)
        prompt_head = jnp.where(sub == 0, pltpu.roll(ptail_ref[rs, :], 1, 0), pltpu.roll(tail[rs], 1, 0))
        xbuf_ref[rs, :] = jnp.where(is_sample, head_s_ref[rs, :], prompt_head)
    xbuf_ref[hl:, :] = x
    ptail_ref[...] = tail
    ctail_ref[...] = tail

    xc = cb_ref[...] + cw_ref[CONV_W - 1:CONV_W, :] * x
    for j in range(1, CONV_W):
        xc = xc + cw_ref[CONV_W - 1 - j:CONV_W - j, :] * xbuf_ref[hl - j * NSUB:hl - j * NSUB + TB, :]

    xcb = xc.astype(bf16)
    r_parts, i_parts = [], []
    for blk in range(RG_BLOCKS):
        sl = slice(blk * RG_BLOCK, (blk + 1) * RG_BLOCK)
        r_parts.append(jnp.dot(xcb[:, sl], wa_ref[blk], preferred_element_type=f32))
        i_parts.append(jnp.dot(xcb[:, sl], wx_ref[blk], preferred_element_type=f32))
    r = jax.nn.sigmoid(jnp.concatenate(r_parts, axis=1) + ba_ref[...])
    ig = jax.nn.sigmoid(jnp.concatenate(i_parts, axis=1) + bx_ref[...])
    nl = -lam_ref[...]
    softplus = jnp.maximum(nl, 0.0) + jnp.log1p(jnp.exp(-jnp.abs(nl)))
    a = jnp.exp(-RG_C * r * softplus)
    a_ref[...] = a
    b_ref[...] = jnp.sqrt(1.0 - a * a) * (ig * xc)

    def rows(i):
        return pl.ds(pl.multiple_of(i * NSUB, NSUB), NSUB)

    def local(i, carry):
        p, e = carry
        ai = a_ref[rows(i), :]
        return p * ai, ai * e + b_ref[rows(i), :]

    p, e = lax.fori_loop(0, SUBLEN, local,
                         (jnp.ones((NSUB, RG_WIDTH), f32), jnp.zeros((NSUB, RG_WIDTH), f32)), unroll=4)
    entry, exit_state = _chain_carries(hcar_ref[...], p, e)
    hcar_ref[...] = exit_state
    h_init = jnp.where(is_sample, h0_s_ref[...], entry)

    def step(i, hprev):
        hcur = a_ref[rows(i), :] * hprev + b_ref[rows(i), :]
        b_ref[rows(i), :] = hcur
        return hcur

    h_last = lax.fori_loop(0, SUBLEN, step, h_init, unroll=4)
    htail_ref[...] = h_last
    hg_ref[...] = (b_ref[...] * _gelu(xgate_ref[...])).astype(hg_ref.dtype)


def rglru_mixer(proj, head_s, h0_s, conv_w, conv_b, wa, wx, ba, bx, lam, *, blocks_per_seq, n_prompt_blocks):
    m = proj.shape[0]
    nblk = m // TB
    hl = (CONV_W - 1) * NSUB
    row = lambda v: v.reshape(1, RG_WIDTH)
    full = lambda shape: pl.BlockSpec(shape, lambda g: (0,) * len(shape))
    return pl.pallas_call(
        functools.partial(_rglru_body, blocks_per_seq=blocks_per_seq, n_prompt_blocks=n_prompt_blocks),
        grid=(nblk,),
        in_specs=[
            pl.BlockSpec((TB, RG_WIDTH), lambda g: (g, COL_RG)),
            pl.BlockSpec((TB, RG_WIDTH), lambda g: (g, COL_GATE)),
            full((hl, RG_WIDTH)),
            full((NSUB, RG_WIDTH)),
            full((CONV_W, RG_WIDTH)),
            full((1, RG_WIDTH)),
            full((RG_BLOCKS, RG_BLOCK, RG_BLOCK)),
            full((RG_BLOCKS, RG_BLOCK, RG_BLOCK)),
            full((1, RG_WIDTH)),
            full((1, RG_WIDTH)),
            full((1, RG_WIDTH)),
        ],
        out_specs=[
            pl.BlockSpec((TB, RG_WIDTH), lambda g: (g, 0)),
            pl.BlockSpec((None, NSUB, RG_WIDTH), lambda g: (g, 0, 0)),
            pl.BlockSpec((None, hl, RG_WIDTH), lambda g: (g, 0, 0)),
        ],
        out_shape=[
            jax.ShapeDtypeStruct((m, RG_WIDTH), bf16),
            jax.ShapeDtypeStruct((nblk, NSUB, RG_WIDTH), f32),
            jax.ShapeDtypeStruct((nblk, hl, RG_WIDTH), f32),
        ],
        scratch_shapes=[
            pltpu.VMEM((hl + TB, RG_WIDTH), f32),
            pltpu.VMEM((TB, RG_WIDTH), f32),
            pltpu.VMEM((TB, RG_WIDTH), f32),
            pltpu.VMEM((1, RG_WIDTH), f32),
            pltpu.VMEM((hl, RG_WIDTH), f32),
        ],
        compiler_params=_cparams(("arbitrary",)),
        name="rglru_mixer",
    )(proj, proj, head_s, h0_s, conv_w, row(conv_b), wa.astype(bf16), wx.astype(bf16), row(ba), row(bx),
      row(lam))


S5_CHUNKS = 8
S5_CH_IN = S5_WIDTH // S5_CHUNKS
S5_CH_ST = S5_LANES // S5_CHUNKS
S5_SCAN_LANES = 512


def _cmul(ar, ai, br, bi):
    return ar * br - ai * bi, ar * bi + ai * br


def _s5_body(u_ref, s0r_ref, s0i_ref, abr_ref, abi_ref, bdr_ref, bdi_ref, cdr_ref, cdi_ref, d_ref, gw_ref,
             gb_ref, gs_ref, str_ref, sti_ref, xr_ref, xi_ref, car_ref, cai_ref,
             *, blocks_per_seq, n_prompt_blocks):
    g = pl.program_id(0)
    is_sample = g >= n_prompt_blocks

    @pl.when(g % blocks_per_seq == 0)
    def _():
        car_ref[...] = jnp.zeros(car_ref.shape, f32)
        cai_ref[...] = jnp.zeros(cai_ref.shape, f32)

    u = u_ref[...]
    ub = u.astype(bf16)
    for c in range(S5_CHUNKS):
        ci = slice(c * S5_CH_IN, (c + 1) * S5_CH_IN)
        cs = slice(c * S5_CH_ST, (c + 1) * S5_CH_ST)
        xr_ref[:, cs] = jnp.dot(ub[:, ci], bdr_ref[c], preferred_element_type=f32)
        xi_ref[:, cs] = jnp.dot(ub[:, ci], bdi_ref[c], preferred_element_type=f32)

    def rows(i):
        return pl.ds(pl.multiple_of(i * NSUB, NSUB), NSUB)

    for c in range(S5_LANES // S5_SCAN_LANES):
        ls = slice(c * S5_SCAN_LANES, (c + 1) * S5_SCAN_LANES)
        ar1 = abr_ref[:, ls]
        ai1 = abi_ref[:, ls]
        ar = jnp.broadcast_to(ar1, (NSUB, S5_SCAN_LANES))
        ai = jnp.broadcast_to(ai1, (NSUB, S5_SCAN_LANES))

        def local(i, carry, ls=ls, ar=ar, ai=ai):
            er, ei = carry
            nr, ni = _cmul(ar, ai, er, ei)
            return nr + xr_ref[rows(i), ls], ni + xi_ref[rows(i), ls]

        zero = jnp.zeros((NSUB, S5_SCAN_LANES), f32)
        er, ei = lax.fori_loop(0, SUBLEN, local, (zero, zero), unroll=4)
        pr, pi = ar1, ai1
        for _ in range(int(math.log2(SUBLEN))):
            pr, pi = _cmul(pr, pi, pr, pi)
        rows_r, rows_i = [car_ref[:, ls]], [cai_ref[:, ls]]
        cr, ci_ = rows_r[0], rows_i[0]
        for s in range(NSUB):
            mr, mi = _cmul(pr, pi, cr, ci_)
            cr, ci_ = mr + er[s:s + 1], mi + ei[s:s + 1]
            if s + 1 < NSUB:
                rows_r.append(cr)
                rows_i.append(ci_)
        car_ref[:, ls] = cr
        cai_ref[:, ls] = ci_
        init_r = jnp.where(is_sample, s0r_ref[:, ls], jnp.concatenate(rows_r, axis=0))
        init_i = jnp.where(is_sample, s0i_ref[:, ls], jnp.concatenate(rows_i, axis=0))

        def step(i, carry, ls=ls, ar=ar, ai=ai):
            sr, si = carry
            nr, ni = _cmul(ar, ai, sr, si)
            nr = nr + xr_ref[rows(i), ls]
            ni = ni + xi_ref[rows(i), ls]
            xr_ref[rows(i), ls] = nr
            xi_ref[rows(i), ls] = ni
            return nr, ni

        sr, si = lax.fori_loop(0, SUBLEN, step, (init_r, init_i), unroll=4)
        str_ref[:, ls] = sr
        sti_ref[:, ls] = si

    y_parts = []
    for c in range(S5_CHUNKS):
        cs = slice(c * S5_CH_ST, (c + 1) * S5_CH_ST)
        y_parts.append(jnp.dot(xr_ref[:, cs].astype(bf16), cdr_ref[c], preferred_element_type=f32)
                       + jnp.dot(xi_ref[:, cs].astype(bf16), cdi_ref[c], preferred_element_type=f32))
    y = jnp.concatenate(y_parts, axis=1) + d_ref[...] * u
    gy = _gelu(y)
    z = jnp.dot(gy.astype(bf16), gw_ref[...], preferred_element_type=f32) + gb_ref[...]
    gs_ref[...] = (gy * jax.nn.sigmoid(z)).astype(gs_ref.dtype)


def _s5_discretise(lam_re, lam_im, log_step, b_re, b_im, c_re, c_im):
    step = jnp.exp(log_step)[:, None]
    mag = jnp.exp(lam_re * step)
    ab_re, ab_im = mag * jnp.cos(lam_im * step), mag * jnp.sin(lam_im * step)
    den = lam_re * lam_re + lam_im * lam_im
    nr, ni = ab_re - 1.0, ab_im
    co_re = (nr * lam_re + ni * lam_im) / den
    co_im = (ni * lam_re - nr * lam_im) / den
    bb_re = co_re[..., None] * b_re - co_im[..., None] * b_im
    bb_im = co_re[..., None] * b_im + co_im[..., None] * b_re
    gpc = S5_GROUPS // S5_CHUNKS
    eye = jnp.eye(gpc, dtype=f32)

    def in_chunks(bb):
        t = bb.reshape(S5_CHUNKS, gpc, S5_STATE, S5_GROUP)
        return jnp.einsum('cgph,gk->cghkp', t, eye).reshape(S5_CHUNKS, S5_CH_IN, S5_CH_ST).astype(bf16)

    def out_chunks(cc):
        t = cc.reshape(S5_CHUNKS, gpc, S5_GROUP, S5_STATE)
        return jnp.einsum('cghp,gk->cgpkh', t, eye).reshape(S5_CHUNKS, S5_CH_ST, S5_CH_IN).astype(bf16)

    return (ab_re.reshape(1, S5_LANES), ab_im.reshape(1, S5_LANES), in_chunks(bb_re), in_chunks(bb_im),
            out_chunks(c_re), out_chunks(-c_im))


def s5_mixer(proj, s0_re, s0_im, disc, d, glu_w, glu_b, *, layer, blocks_per_seq, n_prompt_blocks):
    m = proj.shape[0]
    nblk = m // TB
    ab_re, ab_im, bd_re, bd_im, cd_re, cd_im = disc
    full = lambda shape: pl.BlockSpec(shape, lambda g: (0,) * len(shape))
    return pl.pallas_call(
        functools.partial(_s5_body, blocks_per_seq=blocks_per_seq, n_prompt_blocks=n_prompt_blocks),
        grid=(nblk,),
        in_specs=[
            pl.BlockSpec((TB, S5_WIDTH), lambda g: (g, COL_S5)),
            full((NSUB, S5_LANES)),
            full((NSUB, S5_LANES)),
            full((1, S5_LANES)),
            full((1, S5_LANES)),
            full((S5_CHUNKS, S5_CH_IN, S5_CH_ST)),
            full((S5_CHUNKS, S5_CH_IN, S5_CH_ST)),
            full((S5_CHUNKS, S5_CH_ST, S5_CH_IN)),
            full((S5_CHUNKS, S5_CH_ST, S5_CH_IN)),
            full((1, S5_WIDTH)),
            pl.BlockSpec((None, S5_WIDTH, S5_WIDTH), lambda g: (layer, 0, 0)),
            full((1, S5_WIDTH)),
        ],
        out_specs=[
            pl.BlockSpec((TB, S5_WIDTH), lambda g: (g, 0)),
            pl.BlockSpec((None, NSUB, S5_LANES), lambda g: (g, 0, 0)),
            pl.BlockSpec((None, NSUB, S5_LANES), lambda g: (g, 0, 0)),
        ],
        out_shape=[
            jax.ShapeDtypeStruct((m, S5_WIDTH), bf16),
            jax.ShapeDtypeStruct((nblk, NSUB, S5_LANES), f32),
            jax.ShapeDtypeStruct((nblk, NSUB, S5_LANES), f32),
        ],
        scratch_shapes=[
            pltpu.VMEM((TB, S5_LANES), f32),
            pltpu.VMEM((TB, S5_LANES), f32),
            pltpu.VMEM((1, S5_LANES), f32),
            pltpu.VMEM((1, S5_LANES), f32),
        ],
        compiler_params=_cparams(("arbitrary",)),
        name="s5_mixer",
    )(proj, s0_re, s0_im, ab_re, ab_im, bd_re, bd_im, cd_re, cd_im, d.reshape(1, S5_WIDTH),
      glu_w, glu_b.reshape(1, S5_WIDTH))


def _merge_body(att_ref, rg_ref, s5_ref, wa_ref, wr_ref, ws_ref, g0_ref, g1_ref, g2_ref, b0_ref, b1_ref,
                b2_ref, o_ref):
    acc = jax.nn.sigmoid(g0_ref[...] + b0_ref[...]) * jnp.dot(att_ref[...], wa_ref[...],
                                                               preferred_element_type=f32)
    acc = acc + jax.nn.sigmoid(g1_ref[...] + b1_ref[...]) * jnp.dot(rg_ref[...], wr_ref[...],
                                                                     preferred_element_type=f32)
    acc = acc + jax.nn.sigmoid(g2_ref[...] + b2_ref[...]) * jnp.dot(s5_ref[...], ws_ref[...],
                                                                     preferred_element_type=f32)
    o_ref[...] = acc.astype(o_ref.dtype)


def gated_merge(o_att, hg, gs, w_att_o, w_rg_o, w_s5_o, proj, b_gate, *, layer, tm, tn):
    m = o_att.shape[0]
    k = o_att.shape[1]
    gcol = COL_GATES // tn
    bcol = D_MODEL // tn
    act = pl.BlockSpec((tm, k), lambda i, j: (i, 0))
    wsp = pl.BlockSpec((None, k, tn), lambda i, j: (layer, 0, j))
    gate = lambda br: pl.BlockSpec((tm, tn), lambda i, j: (i, gcol + br * bcol + j))
    bias = lambda br: pl.BlockSpec((1, tn), lambda i, j: (0, br * bcol + j))
    bg = b_gate.reshape(1, N_BRANCH * D_MODEL)
    return pl.pallas_call(
        _merge_body,
        grid=(m // tm, D_MODEL // tn),
        in_specs=[act, act, act, wsp, wsp, wsp, gate(0), gate(1), gate(2), bias(0), bias(1), bias(2)],
        out_specs=pl.BlockSpec((tm, tn), lambda i, j: (i, j)),
        out_shape=jax.ShapeDtypeStruct((m, D_MODEL), bf16),
        compiler_params=_cparams(("parallel", "arbitrary")),
        name="gated_merge",
    )(o_att, hg, gs, w_att_o, w_rg_o, w_s5_o, proj, proj, proj, bg, bg, bg)


def _residual_matmul_body(x_ref, a_ref, w_ref, o_ref):
    o_ref[...] = x_ref[...] + jnp.dot(a_ref[...], w_ref[...], preferred_element_type=f32)


def residual_matmul(x, a, w, *, layer, tm, tn):
    m, n = x.shape
    k = a.shape[1]
    return pl.pallas_call(
        _residual_matmul_body,
        grid=(m // tm, n // tn),
        in_specs=[
            pl.BlockSpec((tm, tn), lambda i, j: (i, j)),
            pl.BlockSpec((tm, k), lambda i, j: (i, 0)),
            pl.BlockSpec((None, k, tn), lambda i, j: (layer, 0, j)),
        ],
        out_specs=pl.BlockSpec((tm, tn), lambda i, j: (i, j)),
        out_shape=jax.ShapeDtypeStruct((m, n), f32),
        compiler_params=_cparams(("parallel", "arbitrary")),
        name="residual_matmul",
    )(x, a, w)


PEER_TT = 256
PEER_ET = 1024
PEER_I1 = PEER_ET // N_KEYS
PEER_PIECE = 32
PEER_RG = 2


def _sort_network(n):
    pairs = []
    p = 1
    while p < n:
        k = p
        while k >= 1:
            for j in range(k % p, n - k, 2 * k):
                for i in range(min(k, n - j - k)):
                    if (i + j) // (2 * p) == (i + j + k) // (2 * p):
                        pairs.append((i + j, i + j + k))
            k //= 2
        p *= 2
    return pairs


def _compare_exchange(v, i, j):
    v[i], v[j] = jnp.maximum(v[i], v[j]), jnp.minimum(v[i], v[j])


def _bitonic_merge(v):
    n = len(v)
    s = n // 2
    while s >= 1:
        for i in range(n):
            if i & s == 0:
                _compare_exchange(v, i, i + s)
        s //= 2


def _top_values(s, k):
    v = [s[i:i + SUBLANES] for i in range(0, s.shape[0], SUBLANES)]
    for i, j in _sort_network(len(v)):
        _compare_exchange(v, i, j)
    shifts = [SUBLANES // 2, SUBLANES // 4, SUBLANES // 8]
    if len(v) < k:
        assert 2 * len(v) == k
        v = v + [pltpu.roll(x, shifts[0], 0) for x in reversed(v)]
        _bitonic_merge(v)
        shifts = shifts[1:]
    v = v[:k]
    for shift in shifts:
        other = [pltpu.roll(x, shift, 0) for x in v]
        v = [jnp.maximum(v[j], other[k - 1 - j]) for j in range(k)]
        _bitonic_merge(v)
    return [x[0:1] for x in v]


def _next_float(x, up):
    bits = pltpu.bitcast(x, jnp.int32)
    step = jnp.where((x > 0) == up, 1, -1)
    return pltpu.bitcast(bits + step, f32)


def _two_sum(a, b):
    s = a + b
    bb = s - a
    return s, (a - (s - bb)) + (b - bb)


def _sum_threshold(theta, a):
    x, e = _two_sum(jnp.broadcast_to(theta, a.shape), -a)
    half_gap = (theta - _next_float(theta, False)) * 0.5
    y, err = _two_sum(x, e - half_gap)
    y = jnp.where(err > 0, _next_float(y, True), y)
    return jnp.where(a + y >= theta, y, _next_float(y, True))


def _peer_route_body(q_ref, sk_ref, s2_ref, e2_ref, thr_ref, c1_ref):
    for h in range(PEER_HEADS):
        s1 = _dot_nt(sk_ref[0], q_ref[:, (2 * h) * N_KEYS:(2 * h + 1) * N_KEYS])
        s2 = _dot_nt(sk_ref[1], q_ref[:, (2 * h + 1) * N_KEYS:(2 * h + 2) * N_KEYS])
        v1 = _top_values(s1, PEER_TOPK)
        v2 = _top_values(s2, PEER_TOPK)
        cand = [v1[k1] + v2[k2] for k1 in range(PEER_TOPK) for k2 in range(PEER_TOPK // (k1 + 1))]
        pad = (-len(cand)) % (SUBLANES * SUBLANES)
        cand = jnp.concatenate(cand + [jnp.full_like(cand[0], -jnp.inf)] * pad, axis=0)
        top = _top_values(cand, PEER_TOPK)
        theta = top[-1]
        z = top[0] * 0.0
        for t in top:
            z = z + jnp.exp(t - top[0])
        s2_ref[h] = s2
        e2_ref[h] = jnp.exp(s2 - v2[0])
        thr_ref[h] = _sum_threshold(theta, s1)
        c1_ref[h] = jnp.exp(s1 - v1[0]) / z


def peer_route(q, subkeys):
    m = q.shape[0]
    shp = jax.ShapeDtypeStruct((PEER_HEADS, N_KEYS, m), f32)
    spec = pl.BlockSpec((PEER_HEADS, N_KEYS, PEER_TT), lambda i: (0, 0, i))
    return pl.pallas_call(
        _peer_route_body,
        grid=(m // PEER_TT,),
        in_specs=[
            pl.BlockSpec((PEER_TT, q.shape[1]), lambda i: (i, 0)),
            pl.BlockSpec((2, N_KEYS, N_KEYS), lambda i: (0, 0, 0)),
        ],
        out_specs=[spec] * 4,
        out_shape=[shp] * 4,
        compiler_params=_cparams(("parallel",)),
        name="peer_route",
    )(q, subkeys)


def _peer_gate_weights(s2_ref, e2_ref, thr_ref, c1_ref, w_ref, token_tiles):
    n_pc = N_KEYS // PEER_PIECE
    for tc in token_tiles:
        ts = slice(tc * LANES, (tc + 1) * LANES)
        for r0 in range(0, PEER_I1, PEER_RG):
            w = [[jnp.zeros((PEER_PIECE, LANES), f32) for _ in range(n_pc)] for _ in range(PEER_RG)]
            for h in range(PEER_HEADS):
                thr = [thr_ref[h, r0 + r:r0 + r + 1, ts] for r in range(PEER_RG)]
                c1 = [c1_ref[h, r0 + r:r0 + r + 1, ts] for r in range(PEER_RG)]
                for pc in range(n_pc):
                    ks = slice(pc * PEER_PIECE, (pc + 1) * PEER_PIECE)
                    s2 = s2_ref[h, ks, ts]
                    e2 = e2_ref[h, ks, ts]
                    for r in range(PEER_RG):
                        w[r][pc] = w[r][pc] + jnp.where(s2 >= thr[r], e2 * c1[r], 0.0)
            for r in range(PEER_RG):
                for pc in range(n_pc):
                    base = (r0 + r) * N_KEYS + pc * PEER_PIECE
                    w_ref[base:base + PEER_PIECE, ts] = w[r][pc]


def _peer_expert_body(xnt_ref, u_ref, vt_ref, s2_ref, e2_ref, thr0_ref, c10_ref, thrn_ref, c1n_ref, o_ref,
                      act_ref, w_ref):
    j = pl.program_id(1)
    n_tiles = o_ref.shape[1] // LANES
    cur = w_ref.at[j % 2]
    nxt = w_ref.at[(j + 1) % 2]

    @pl.when(j == 0)
    def _():
        o_ref[...] = jnp.zeros(o_ref.shape, f32)
        _peer_gate_weights(s2_ref, e2_ref, thr0_ref, c10_ref, cur, range(n_tiles))

    zt = jnp.dot(u_ref[...], xnt_ref[...], preferred_element_type=f32)
    _peer_gate_weights(s2_ref, e2_ref, thrn_ref, c1n_ref, nxt, range(n_tiles // 2))
    act_ref[...] = (_gelu(zt) * cur[...]).astype(bf16)
    o_ref[...] += jnp.dot(vt_ref[...], act_ref[...], preferred_element_type=f32)
    _peer_gate_weights(s2_ref, e2_ref, thrn_ref, c1n_ref, nxt, range(n_tiles // 2, n_tiles))


def peer_experts(xnt, u_tab, vt_tab, s2, e2, thr, c1, *, layer, tt):
    d, m = xnt.shape
    thr = thr.reshape(PEER_HEADS, N_KEYS // PEER_I1, PEER_I1, m)
    c1 = c1.reshape(PEER_HEADS, N_KEYS // PEER_I1, PEER_I1, m)
    once = pl.Buffered(1)
    tok = pl.BlockSpec((PEER_HEADS, N_KEYS, tt), lambda i, j: (0, 0, i), pipeline_mode=once)
    n_steps = N_EXPERTS // PEER_ET
    row0 = pl.BlockSpec((PEER_HEADS, None, PEER_I1, tt), lambda i, j: (0, 0, 0, i))
    row_next = pl.BlockSpec((PEER_HEADS, None, PEER_I1, tt), lambda i, j: (0, jnp.minimum(j + 1, n_steps - 1), 0, i))
    return pl.pallas_call(
        _peer_expert_body,
        grid=(m // tt, n_steps),
        in_specs=[
            pl.BlockSpec((d, tt), lambda i, j: (0, i), pipeline_mode=once),
            pl.BlockSpec((None, PEER_ET, d), lambda i, j: (layer, j, 0)),
            pl.BlockSpec((None, d, PEER_ET), lambda i, j: (layer, 0, j)),
            tok, tok, row0, row0, row_next, row_next,
        ],
        out_specs=pl.BlockSpec((d, tt), lambda i, j: (0, i)),
        out_shape=jax.ShapeDtypeStruct((d, m), f32),
        scratch_shapes=[pltpu.VMEM((PEER_ET, tt), bf16), pltpu.VMEM((2, PEER_ET, tt), f32)],
        compiler_params=_cparams(("parallel", "arbitrary")),
        name="peer_experts",
    )(xnt, u_tab, vt_tab, s2, e2, thr, c1, thr, c1)


def _add_transposed_body(x_ref, dt_ref, o_ref):
    o_ref[...] = x_ref[...] + dt_ref[...].T


def _add_transposed_split_body(x_ref, dt_ref, op_ref, os_ref, *, n_prompt_blocks):
    y = x_ref[...] + dt_ref[...].T
    is_sample = pl.program_id(0) >= n_prompt_blocks

    @pl.when(jnp.logical_not(is_sample))
    def _():
        op_ref[...] = y

    @pl.when(is_sample)
    def _():
        os_ref[...] = y


def add_transposed_split(x, dt, *, n_prompt_blocks):
    m, d = x.shape
    mp = n_prompt_blocks * TB
    assert m == mp + TB
    return pl.pallas_call(
        functools.partial(_add_transposed_split_body, n_prompt_blocks=n_prompt_blocks),
        grid=(m // TB,),
        in_specs=[pl.BlockSpec((TB, d), lambda i: (i, 0)), pl.BlockSpec((d, TB), lambda i: (0, i))],
        out_specs=[pl.BlockSpec((TB, d), lambda i: (jnp.minimum(i, n_prompt_blocks - 1), 0)),
                   pl.BlockSpec((TB, d), lambda i: (0, 0))],
        out_shape=[jax.ShapeDtypeStruct((mp, d), f32), jax.ShapeDtypeStruct((TB, d), f32)],
        compiler_params=_cparams(("arbitrary",)),
        name="add_transposed_split",
    )(x, dt)


def add_transposed(x, dt, *, tm):
    m, d = x.shape
    return pl.pallas_call(
        _add_transposed_body,
        grid=(m // tm,),
        in_specs=[pl.BlockSpec((tm, d), lambda i: (i, 0)), pl.BlockSpec((d, tm), lambda i: (0, i))],
        out_specs=pl.BlockSpec((tm, d), lambda i: (i, 0)),
        out_shape=jax.ShapeDtypeStruct((m, d), f32),
        compiler_params=_cparams(("parallel",)),
        name="add_transposed",
    )(x, dt)


def _to_blocks(x):
    b, t, c = x.shape
    return x.reshape(b, t // TB, NSUB, SUBLEN, c).transpose(0, 1, 3, 2, 4).reshape(b * t, c)


def _from_blocks(y, b):
    m, c = y.shape
    t = m // b
    return y.reshape(b, t // TB, SUBLEN, NSUB, c).transpose(0, 1, 3, 2, 4).reshape(b, t, c)


def _layer(x, l, p, wb, tiles, dims, state, final):
    n_seq, blocks_per_seq, n_stream = dims
    n_prompt_blocks = n_seq * blocks_per_seq
    mp = n_prompt_blocks * TB
    pbias, sbc, sbn = tiles
    cache_k, cache_v, rg_h0, rg_conv0, s5_re0, s5_im0 = state
    tm = 3 * TB if x.shape[0] % (3 * TB) == 0 else TB
    tm_big = 1408 if x.shape[0] % 1408 == 0 else tm
    lam_init = 0.8 - 0.6 * math.exp(-0.3 * l)
    lam = (jnp.exp(jnp.sum(p['lam_q1'] * p['lam_k1'])) - jnp.exp(jnp.sum(p['lam_q2'] * p['lam_k2']))
           + lam_init).reshape(1)

    proj = norm_matmul(x, p['norm1_g'], wb['w_in'], layer=l, tm=tm_big, tn=1024, out_dtype=f32)
    qn, kb, vb, kn_p, kn_s, v_p, v_s = qk_norm(proj, p['qn_g'], p['kn_g'], n_prompt_blocks=n_prompt_blocks)

    o_att = prompt_attention(qn, kb, vb, pbias, lam, p['subln_g'], n_seq=n_seq,
                             blocks_per_seq=blocks_per_seq, lam_init=lam_init)
    stream = lambda a: a.reshape(SUBLEN, n_stream, -1).transpose(1, 0, 2)
    rows = lambda c: c.reshape(c.shape[0], c.shape[1], -1, V_DIM)
    o_s = sample_attention(stream(qn[mp:]), stream(kn_s), stream(v_s), rows(cache_k), rows(cache_v), sbc, sbn,
                           lam, p['subln_g'], layer=l, lam_init=lam_init)
    o_att = jnp.concatenate([o_att, o_s.transpose(1, 0, 2).reshape(TB, ATT_WIDTH)], axis=0)

    head_s = rg_conv0.transpose(1, 0, 2).reshape((CONV_W - 1) * NSUB, RG_WIDTH)
    hg, h_tail, c_tail = rglru_mixer(proj, head_s, rg_h0, p['rg_conv_w'], p['rg_conv_b'], p['rg_wa'],
                                     p['rg_wx'], p['rg_ba'], p['rg_bx'], p['rg_lambda'],
                                     blocks_per_seq=blocks_per_seq, n_prompt_blocks=n_prompt_blocks)
    disc = _s5_discretise(p['s5_lam_re'], p['s5_lam_im'], p['s5_log_step'], p['s5_b_re'], p['s5_b_im'],
                          p['s5_c_re'], p['s5_c_im'])
    gs, st_re, st_im = s5_mixer(proj, s5_re0.reshape(NSUB, S5_LANES), s5_im0.reshape(NSUB, S5_LANES), disc,
                                p['s5_d'], wb['s5_glu_w'], p['s5_glu_b'], layer=l,
                                blocks_per_seq=blocks_per_seq, n_prompt_blocks=n_prompt_blocks)

    merged = gated_merge(o_att, hg, gs, wb['w_att_o'], wb['w_rg_o'], wb['w_s5_o'], proj, p['b_gate'],
                         layer=l, tm=tm_big, tn=512)
    x = residual_matmul(x, merged, wb['w_out'], layer=l, tm=tm_big, tn=1024)

    q, xn2 = norm_matmul(x, p['norm2_g'], wb['peer_wq'], layer=l, tm=tm, tn=1024, out_dtype=bf16,
                         emit_xn=True)
    s2, e2, thr, c1 = peer_route(q, p['peer_subkeys'].astype(bf16))
    peer_t = peer_experts(xn2.T, wb['peer_u'], wb['peer_vt'], s2, e2, thr, c1, layer=l, tt=tm)
    if final:
        x = add_transposed_split(x, peer_t, n_prompt_blocks=n_prompt_blocks)
    else:
        x = add_transposed(x, peer_t, tm=TB)

    last = jnp.arange(n_seq) * blocks_per_seq + blocks_per_seq - 1
    hl = CONV_W - 1
    outs_p = (
        _from_blocks(kn_p, n_seq).reshape(n_seq, -1, N_HEADS, 2 * QK_DIM),
        _from_blocks(v_p, n_seq).reshape(n_seq, -1, N_HEADS, V_DIM),
        h_tail[last, NSUB - 1],
        c_tail[last].reshape(n_seq, hl, NSUB, RG_WIDTH)[:, :, NSUB - 1],
        st_re[last, NSUB - 1].reshape(n_seq, S5_GROUPS, S5_STATE),
        st_im[last, NSUB - 1].reshape(n_seq, S5_GROUPS, S5_STATE),
    )
    outs_s = (
        stream(kn_s).reshape(n_stream, SUBLEN, N_HEADS, 2 * QK_DIM),
        stream(v_s).reshape(n_stream, SUBLEN, N_HEADS, V_DIM),
        h_tail[n_prompt_blocks],
        c_tail[n_prompt_blocks].reshape(hl, NSUB, RG_WIDTH).transpose(1, 0, 2),
        st_re[n_prompt_blocks].reshape(n_stream, S5_GROUPS, S5_STATE),
        st_im[n_prompt_blocks].reshape(n_stream, S5_GROUPS, S5_STATE),
    )
    return x, outs_p, outs_s


def kernel(x_prompt, x_sample, cache_k, cache_v, state_rg_h, state_rg_conv, state_s5_re, state_s5_im, rel_bias, norm1_g, w_in, b_gate, qn_g, kn_g, lam_q1, lam_k1, lam_q2, lam_k2, subln_g, w_att_o, rg_conv_w, rg_conv_b, rg_wa, rg_ba, rg_wx, rg_bx, rg_lambda, w_rg_o, s5_lam_re, s5_lam_im, s5_log_step, s5_b_re, s5_b_im, s5_c_re, s5_c_im, s5_d, s5_glu_w, s5_glu_b, w_s5_o, w_out, norm2_g, peer_wq, peer_subkeys, peer_u, peer_v):
    n_seq, t_prompt, _ = x_prompt.shape
    n_stream, t_sample, _ = x_sample.shape
    depth = w_in.shape[0]
    assert n_stream == NSUB and t_sample == SUBLEN and t_prompt % TB == 0
    blocks_per_seq = t_prompt // TB
    dims = (n_seq, blocks_per_seq, n_stream)
    past = cache_k.shape[2]

    small = dict(norm1_g=norm1_g, b_gate=b_gate, qn_g=qn_g, kn_g=kn_g, lam_q1=lam_q1,
                 lam_k1=lam_k1, lam_q2=lam_q2, lam_k2=lam_k2, subln_g=subln_g,
                 rg_conv_w=rg_conv_w, rg_conv_b=rg_conv_b, rg_wa=rg_wa, rg_ba=rg_ba, rg_wx=rg_wx,
                 rg_bx=rg_bx, rg_lambda=rg_lambda, s5_lam_re=s5_lam_re,
                 s5_lam_im=s5_lam_im, s5_log_step=s5_log_step, s5_b_re=s5_b_re, s5_b_im=s5_b_im,
                 s5_c_re=s5_c_re, s5_c_im=s5_c_im, s5_d=s5_d, s5_glu_b=s5_glu_b,
                 norm2_g=norm2_g, peer_subkeys=peer_subkeys)
    wb = dict(w_in=w_in.astype(bf16), w_att_o=w_att_o.astype(bf16), w_rg_o=w_rg_o.astype(bf16),
              w_s5_o=w_s5_o.astype(bf16), w_out=w_out.astype(bf16), s5_glu_w=s5_glu_w.astype(bf16),
              peer_wq=peer_wq.astype(bf16), peer_u=peer_u.astype(bf16),
              peer_vt=jnp.swapaxes(peer_v, 1, 2).astype(bf16))

    tiles = bias_tiles(rel_bias, past)
    x = jnp.concatenate([_to_blocks(x_prompt), x_sample.transpose(1, 0, 2).reshape(TB, D_MODEL)], axis=0)
    outs_p, outs_s = [], []
    for l in range(depth):
        p = {k: v[l] for k, v in small.items()}
        state = (cache_k, cache_v, state_rg_h[l], state_rg_conv[l], state_s5_re[l], state_s5_im[l])
        x, op, os_ = _layer(x, l, p, wb, tiles, dims, state, final=(l == depth - 1))
        outs_p.append(op)
        outs_s.append(os_)

    x_prompt_rows, x_sample_rows = x
    y_prompt = _from_blocks(x_prompt_rows, n_seq)
    y_sample = x_sample_rows.reshape(SUBLEN, n_stream, D_MODEL).transpose(1, 0, 2)
    stack = lambda outs, i: jnp.stack([o[i] for o in outs])
    return (y_prompt, y_sample) + tuple(stack(outs_p, i) for i in range(6)) + tuple(stack(outs_s, i)
                                                                                    for i in range(6))
```

```python
import functools
import math

import jax
import jax.numpy as jnp
from jax import lax
from jax.experimental import pallas as pl
from jax.experimental.pallas import tpu as pltpu

f32 = jnp.float32
bf16 = jnp.bfloat16

D_MODEL = 2048
CHUNK = 64
EPS = 1e-6
N_HEADS = 8
QK_DIM = 64
V_DIM = 128
Q_WIDTH = 1024
ATT_WIDTH = 1024
N_BUCKETS = 32
MAX_DISTANCE = 128
RG_WIDTH = 1024
RG_BLOCKS = 8
RG_BLOCK = 128
CONV_W = 4
RG_C = 8.0
S5_WIDTH = 1024
S5_GROUP = 16
S5_GROUPS = 64
S5_STATE = 64
S5_LANES = S5_GROUPS * S5_STATE
PEER_HEADS = 8
N_KEYS = 128
N_EXPERTS = N_KEYS * N_KEYS
PEER_TOPK = 16
N_BRANCH = 3
IN_COLS = 2 * Q_WIDTH + ATT_WIDTH + 2 * RG_WIDTH + S5_WIDTH + N_BRANCH * D_MODEL

SUBLANES = 8
LANES = 128
VMEM_LIMIT = 56 * 1024 * 1024

NSUB = SUBLANES
SUBLEN = 32
TB = NSUB * SUBLEN
MASK_NEG = -1e30

COL_Q, COL_K, COL_V, COL_RG, COL_GATE, COL_S5 = 0, 1, 2, 3, 4, 5
COL_GATES = 6 * 1024


def _cparams(sem):
    return pltpu.CompilerParams(dimension_semantics=sem, vmem_limit_bytes=VMEM_LIMIT)


def _gelu(x):
    return 0.5 * x * (1.0 + lax.erf(x * (2.0 ** -0.5)))


def _dot_nt(a, b):
    return lax.dot_general(a, b, (((1,), (1,)), ((), ())), preferred_element_type=f32)


def _dot_tn(a, b):
    return lax.dot_general(a, b, (((0,), (0,)), ((), ())), preferred_element_type=f32)


def _norm_matmul_body(x_ref, g_ref, w_ref, o_ref, *rest, emit_xn):
    xn_ref = rest[-1]

    @pl.when(pl.program_id(1) == 0)
    def _():
        x = x_ref[...]
        ms = jnp.mean(x * x, axis=-1, keepdims=True)
        xn = (x * lax.rsqrt(ms + EPS) * g_ref[...]).astype(bf16)
        xn_ref[...] = xn
        if emit_xn:
            rest[0][...] = xn

    o_ref[...] = jnp.dot(xn_ref[...], w_ref[...], preferred_element_type=f32).astype(o_ref.dtype)


def norm_matmul(x, g, w, *, layer, tm, tn, out_dtype, emit_xn=False):
    m, d = x.shape
    n = w.shape[2]
    out_shape = [jax.ShapeDtypeStruct((m, n), out_dtype)]
    out_specs = [pl.BlockSpec((tm, tn), lambda i, j: (i, j))]
    if emit_xn:
        out_shape.append(jax.ShapeDtypeStruct((m, d), bf16))
        out_specs.append(pl.BlockSpec((tm, d), lambda i, j: (i, 0)))
    res = pl.pallas_call(
        functools.partial(_norm_matmul_body, emit_xn=emit_xn),
        grid=(m // tm, n // tn),
        in_specs=[
            pl.BlockSpec((tm, d), lambda i, j: (i, 0)),
            pl.BlockSpec((1, d), lambda i, j: (0, 0)),
            pl.BlockSpec((None, d, tn), lambda i, j: (layer, 0, j)),
        ],
        out_specs=out_specs,
        out_shape=out_shape,
        scratch_shapes=[pltpu.VMEM((tm, d), bf16)],
        compiler_params=_cparams(("parallel", "arbitrary")),
        name="norm_matmul",
    )(x, g.reshape(1, d), w)
    return res if emit_xn else res[0]


def _group_mean_sq(x, ones_bd):
    sq = x * x
    hi = sq.astype(bf16)
    lo = (sq - hi.astype(f32)).astype(bf16)
    s = jnp.dot(hi, ones_bd, preferred_element_type=f32) + jnp.dot(lo, ones_bd, preferred_element_type=f32)
    return s * (1.0 / QK_DIM)


def _qknorm_body(q_ref, k_ref, v_ref, qg_ref, kg_ref, qo_ref, kb_ref, vb_ref, kp_ref, ks_ref, vp_ref, vs_ref,
                 kn_ref, *, n_prompt_blocks):
    r = lax.broadcasted_iota(jnp.int32, (LANES, LANES), 0) // QK_DIM
    c = lax.broadcasted_iota(jnp.int32, (LANES, LANES), 1) // QK_DIM
    ones_bd = (r == c).astype(bf16)
    for t in range(Q_WIDTH // LANES):
        sl = slice(t * LANES, (t + 1) * LANES)
        q = q_ref[:, sl]
        k = k_ref[:, sl]
        qn = q * lax.rsqrt(_group_mean_sq(q, ones_bd) + EPS) * qg_ref[:, sl]
        kn = k * lax.rsqrt(_group_mean_sq(k, ones_bd) + EPS) * kg_ref[:, sl]
        qo_ref[:, sl] = (qn * (QK_DIM ** -0.5)).astype(bf16)
        kn_ref[:, sl] = kn
        kb_ref[:, sl] = kn.astype(bf16)
    vb_ref[...] = v_ref[...].astype(bf16)

    is_sample = pl.program_id(0) >= n_prompt_blocks

    @pl.when(jnp.logical_not(is_sample))
    def _():
        kp_ref[...] = kn_ref[...]
        vp_ref[...] = v_ref[...]

    @pl.when(is_sample)
    def _():
        ks_ref[...] = kn_ref[...]
        vs_ref[...] = v_ref[...]


def qk_norm(proj, qn_g, kn_g, *, n_prompt_blocks):
    m = proj.shape[0]
    mp = n_prompt_blocks * TB
    assert m == mp + TB
    qg = jnp.tile(qn_g, Q_WIDTH // QK_DIM).reshape(1, Q_WIDTH)
    kg = jnp.tile(kn_g, Q_WIDTH // QK_DIM).reshape(1, Q_WIDTH)
    col = lambda c: pl.BlockSpec((TB, Q_WIDTH), lambda i: (i, c))
    gain = pl.BlockSpec((1, Q_WIDTH), lambda i: (0, 0))
    prompt = pl.BlockSpec((TB, Q_WIDTH), lambda i: (jnp.minimum(i, n_prompt_blocks - 1), 0))
    sample = pl.BlockSpec((TB, Q_WIDTH), lambda i: (0, 0))
    half = lambda rows, dt: jax.ShapeDtypeStruct((rows, Q_WIDTH), dt)
    return pl.pallas_call(
        functools.partial(_qknorm_body, n_prompt_blocks=n_prompt_blocks),
        grid=(m // TB,),
        in_specs=[col(COL_Q), col(COL_K), col(COL_V), gain, gain],
        out_specs=[col(0), col(0), col(0), prompt, sample, prompt, sample],
        out_shape=[half(m, bf16), half(m, bf16), half(m, bf16), half(mp, f32), half(TB, f32), half(mp, f32),
                   half(TB, f32)],
        scratch_shapes=[pltpu.VMEM((TB, Q_WIDTH), f32)],
        compiler_params=_cparams(("arbitrary",)),
        name="qk_norm",
    )(proj, proj, proj, qg, kg)


def _bucket(rel):
    half = N_BUCKETS // 2
    max_exact = half // 2
    ret = jnp.where(rel > 0, half, 0)
    n = jnp.abs(rel)
    nf = jnp.maximum(n, 1).astype(f32)
    large = max_exact + (jnp.log(nf / max_exact) / math.log(MAX_DISTANCE / max_exact)
                         * (half - max_exact)).astype(jnp.int32)
    large = jnp.minimum(large, half - 1)
    return ret + jnp.where(n < max_exact, n, large)


def _bias_from_bucket(bucket, rb_ref, h):
    out = jnp.zeros(bucket.shape, f32)
    for j in range(N_BUCKETS):
        out = jnp.where(bucket == j, rb_ref[j, h], out)
    return out


def _block_time(r):
    return (r % NSUB) * SUBLEN + r // NSUB


def _bias_body(rb_ref, pb_ref, sbc_ref, sbn_ref, *, past):
    h = pl.program_id(0)
    tq = _block_time(lax.broadcasted_iota(jnp.int32, (TB, TB), 0))
    tk = _block_time(lax.broadcasted_iota(jnp.int32, (TB, TB), 1))
    b0 = _bias_from_bucket(_bucket(tk - tq), rb_ref, h)
    pb_ref[0] = jnp.where(tk // CHUNK <= tq // CHUNK, b0, MASK_NEG)
    pb_ref[1] = _bias_from_bucket(_bucket(tk - tq - TB), rb_ref, h)
    pb_ref[2] = _bias_from_bucket(_bucket(tk - tq - 2 * TB), rb_ref, h)
    nq = sbc_ref.shape[0]
    qpos = past + lax.broadcasted_iota(jnp.int32, (nq, past), 0)
    kpos = lax.broadcasted_iota(jnp.int32, (nq, past), 1)
    sbc_ref[...] = _bias_from_bucket(_bucket(kpos - qpos), rb_ref, h)
    qn = lax.broadcasted_iota(jnp.int32, (nq, nq), 0)
    kn = lax.broadcasted_iota(jnp.int32, (nq, nq), 1)
    sbn_ref[...] = _bias_from_bucket(_bucket(kn - qn), rb_ref, h)


def bias_tiles(rel_bias, past):
    return pl.pallas_call(
        functools.partial(_bias_body, past=past),
        grid=(N_HEADS,),
        in_specs=[pl.BlockSpec(memory_space=pltpu.SMEM)],
        out_specs=[
            pl.BlockSpec((None, 3, TB, TB), lambda h: (h, 0, 0, 0)),
            pl.BlockSpec((None, SUBLEN, past), lambda h: (h, 0, 0)),
            pl.BlockSpec((None, SUBLEN, SUBLEN), lambda h: (h, 0, 0)),
        ],
        out_shape=[
            jax.ShapeDtypeStruct((N_HEADS, 3, TB, TB), f32),
            jax.ShapeDtypeStruct((N_HEADS, SUBLEN, past), f32),
            jax.ShapeDtypeStruct((N_HEADS, SUBLEN, SUBLEN), f32),
        ],
        compiler_params=_cparams(("parallel",)),
        name="bias_tiles",
    )(rel_bias)


def _stack_maps(q):
    lane = lax.broadcasted_iota(jnp.int32, q.shape, 1)
    zero = jnp.zeros_like(q)
    return jnp.concatenate([jnp.where(lane < QK_DIM, q, zero), jnp.where(lane >= QK_DIM, q, zero)], axis=0)


def _online_softmax_step(s, v, m_prev, l_prev, acc_prev):
    m_new = jnp.maximum(m_prev, jnp.max(s, axis=-1, keepdims=True))
    alpha = jnp.exp(m_prev - m_new)
    p = jnp.exp(s - jnp.tile(m_new, (1, s.shape[1] // LANES)))
    l_new = alpha * l_prev + jnp.sum(p, axis=-1, keepdims=True)
    acc_new = alpha * acc_prev + jnp.dot(p.astype(bf16), v, preferred_element_type=f32)
    return m_new, l_new, acc_new


def _diff_combine(l, acc, lam, g, nq, lam_init):
    o = acc[:nq] / l[:nq] - lam * (acc[nq:] / l[nq:])
    ms = jnp.mean(o * o, axis=-1, keepdims=True)
    return o * lax.rsqrt(ms + EPS) * g * (1.0 - lam_init)


def _prompt_attn_body(lam_ref, q_ref, k_ref, v_ref, pb_ref, g_ref, o_ref, m_ref, l_ref, acc_ref, *, lam_init):
    qi = pl.program_id(1)
    kj = pl.program_id(2)

    @pl.when(kj == 0)
    def _():
        m_ref[...] = jnp.full(m_ref.shape, -jnp.inf, f32)
        l_ref[...] = jnp.zeros(l_ref.shape, f32)
        acc_ref[...] = jnp.zeros(acc_ref.shape, f32)

    @pl.when(kj <= qi)
    def _():
        for h in range(N_HEADS):
            sl = slice(h * V_DIM, (h + 1) * V_DIM)
            qq = _stack_maps(q_ref[:, sl])
            s = _dot_nt(qq, k_ref[:, sl])
            s = (s.reshape(2, TB, TB) + pb_ref[h][None]).reshape(2 * TB, TB)
            m, l, acc = _online_softmax_step(s, v_ref[:, sl], m_ref[h], l_ref[h], acc_ref[h])
            m_ref[h] = m
            l_ref[h] = l
            acc_ref[h] = acc

    @pl.when(kj == pl.num_programs(2) - 1)
    def _():
        for h in range(N_HEADS):
            o = _diff_combine(l_ref[h], acc_ref[h], lam_ref[0], g_ref[...], TB, lam_init)
            o_ref[:, h * V_DIM:(h + 1) * V_DIM] = o.astype(o_ref.dtype)


def prompt_attention(qn, kb, vb, pbias, lam, subln_g, *, n_seq, blocks_per_seq, lam_init):
    nb = blocks_per_seq
    m = n_seq * nb * TB
    kv = pl.BlockSpec((TB, ATT_WIDTH), lambda b, i, j: (b * nb + jnp.minimum(i, j), 0))
    return pl.pallas_call(
        functools.partial(_prompt_attn_body, lam_init=lam_init),
        grid=(n_seq, nb, nb),
        in_specs=[
            pl.BlockSpec(memory_space=pltpu.SMEM),
            pl.BlockSpec((TB, Q_WIDTH), lambda b, i, j: (b * nb + i, 0)),
            kv, kv,
            pl.BlockSpec((N_HEADS, None, TB, TB), lambda b, i, j: (0, jnp.clip(i - j, 0, 2), 0, 0)),
            pl.BlockSpec((1, V_DIM), lambda b, i, j: (0, 0)),
        ],
        out_specs=pl.BlockSpec((TB, ATT_WIDTH), lambda b, i, j: (b * nb + i, 0)),
        out_shape=jax.ShapeDtypeStruct((m, ATT_WIDTH), bf16),
        scratch_shapes=[pltpu.VMEM((N_HEADS, 2 * TB, LANES), f32), pltpu.VMEM((N_HEADS, 2 * TB, LANES), f32),
                        pltpu.VMEM((N_HEADS, 2 * TB, V_DIM), f32)],
        compiler_params=_cparams(("parallel", "parallel", "arbitrary")),
        name="prompt_attention",
    )(lam, qn, kb, vb, pbias, subln_g.reshape(1, V_DIM))


def _sample_attn_body(lam_ref, q_ref, kn_ref, vn_ref, ck_ref, cv_ref, sbc_ref, sbn_ref, g_ref, o_ref,
                      m_ref, l_ref, acc_ref, *, lam_init):
    kj = pl.program_id(1)
    nq = q_ref.shape[0]
    for h in range(N_HEADS):
        sl = slice(h * V_DIM, (h + 1) * V_DIM)
        qq = _stack_maps(q_ref[:, sl])

        @pl.when(kj == 0)
        def _():
            s = _dot_nt(qq, kn_ref[:, sl].astype(bf16))
            s = (s.reshape(2, nq, nq) + sbn_ref[h][None]).reshape(2 * nq, nq)
            m = jnp.max(s, axis=-1, keepdims=True)
            p = jnp.exp(s - m)
            m_ref[h] = jnp.broadcast_to(m, m_ref.shape[1:])
            l_ref[h] = jnp.broadcast_to(jnp.sum(p, axis=-1, keepdims=True), l_ref.shape[1:])
            acc_ref[h] = jnp.dot(p.astype(bf16), vn_ref[:, sl].astype(bf16), preferred_element_type=f32)

        ck = ck_ref.shape[0] // N_HEADS
        head_rows = pl.ds(h, ck, stride=N_HEADS)
        s = _dot_nt(qq, ck_ref[head_rows, :].astype(bf16))
        s = (s.reshape(2, nq, ck) + sbc_ref[h][None]).reshape(2 * nq, ck)
        m, l, acc = _online_softmax_step(s, cv_ref[head_rows, :].astype(bf16), m_ref[h], l_ref[h], acc_ref[h])
        m_ref[h] = m
        l_ref[h] = l
        acc_ref[h] = acc

        @pl.when(kj == pl.num_programs(1) - 1)
        def _():
            o = _diff_combine(l_ref[h], acc_ref[h], lam_ref[0], g_ref[...], nq, lam_init)
            o_ref[:, sl] = o.astype(o_ref.dtype)


def sample_attention(q_s, kn_s, v_s, cache_k, cache_v, sbc, sbn, lam, subln_g, *, layer, lam_init):
    nb, nq, _ = q_s.shape
    past = cache_k.shape[2] // N_HEADS
    chunk = min(past, 2048)
    return pl.pallas_call(
        functools.partial(_sample_attn_body, lam_init=lam_init),
        grid=(nb, past // chunk),
        in_specs=[
            pl.BlockSpec(memory_space=pltpu.SMEM),
            pl.BlockSpec((None, nq, ATT_WIDTH), lambda b, j: (b, 0, 0)),
            pl.BlockSpec((None, nq, ATT_WIDTH), lambda b, j: (b, 0, 0)),
            pl.BlockSpec((None, nq, ATT_WIDTH), lambda b, j: (b, 0, 0)),
            pl.BlockSpec((None, None, chunk * N_HEADS, V_DIM), lambda b, j: (layer, b, j, 0)),
            pl.BlockSpec((None, None, chunk * N_HEADS, V_DIM), lambda b, j: (layer, b, j, 0)),
            pl.BlockSpec((N_HEADS, nq, chunk), lambda b, j: (0, 0, j)),
            pl.BlockSpec((N_HEADS, nq, nq), lambda b, j: (0, 0, 0)),
            pl.BlockSpec((1, V_DIM), lambda b, j: (0, 0)),
        ],
        out_specs=pl.BlockSpec((None, nq, ATT_WIDTH), lambda b, j: (b, 0, 0)),
        out_shape=jax.ShapeDtypeStruct((nb, nq, ATT_WIDTH), bf16),
        scratch_shapes=[pltpu.VMEM((N_HEADS, 2 * nq, LANES), f32), pltpu.VMEM((N_HEADS, 2 * nq, LANES), f32),
                        pltpu.VMEM((N_HEADS, 2 * nq, V_DIM), f32)],
        compiler_params=_cparams(("parallel", "arbitrary")),
        name="sample_attention",
    )(lam, q_s, kn_s, v_s, cache_k, cache_v, sbc, sbn, subln_g.reshape(1, V_DIM))


def _chain_carries(c0, mult, add):
    rows = [c0]
    c = c0
    for s in range(NSUB):
        c = mult[s:s + 1] * c + add[s:s + 1]
        if s + 1 < NSUB:
            rows.append(c)
    return jnp.concatenate(rows, axis=0), c


def _rglru_body(xrg_ref, xgate_ref, head_s_ref, h0_s_ref, cw_ref, cb_ref, wa_ref, wx_ref, ba_ref, bx_ref,
                lam_ref, hg_ref, htail_ref, ctail_ref, xbuf_ref, a_ref, b_ref, hcar_ref, ptail_ref,
                *, blocks_per_seq, n_prompt_blocks):
    g = pl.program_id(0)
    is_sample = g >= n_prompt_blocks
    hl = (CONV_W - 1) * NSUB

    @pl.when(g % blocks_per_seq == 0)
    def _():
        hcar_ref[...] = jnp.zeros(hcar_ref.shape, f32)
        ptail_ref[...] = jnp.zeros(ptail_ref.shape, f32)

    x = xrg_ref[...]
    tail = x[TB - hl:]
    sub = lax.broadcasted_iota(jnp.int32, (NSUB, RG_WIDTH), 0)
    for k in range(CONV_W - 1):
        rs = slice(k * NSUB, (k + 1) * NSUB)
        prompt_head = jnp.where(sub == 0, pltpu.roll(ptail_ref[rs, :], 1, 0), pltpu.roll(tail[rs], 1, 0))
        xbuf_ref[rs, :] = jnp.where(is_sample, head_s_ref[rs, :], prompt_head)
    xbuf_ref[hl:, :] = x
    ptail_ref[...] = tail
    ctail_ref[...] = tail

    xc = cb_ref[...] + cw_ref[CONV_W - 1:CONV_W, :] * x
    for j in range(1, CONV_W):
        xc = xc + cw_ref[CONV_W - 1 - j:CONV_W - j, :] * xbuf_ref[hl - j * NSUB:hl - j * NSUB + TB, :]

    xcb = xc.astype(bf16)
    r_parts, i_parts = [], []
    for blk in range(RG_BLOCKS):
        sl = slice(blk * RG_BLOCK, (blk + 1) * RG_BLOCK)
        r_parts.append(jnp.dot(xcb[:, sl], wa_ref[blk], preferred_element_type=f32))
        i_parts.append(jnp.dot(xcb[:, sl], wx_ref[blk], preferred_element_type=f32))
    r = jax.nn.sigmoid(jnp.concatenate(r_parts, axis=1) + ba_ref[...])
    ig = jax.nn.sigmoid(jnp.concatenate(i_parts, axis=1) + bx_ref[...])
    nl = -lam_ref[...]
    softplus = jnp.maximum(nl, 0.0) + jnp.log1p(jnp.exp(-jnp.abs(nl)))
    a = jnp.exp(-RG_C * r * softplus)
    a_ref[...] = a
    b_ref[...] = jnp.sqrt(1.0 - a * a) * (ig * xc)

    def rows(i):
        return pl.ds(pl.multiple_of(i * NSUB, NSUB), NSUB)

    def local(i, carry):
        p, e = carry
        ai = a_ref[rows(i), :]
        return p * ai, ai * e + b_ref[rows(i), :]

    p, e = lax.fori_loop(0, SUBLEN, local,
                         (jnp.ones((NSUB, RG_WIDTH), f32), jnp.zeros((NSUB, RG_WIDTH), f32)), unroll=4)
    entry, exit_state = _chain_carries(hcar_ref[...], p, e)
    hcar_ref[...] = exit_state
    h_init = jnp.where(is_sample, h0_s_ref[...], entry)

    def step(i, hprev):
        hcur = a_ref[rows(i), :] * hprev + b_ref[rows(i), :]
        b_ref[rows(i), :] = hcur
        return hcur

    h_last = lax.fori_loop(0, SUBLEN, step, h_init, unroll=4)
    htail_ref[...] = h_last
    hg_ref[...] = (b_ref[...] * _gelu(xgate_ref[...])).astype(hg_ref.dtype)


def rglru_mixer(proj, head_s, h0_s, conv_w, conv_b, wa, wx, ba, bx, lam, *, blocks_per_seq, n_prompt_blocks):
    m = proj.shape[0]
    nblk = m // TB
    hl = (CONV_W - 1) * NSUB
    row = lambda v: v.reshape(1, RG_WIDTH)
    full = lambda shape: pl.BlockSpec(shape, lambda g: (0,) * len(shape))
    return pl.pallas_call(
        functools.partial(_rglru_body, blocks_per_seq=blocks_per_seq, n_prompt_blocks=n_prompt_blocks),
        grid=(nblk,),
        in_specs=[
            pl.BlockSpec((TB, RG_WIDTH), lambda g: (g, COL_RG)),
            pl.BlockSpec((TB, RG_WIDTH), lambda g: (g, COL_GATE)),
            full((hl, RG_WIDTH)),
            full((NSUB, RG_WIDTH)),
            full((CONV_W, RG_WIDTH)),
            full((1, RG_WIDTH)),
            full((RG_BLOCKS, RG_BLOCK, RG_BLOCK)),
            full((RG_BLOCKS, RG_BLOCK, RG_BLOCK)),
            full((1, RG_WIDTH)),
            full((1, RG_WIDTH)),
            full((1, RG_WIDTH)),
        ],
        out_specs=[
            pl.BlockSpec((TB, RG_WIDTH), lambda g: (g, 0)),
            pl.BlockSpec((None, NSUB, RG_WIDTH), lambda g: (g, 0, 0)),
            pl.BlockSpec((None, hl, RG_WIDTH), lambda g: (g, 0, 0)),
        ],
        out_shape=[
            jax.ShapeDtypeStruct((m, RG_WIDTH), bf16),
            jax.ShapeDtypeStruct((nblk, NSUB, RG_WIDTH), f32),
            jax.ShapeDtypeStruct((nblk, hl, RG_WIDTH), f32),
        ],
        scratch_shapes=[
            pltpu.VMEM((hl + TB, RG_WIDTH), f32),
            pltpu.VMEM((TB, RG_WIDTH), f32),
            pltpu.VMEM((TB, RG_WIDTH), f32),
            pltpu.VMEM((1, RG_WIDTH), f32),
            pltpu.VMEM((hl, RG_WIDTH), f32),
        ],
        compiler_params=_cparams(("arbitrary",)),
        name="rglru_mixer",
    )(proj, proj, head_s, h0_s, conv_w, row(conv_b), wa.astype(bf16), wx.astype(bf16), row(ba), row(bx),
      row(lam))


S5_CHUNKS = 8
S5_CH_IN = S5_WIDTH // S5_CHUNKS
S5_CH_ST = S5_LANES // S5_CHUNKS
S5_SCAN_LANES = 512


def _cmul(ar, ai, br, bi):
    return ar * br - ai * bi, ar * bi + ai * br


def _s5_body(u_ref, s0r_ref, s0i_ref, abr_ref, abi_ref, bdr_ref, bdi_ref, cdr_ref, cdi_ref, d_ref, gw_ref,
             gb_ref, gs_ref, str_ref, sti_ref, xr_ref, xi_ref, car_ref, cai_ref,
             *, blocks_per_seq, n_prompt_blocks):
    g = pl.program_id(0)
    is_sample = g >= n_prompt_blocks

    @pl.when(g % blocks_per_seq == 0)
    def _():
        car_ref[...] = jnp.zeros(car_ref.shape, f32)
        cai_ref[...] = jnp.zeros(cai_ref.shape, f32)

    u = u_ref[...]
    ub = u.astype(bf16)
    for c in range(S5_CHUNKS):
        ci = slice(c * S5_CH_IN, (c + 1) * S5_CH_IN)
        cs = slice(c * S5_CH_ST, (c + 1) * S5_CH_ST)
        xr_ref[:, cs] = jnp.dot(ub[:, ci], bdr_ref[c], preferred_element_type=f32)
        xi_ref[:, cs] = jnp.dot(ub[:, ci], bdi_ref[c], preferred_element_type=f32)

    def rows(i):
        return pl.ds(pl.multiple_of(i * NSUB, NSUB), NSUB)

    for c in range(S5_LANES // S5_SCAN_LANES):
        ls = slice(c * S5_SCAN_LANES, (c + 1) * S5_SCAN_LANES)
        ar1 = abr_ref[:, ls]
        ai1 = abi_ref[:, ls]
        ar = jnp.broadcast_to(ar1, (NSUB, S5_SCAN_LANES))
        ai = jnp.broadcast_to(ai1, (NSUB, S5_SCAN_LANES))

        def local(i, carry, ls=ls, ar=ar, ai=ai):
            er, ei = carry
            nr, ni = _cmul(ar, ai, er, ei)
            return nr + xr_ref[rows(i), ls], ni + xi_ref[rows(i), ls]

        zero = jnp.zeros((NSUB, S5_SCAN_LANES), f32)
        er, ei = lax.fori_loop(0, SUBLEN, local, (zero, zero), unroll=4)
        pr, pi = ar1, ai1
        for _ in range(int(math.log2(SUBLEN))):
            pr, pi = _cmul(pr, pi, pr, pi)
        rows_r, rows_i = [car_ref[:, ls]], [cai_ref[:, ls]]
        cr, ci_ = rows_r[0], rows_i[0]
        for s in range(NSUB):
            mr, mi = _cmul(pr, pi, cr, ci_)
            cr, ci_ = mr + er[s:s + 1], mi + ei[s:s + 1]
            if s + 1 < NSUB:
                rows_r.append(cr)
                rows_i.append(ci_)
        car_ref[:, ls] = cr
        cai_ref[:, ls] = ci_
        init_r = jnp.where(is_sample, s0r_ref[:, ls], jnp.concatenate(rows_r, axis=0))
        init_i = jnp.where(is_sample, s0i_ref[:, ls], jnp.concatenate(rows_i, axis=0))

        def step(i, carry, ls=ls, ar=ar, ai=ai):
            sr, si = carry
            nr, ni = _cmul(ar, ai, sr, si)
            nr = nr + xr_ref[rows(i), ls]
            ni = ni + xi_ref[rows(i), ls]
            xr_ref[rows(i), ls] = nr
            xi_ref[rows(i), ls] = ni
            return nr, ni

        sr, si = lax.fori_loop(0, SUBLEN, step, (init_r, init_i), unroll=4)
        str_ref[:, ls] = sr
        sti_ref[:, ls] = si

    y_parts = []
    for c in range(S5_CHUNKS):
        cs = slice(c * S5_CH_ST, (c + 1) * S5_CH_ST)
        y_parts.append(jnp.dot(xr_ref[:, cs].astype(bf16), cdr_ref[c], preferred_element_type=f32)
                       + jnp.dot(xi_ref[:, cs].astype(bf16), cdi_ref[c], preferred_element_type=f32))
    y = jnp.concatenate(y_parts, axis=1) + d_ref[...] * u
    gy = _gelu(y)
    z = jnp.dot(gy.astype(bf16), gw_ref[...], preferred_element_type=f32) + gb_ref[...]
    gs_ref[...] = (gy * jax.nn.sigmoid(z)).astype(gs_ref.dtype)


def _s5_discretise(lam_re, lam_im, log_step, b_re, b_im, c_re, c_im):
    step = jnp.exp(log_step)[:, None]
    mag = jnp.exp(lam_re * step)
    ab_re, ab_im = mag * jnp.cos(lam_im * step), mag * jnp.sin(lam_im * step)
    den = lam_re * lam_re + lam_im * lam_im
    nr, ni = ab_re - 1.0, ab_im
    co_re = (nr * lam_re + ni * lam_im) / den
    co_im = (ni * lam_re - nr * lam_im) / den
    bb_re = co_re[..., None] * b_re - co_im[..., None] * b_im
    bb_im = co_re[..., None] * b_im + co_im[..., None] * b_re
    gpc = S5_GROUPS // S5_CHUNKS
    eye = jnp.eye(gpc, dtype=f32)

    def in_chunks(bb):
        t = bb.reshape(S5_CHUNKS, gpc, S5_STATE, S5_GROUP)
        return jnp.einsum('cgph,gk->cghkp', t, eye).reshape(S5_CHUNKS, S5_CH_IN, S5_CH_ST).astype(bf16)

    def out_chunks(cc):
        t = cc.reshape(S5_CHUNKS, gpc, S5_GROUP, S5_STATE)
        return jnp.einsum('cghp,gk->cgpkh', t, eye).reshape(S5_CHUNKS, S5_CH_ST, S5_CH_IN).astype(bf16)

    return (ab_re.reshape(1, S5_LANES), ab_im.reshape(1, S5_LANES), in_chunks(bb_re), in_chunks(bb_im),
            out_chunks(c_re), out_chunks(-c_im))


def s5_mixer(proj, s0_re, s0_im, disc, d, glu_w, glu_b, *, layer, blocks_per_seq, n_prompt_blocks):
    m = proj.shape[0]
    nblk = m // TB
    ab_re, ab_im, bd_re, bd_im, cd_re, cd_im = disc
    full = lambda shape: pl.BlockSpec(shape, lambda g: (0,) * len(shape))
    return pl.pallas_call(
        functools.partial(_s5_body, blocks_per_seq=blocks_per_seq, n_prompt_blocks=n_prompt_blocks),
        grid=(nblk,),
        in_specs=[
            pl.BlockSpec((TB, S5_WIDTH), lambda g: (g, COL_S5)),
            full((NSUB, S5_LANES)),
            full((NSUB, S5_LANES)),
            full((1, S5_LANES)),
            full((1, S5_LANES)),
            full((S5_CHUNKS, S5_CH_IN, S5_CH_ST)),
            full((S5_CHUNKS, S5_CH_IN, S5_CH_ST)),
            full((S5_CHUNKS, S5_CH_ST, S5_CH_IN)),
            full((S5_CHUNKS, S5_CH_ST, S5_CH_IN)),
            full((1, S5_WIDTH)),
            pl.BlockSpec((None, S5_WIDTH, S5_WIDTH), lambda g: (layer, 0, 0)),
            full((1, S5_WIDTH)),
        ],
        out_specs=[
            pl.BlockSpec((TB, S5_WIDTH), lambda g: (g, 0)),
            pl.BlockSpec((None, NSUB, S5_LANES), lambda g: (g, 0, 0)),
            pl.BlockSpec((None, NSUB, S5_LANES), lambda g: (g, 0, 0)),
        ],
        out_shape=[
            jax.ShapeDtypeStruct((m, S5_WIDTH), bf16),
            jax.ShapeDtypeStruct((nblk, NSUB, S5_LANES), f32),
            jax.ShapeDtypeStruct((nblk, NSUB, S5_LANES), f32),
        ],
        scratch_shapes=[
            pltpu.VMEM((TB, S5_LANES), f32),
            pltpu.VMEM((TB, S5_LANES), f32),
            pltpu.VMEM((1, S5_LANES), f32),
            pltpu.VMEM((1, S5_LANES), f32),
        ],
        compiler_params=_cparams(("arbitrary",)),
        name="s5_mixer",
    )(proj, s0_re, s0_im, ab_re, ab_im, bd_re, bd_im, cd_re, cd_im, d.reshape(1, S5_WIDTH),
      glu_w, glu_b.reshape(1, S5_WIDTH))


def _merge_body(att_ref, rg_ref, s5_ref, wa_ref, wr_ref, ws_ref, g0_ref, g1_ref, g2_ref, b0_ref, b1_ref,
                b2_ref, o_ref):
    acc = jax.nn.sigmoid(g0_ref[...] + b0_ref[...]) * jnp.dot(att_ref[...], wa_ref[...],
                                                               preferred_element_type=f32)
    acc = acc + jax.nn.sigmoid(g1_ref[...] + b1_ref[...]) * jnp.dot(rg_ref[...], wr_ref[...],
                                                                     preferred_element_type=f32)
    acc = acc + jax.nn.sigmoid(g2_ref[...] + b2_ref[...]) * jnp.dot(s5_ref[...], ws_ref[...],
                                                                     preferred_element_type=f32)
    o_ref[...] = acc.astype(o_ref.dtype)


def gated_merge(o_att, hg, gs, w_att_o, w_rg_o, w_s5_o, proj, b_gate, *, layer, tm, tn):
    m = o_att.shape[0]
    k = o_att.shape[1]
    gcol = COL_GATES // tn
    bcol = D_MODEL // tn
    act = pl.BlockSpec((tm, k), lambda i, j: (i, 0))
    wsp = pl.BlockSpec((None, k, tn), lambda i, j: (layer, 0, j))
    gate = lambda br: pl.BlockSpec((tm, tn), lambda i, j: (i, gcol + br * bcol + j))
    bias = lambda br: pl.BlockSpec((1, tn), lambda i, j: (0, br * bcol + j))
    bg = b_gate.reshape(1, N_BRANCH * D_MODEL)
    return pl.pallas_call(
        _merge_body,
        grid=(m // tm, D_MODEL // tn),
        in_specs=[act, act, act, wsp, wsp, wsp, gate(0), gate(1), gate(2), bias(0), bias(1), bias(2)],
        out_specs=pl.BlockSpec((tm, tn), lambda i, j: (i, j)),
        out_shape=jax.ShapeDtypeStruct((m, D_MODEL), bf16),
        compiler_params=_cparams(("parallel", "arbitrary")),
        name="gated_merge",
    )(o_att, hg, gs, w_att_o, w_rg_o, w_s5_o, proj, proj, proj, bg, bg, bg)


def _residual_matmul_body(x_ref, a_ref, w_ref, o_ref):
    o_ref[...] = x_ref[...] + jnp.dot(a_ref[...], w_ref[...], preferred_element_type=f32)


def residual_matmul(x, a, w, *, layer, tm, tn):
    m, n = x.shape
    k = a.shape[1]
    return pl.pallas_call(
        _residual_matmul_body,
        grid=(m // tm, n // tn),
        in_specs=[
            pl.BlockSpec((tm, tn), lambda i, j: (i, j)),
            pl.BlockSpec((tm, k), lambda i, j: (i, 0)),
            pl.BlockSpec((None, k, tn), lambda i, j: (layer, 0, j)),
        ],
        out_specs=pl.BlockSpec((tm, tn), lambda i, j: (i, j)),
        out_shape=jax.ShapeDtypeStruct((m, n), f32),
        compiler_params=_cparams(("parallel", "arbitrary")),
        name="residual_matmul",
    )(x, a, w)


PEER_TT = 256
PEER_ET = 1024
PEER_I1 = PEER_ET // N_KEYS
PEER_PIECE = 32
PEER_RG = 2


def _sort_network(n):
    pairs = []
    p = 1
    while p < n:
        k = p
        while k >= 1:
            for j in range(k % p, n - k, 2 * k):
                for i in range(min(k, n - j - k)):
                    if (i + j) // (2 * p) == (i + j + k) // (2 * p):
                        pairs.append((i + j, i + j + k))
            k //= 2
        p *= 2
    return pairs


def _compare_exchange(v, i, j):
    v[i], v[j] = jnp.maximum(v[i], v[j]), jnp.minimum(v[i], v[j])


def _bitonic_merge(v):
    n = len(v)
    s = n // 2
    while s >= 1:
        for i in range(n):
            if i & s == 0:
                _compare_exchange(v, i, i + s)
        s //= 2


def _top_values(s, k):
    v = [s[i:i + SUBLANES] for i in range(0, s.shape[0], SUBLANES)]
    for i, j in _sort_network(len(v)):
        _compare_exchange(v, i, j)
    shifts = [SUBLANES // 2, SUBLANES // 4, SUBLANES // 8]
    if len(v) < k:
        assert 2 * len(v) == k
        v = v + [pltpu.roll(x, shifts[0], 0) for x in reversed(v)]
        _bitonic_merge(v)
        shifts = shifts[1:]
    v = v[:k]
    for shift in shifts:
        other = [pltpu.roll(x, shift, 0) for x in v]
        v = [jnp.maximum(v[j], other[k - 1 - j]) for j in range(k)]
        _bitonic_merge(v)
    return [x[0:1] for x in v]


def _next_float(x, up):
    bits = pltpu.bitcast(x, jnp.int32)
    step = jnp.where((x > 0) == up, 1, -1)
    return pltpu.bitcast(bits + step, f32)


def _two_sum(a, b):
    s = a + b
    bb = s - a
    return s, (a - (s - bb)) + (b - bb)


def _sum_threshold(theta, a):
    x, e = _two_sum(jnp.broadcast_to(theta, a.shape), -a)
    half_gap = (theta - _next_float(theta, False)) * 0.5
    y, err = _two_sum(x, e - half_gap)
    y = jnp.where(err > 0, _next_float(y, True), y)
    return jnp.where(a + y >= theta, y, _next_float(y, True))


def _peer_route_body(q_ref, sk_ref, s2_ref, e2_ref, thr_ref, c1_ref):
    for h in range(PEER_HEADS):
        s1 = _dot_nt(sk_ref[0], q_ref[:, (2 * h) * N_KEYS:(2 * h + 1) * N_KEYS])
        s2 = _dot_nt(sk_ref[1], q_ref[:, (2 * h + 1) * N_KEYS:(2 * h + 2) * N_KEYS])
        v1 = _top_values(s1, PEER_TOPK)
        v2 = _top_values(s2, PEER_TOPK)
        cand = [v1[k1] + v2[k2] for k1 in range(PEER_TOPK) for k2 in range(PEER_TOPK // (k1 + 1))]
        pad = (-len(cand)) % (SUBLANES * SUBLANES)
        cand = jnp.concatenate(cand + [jnp.full_like(cand[0], -jnp.inf)] * pad, axis=0)
        top = _top_values(cand, PEER_TOPK)
        theta = top[-1]
        z = top[0] * 0.0
        for t in top:
            z = z + jnp.exp(t - top[0])
        s2_ref[h] = s2
        e2_ref[h] = jnp.exp(s2 - v2[0])
        thr_ref[h] = _sum_threshold(theta, s1)
        c1_ref[h] = jnp.exp(s1 - v1[0]) / z


def peer_route(q, subkeys):
    m = q.shape[0]
    shp = jax.ShapeDtypeStruct((PEER_HEADS, N_KEYS, m), f32)
    spec = pl.BlockSpec((PEER_HEADS, N_KEYS, PEER_TT), lambda i: (0, 0, i))
    return pl.pallas_call(
        _peer_route_body,
        grid=(m // PEER_TT,),
        in_specs=[
            pl.BlockSpec((PEER_TT, q.shape[1]), lambda i: (i, 0)),
            pl.BlockSpec((2, N_KEYS, N_KEYS), lambda i: (0, 0, 0)),
        ],
        out_specs=[spec] * 4,
        out_shape=[shp] * 4,
        compiler_params=_cparams(("parallel",)),
        name="peer_route",
    )(q, subkeys)


def _peer_gate_weights(s2_ref, e2_ref, thr_ref, c1_ref, w_ref, token_tiles):
    n_pc = N_KEYS // PEER_PIECE
    for tc in token_tiles:
        ts = slice(tc * LANES, (tc + 1) * LANES)
        for r0 in range(0, PEER_I1, PEER_RG):
            w = [[jnp.zeros((PEER_PIECE, LANES), f32) for _ in range(n_pc)] for _ in range(PEER_RG)]
            for h in range(PEER_HEADS):
                thr = [thr_ref[h, r0 + r:r0 + r + 1, ts] for r in range(PEER_RG)]
                c1 = [c1_ref[h, r0 + r:r0 + r + 1, ts] for r in range(PEER_RG)]
                for pc in range(n_pc):
                    ks = slice(pc * PEER_PIECE, (pc + 1) * PEER_PIECE)
                    s2 = s2_ref[h, ks, ts]
                    e2 = e2_ref[h, ks, ts]
                    for r in range(PEER_RG):
                        w[r][pc] = w[r][pc] + jnp.where(s2 >= thr[r], e2 * c1[r], 0.0)
            for r in range(PEER_RG):
                for pc in range(n_pc):
                    base = (r0 + r) * N_KEYS + pc * PEER_PIECE
                    w_ref[base:base + PEER_PIECE, ts] = w[r][pc]


def _peer_expert_body(xnt_ref, u_ref, vt_ref, s2_ref, e2_ref, thr0_ref, c10_ref, thrn_ref, c1n_ref, o_ref,
                      act_ref, w_ref):
    j = pl.program_id(1)
    n_tiles = o_ref.shape[1] // LANES
    cur = w_ref.at[j % 2]
    nxt = w_ref.at[(j + 1) % 2]

    @pl.when(j == 0)
    def _():
        o_ref[...] = jnp.zeros(o_ref.shape, f32)
        _peer_gate_weights(s2_ref, e2_ref, thr0_ref, c10_ref, cur, range(n_tiles))

    zt = jnp.dot(u_ref[...], xnt_ref[...], preferred_element_type=f32)
    _peer_gate_weights(s2_ref, e2_ref, thrn_ref, c1n_ref, nxt, range(n_tiles // 2))
    act_ref[...] = (_gelu(zt) * cur[...]).astype(bf16)
    o_ref[...] += jnp.dot(vt_ref[...], act_ref[...], preferred_element_type=f32)
    _peer_gate_weights(s2_ref, e2_ref, thrn_ref, c1n_ref, nxt, range(n_tiles // 2, n_tiles))


def peer_experts(xnt, u_tab, vt_tab, s2, e2, thr, c1, *, layer, tt):
    d, m = xnt.shape
    thr = thr.reshape(PEER_HEADS, N_KEYS // PEER_I1, PEER_I1, m)
    c1 = c1.reshape(PEER_HEADS, N_KEYS // PEER_I1, PEER_I1, m)
    once = pl.Buffered(1)
    tok = pl.BlockSpec((PEER_HEADS, N_KEYS, tt), lambda i, j: (0, 0, i), pipeline_mode=once)
    n_steps = N_EXPERTS // PEER_ET
    row0 = pl.BlockSpec((PEER_HEADS, None, PEER_I1, tt), lambda i, j: (0, 0, 0, i))
    row_next = pl.BlockSpec((PEER_HEADS, None, PEER_I1, tt), lambda i, j: (0, jnp.minimum(j + 1, n_steps - 1), 0, i))
    return pl.pallas_call(
        _peer_expert_body,
        grid=(m // tt, n_steps),
        in_specs=[
            pl.BlockSpec((d, tt), lambda i, j: (0, i), pipeline_mode=once),
            pl.BlockSpec((None, PEER_ET, d), lambda i, j: (layer, j, 0)),
            pl.BlockSpec((None, d, PEER_ET), lambda i, j: (layer, 0, j)),
            tok, tok, row0, row0, row_next, row_next,
        ],
        out_specs=pl.BlockSpec((d, tt), lambda i, j: (0, i)),
        out_shape=jax.ShapeDtypeStruct((d, m), f32),
        scratch_shapes=[pltpu.VMEM((PEER_ET, tt), bf16), pltpu.VMEM((2, PEER_ET, tt), f32)],
        compiler_params=_cparams(("parallel", "arbitrary")),
        name="peer_experts",
    )(xnt, u_tab, vt_tab, s2, e2, thr, c1, thr, c1)


def _add_transposed_body(x_ref, dt_ref, o_ref):
    o_ref[...] = x_ref[...] + dt_ref[...].T


def _add_transposed_split_body(x_ref, dt_ref, op_ref, os_ref, *, n_prompt_blocks):
    y = x_ref[...] + dt_ref[...].T
    is_sample = pl.program_id(0) >= n_prompt_blocks

    @pl.when(jnp.logical_not(is_sample))
    def _():
        op_ref[...] = y

    @pl.when(is_sample)
    def _():
        os_ref[...] = y


def add_transposed_split(x, dt, *, n_prompt_blocks):
    m, d = x.shape
    mp = n_prompt_blocks * TB
    assert m == mp + TB
    return pl.pallas_call(
        functools.partial(_add_transposed_split_body, n_prompt_blocks=n_prompt_blocks),
        grid=(m // TB,),
        in_specs=[pl.BlockSpec((TB, d), lambda i: (i, 0)), pl.BlockSpec((d, TB), lambda i: (0, i))],
        out_specs=[pl.BlockSpec((TB, d), lambda i: (jnp.minimum(i, n_prompt_blocks - 1), 0)),
                   pl.BlockSpec((TB, d), lambda i: (0, 0))],
        out_shape=[jax.ShapeDtypeStruct((mp, d), f32), jax.ShapeDtypeStruct((TB, d), f32)],
        compiler_params=_cparams(("arbitrary",)),
        name="add_transposed_split",
    )(x, dt)


def add_transposed(x, dt, *, tm):
    m, d = x.shape
    return pl.pallas_call(
        _add_transposed_body,
        grid=(m // tm,),
        in_specs=[pl.BlockSpec((tm, d), lambda i: (i, 0)), pl.BlockSpec((d, tm), lambda i: (0, i))],
        out_specs=pl.BlockSpec((tm, d), lambda i: (i, 0)),
        out_shape=jax.ShapeDtypeStruct((m, d), f32),
        compiler_params=_cparams(("parallel",)),
        name="add_transposed",
    )(x, dt)


def _to_blocks(x):
    b, t, c = x.shape
    return x.reshape(b, t // TB, NSUB, SUBLEN, c).transpose(0, 1, 3, 2, 4).reshape(b * t, c)


def _from_blocks(y, b):
    m, c = y.shape
    t = m // b
    return y.reshape(b, t // TB, SUBLEN, NSUB, c).transpose(0, 1, 3, 2, 4).reshape(b, t, c)


def _layer(x, l, p, wb, tiles, dims, state, final):
    n_seq, blocks_per_seq, n_stream = dims
    n_prompt_blocks = n_seq * blocks_per_seq
    mp = n_prompt_blocks * TB
    pbias, sbc, sbn = tiles
    cache_k, cache_v, rg_h0, rg_conv0, s5_re0, s5_im0 = state
    tm = 3 * TB if x.shape[0] % (3 * TB) == 0 else TB
    tm_big = 1408 if x.shape[0] % 1408 == 0 else tm
    lam_init = 0.8 - 0.6 * math.exp(-0.3 * l)
    lam = (jnp.exp(jnp.sum(p['lam_q1'] * p['lam_k1'])) - jnp.exp(jnp.sum(p['lam_q2'] * p['lam_k2']))
           + lam_init).reshape(1)

    proj = norm_matmul(x, p['norm1_g'], wb['w_in'], layer=l, tm=tm_big, tn=1024, out_dtype=f32)
    qn, kb, vb, kn_p, kn_s, v_p, v_s = qk_norm(proj, p['qn_g'], p['kn_g'], n_prompt_blocks=n_prompt_blocks)

    o_att = prompt_attention(qn, kb, vb, pbias, lam, p['subln_g'], n_seq=n_seq,
                             blocks_per_seq=blocks_per_seq, lam_init=lam_init)
    stream = lambda a: a.reshape(SUBLEN, n_stream, -1).transpose(1, 0, 2)
    rows = lambda c: c.reshape(c.shape[0], c.shape[1], -1, V_DIM)
    o_s = sample_attention(stream(qn[mp:]), stream(kn_s), stream(v_s), rows(cache_k), rows(cache_v), sbc, sbn,
                           lam, p['subln_g'], layer=l, lam_init=lam_init)
    o_att = jnp.concatenate([o_att, o_s.transpose(1, 0, 2).reshape(TB, ATT_WIDTH)], axis=0)

    head_s = rg_conv0.transpose(1, 0, 2).reshape((CONV_W - 1) * NSUB, RG_WIDTH)
    hg, h_tail, c_tail = rglru_mixer(proj, head_s, rg_h0, p['rg_conv_w'], p['rg_conv_b'], p['rg_wa'],
                                     p['rg_wx'], p['rg_ba'], p['rg_bx'], p['rg_lambda'],
                                     blocks_per_seq=blocks_per_seq, n_prompt_blocks=n_prompt_blocks)
    disc = _s5_discretise(p['s5_lam_re'], p['s5_lam_im'], p['s5_log_step'], p['s5_b_re'], p['s5_b_im'],
                          p['s5_c_re'], p['s5_c_im'])
    gs, st_re, st_im = s5_mixer(proj, s5_re0.reshape(NSUB, S5_LANES), s5_im0.reshape(NSUB, S5_LANES), disc,
                                p['s5_d'], wb['s5_glu_w'], p['s5_glu_b'], layer=l,
                                blocks_per_seq=blocks_per_seq, n_prompt_blocks=n_prompt_blocks)

    merged = gated_merge(o_att, hg, gs, wb['w_att_o'], wb['w_rg_o'], wb['w_s5_o'], proj, p['b_gate'],
                         layer=l, tm=tm_big, tn=512)
    x = residual_matmul(x, merged, wb['w_out'], layer=l, tm=tm_big, tn=1024)

    q, xn2 = norm_matmul(x, p['norm2_g'], wb['peer_wq'], layer=l, tm=tm, tn=1024, out_dtype=bf16,
                         emit_xn=True)
    s2, e2, thr, c1 = peer_route(q, p['peer_subkeys'].astype(bf16))
    peer_t = peer_experts(xn2.T, wb['peer_u'], wb['peer_vt'], s2, e2, thr, c1, layer=l, tt=tm)
    if final:
        x = add_transposed_split(x, peer_t, n_prompt_blocks=n_prompt_blocks)
    else:
        x = add_transposed(x, peer_t, tm=TB)

    last = jnp.arange(n_seq) * blocks_per_seq + blocks_per_seq - 1
    hl = CONV_W - 1
    outs_p = (
        _from_blocks(kn_p, n_seq).reshape(n_seq, -1, N_HEADS, 2 * QK_DIM),
        _from_blocks(v_p, n_seq).reshape(n_seq, -1, N_HEADS, V_DIM),
        h_tail[last, NSUB - 1],
        c_tail[last].reshape(n_seq, hl, NSUB, RG_WIDTH)[:, :, NSUB - 1],
        st_re[last, NSUB - 1].reshape(n_seq, S5_GROUPS, S5_STATE),
        st_im[last, NSUB - 1].reshape(n_seq, S5_GROUPS, S5_STATE),
    )
    outs_s = (
        stream(kn_s).reshape(n_stream, SUBLEN, N_HEADS, 2 * QK_DIM),
        stream(v_s).reshape(n_stream, SUBLEN, N_HEADS, V_DIM),
        h_tail[n_prompt_blocks],
        c_tail[n_prompt_blocks].reshape(hl, NSUB, RG_WIDTH).transpose(1, 0, 2),
        st_re[n_prompt_blocks].reshape(n_stream, S5_GROUPS, S5_STATE),
        st_im[n_prompt_blocks].reshape(n_stream, S5_GROUPS, S5_STATE),
    )
    return x, outs_p, outs_s


def kernel(x_prompt, x_sample, cache_k, cache_v, state_rg_h, state_rg_conv, state_s5_re, state_s5_im, rel_bias, norm1_g, w_in, b_gate, qn_g, kn_g, lam_q1, lam_k1, lam_q2, lam_k2, subln_g, w_att_o, rg_conv_w, rg_conv_b, rg_wa, rg_ba, rg_wx, rg_bx, rg_lambda, w_rg_o, s5_lam_re, s5_lam_im, s5_log_step, s5_b_re, s5_b_im, s5_c_re, s5_c_im, s5_d, s5_glu_w, s5_glu_b, w_s5_o, w_out, norm2_g, peer_wq, peer_subkeys, peer_u, peer_v):
    n_seq, t_prompt, _ = x_prompt.shape
    n_stream, t_sample, _ = x_sample.shape
    depth = w_in.shape[0]
    assert n_stream == NSUB and t_sample == SUBLEN and t_prompt % TB == 0
    blocks_per_seq = t_prompt // TB
    dims = (n_seq, blocks_per_seq, n_stream)
    past = cache_k.shape[2]

    small = dict(norm1_g=norm1_g, b_gate=b_gate, qn_g=qn_g, kn_g=kn_g, lam_q1=lam_q1,
                 lam_k1=lam_k1, lam_q2=lam_q2, lam_k2=lam_k2, subln_g=subln_g,
                 rg_conv_w=rg_conv_w, rg_conv_b=rg_conv_b, rg_wa=rg_wa, rg_ba=rg_ba, rg_wx=rg_wx,
                 rg_bx=rg_bx, rg_lambda=rg_lambda, s5_lam_re=s5_lam_re,
                 s5_lam_im=s5_lam_im, s5_log_step=s5_log_step, s5_b_re=s5_b_re, s5_b_im=s5_b_im,
                 s5_c_re=s5_c_re, s5_c_im=s5_c_im, s5_d=s5_d, s5_glu_b=s5_glu_b,
                 norm2_g=norm2_g, peer_subkeys=peer_subkeys)
    wb = dict(w_in=w_in.astype(bf16), w_att_o=w_att_o.astype(bf16), w_rg_o=w_rg_o.astype(bf16),
              w_s5_o=w_s5_o.astype(bf16), w_out=w_out.astype(bf16), s5_glu_w=s5_glu_w.astype(bf16),
              peer_wq=peer_wq.astype(bf16), peer_u=peer_u.astype(bf16),
              peer_vt=jnp.swapaxes(peer_v, 1, 2).astype(bf16))

    tiles = bias_tiles(rel_bias, past)
    x = jnp.concatenate([_to_blocks(x_prompt), x_sample.transpose(1, 0, 2).reshape(TB, D_MODEL)], axis=0)
    outs_p, outs_s = [], []
    for l in range(depth):
        p = {k: v[l] for k, v in small.items()}
        state = (cache_k, cache_v, state_rg_h[l], state_rg_conv[l], state_s5_re[l], state_s5_im[l])
        x, op, os_ = _layer(x, l, p, wb, tiles, dims, state, final=(l == depth - 1))
        outs_p.append(op)
        outs_s.append(os_)

    x_prompt_rows, x_sample_rows = x
    y_prompt = _from_blocks(x_prompt_rows, n_seq)
    y_sample = x_sample_rows.reshape(SUBLEN, n_stream, D_MODEL).transpose(1, 0, 2)
    stack = lambda outs, i: jnp.stack([o[i] for o in outs])
    return (y_prompt, y_sample) + tuple(stack(outs_p, i) for i in range(6)) + tuple(stack(outs_s, i)
                                                                                    for i in range(6))
```
